```python
import jax, jax.numpy as jnp
from jax import lax
import numpy as np

D_MODEL = 2048
BATCH = 2
SEQ = 8192
DEPTH = 4

GRID_W = 64
CTX_LEN = 256
HEAD_DIM = 128
MIX_WIDTH = D_MODEL
NA_HEADS = D_MODEL // (2 * HEAD_DIM)
GDN_HEADS = D_MODEL // (2 * HEAD_DIM)
NA_W = NA_HEADS * HEAD_DIM
GDN_W = GDN_HEADS * HEAD_DIM
NA_KR = 8
NA_KC = 16
NA_QCOLS = 16
NA_BAND = NA_KC + NA_QCOLS
GDN_CHUNK = 64
GDN_CONV = 5
ROPE_THETA = 10000.0
PEER_HEADS = 8
PEER_TOPK = 16
PEER_NKEYS = 128
PEER_NEXPERTS = PEER_NKEYS * PEER_NKEYS
PEER_QDIM = 256
PEER_BLOCK = 64
N_MOD = 6
LN_EPS = 1e-6
NEG_INF = -1e30
DEEPNORM_ALPHA = (2 * DEPTH) ** 0.25
DEEPNORM_BETA = (8 * DEPTH) ** -0.25
PROJ_SIZES = (NA_W, NA_W, NA_W, 3 * GDN_W, GDN_W, 2 * GDN_HEADS, 2 * GDN_HEADS)
PROJ_WIDTH = sum(PROJ_SIZES)

kernel_name = "hybrid_na_gdn_peer_dit_block"


def _layer_norm(x, g, b):
    xf = x.astype(jnp.float32)
    mu = jnp.mean(xf, axis=-1, keepdims=True)
    var = jnp.mean(jnp.square(xf - mu), axis=-1, keepdims=True)
    return ((xf - mu) * lax.rsqrt(var + LN_EPS) * g.astype(jnp.float32) + b.astype(jnp.float32)).astype(x.dtype)


def _l2norm(x):
    return x * lax.rsqrt(jnp.sum(x * x, axis=-1, keepdims=True) + 1e-6)


def _axial_rope_tables(seq_len):
    t = jnp.arange(seq_len)
    row = (t // GRID_W).astype(jnp.float32)
    col = (t % GRID_W).astype(jnp.float32)
    n_freq = HEAD_DIM // 4
    inv = ROPE_THETA ** (-jnp.arange(n_freq, dtype=jnp.float32) / n_freq)
    ang = jnp.stack([row[:, None] * inv, col[:, None] * inv], axis=1)
    return jnp.cos(ang), jnp.sin(ang)


def _apply_axial_rope(x, cos, sin):
    B, S, H, D = x.shape
    xs = x.reshape(B, S, H, 2, 2, D // 4)
    x1, x2 = xs[..., 0, :], xs[..., 1, :]
    c = cos[None, :, None]
    s = sin[None, :, None]
    out = jnp.stack([x1 * c - x2 * s, x2 * c + x1 * s], axis=-2)
    return out.reshape(B, S, H, D).astype(x.dtype)


def _dwconv_centred(x, w):
    k = w.shape[0]
    return lax.conv_general_dilated(x, w[:, None, :].astype(x.dtype), window_strides=(1,),
                                    padding=[(k // 2, k // 2)], dimension_numbers=('NWC', 'WIO', 'NWC'),
                                    feature_group_count=x.shape[-1])


def _na_column_tables():
    n_cb = GRID_W // NA_QCOLS
    qcol = np.arange(GRID_W).reshape(n_cb, NA_QCOLS)
    win_start = np.clip(qcol - NA_KC // 2, 0, GRID_W - NA_KC)
    band_start = np.clip(win_start[:, 0], 0, GRID_W - NA_BAND)
    kcol = band_start[:, None] + np.arange(NA_BAND)
    valid = (kcol[:, None, :] >= win_start[..., None]) & (kcol[:, None, :] < win_start[..., None] + NA_KC)
    dc = np.clip(kcol[:, None, :] - qcol[..., None] + NA_KC - 1, 0, 2 * NA_KC - 2)
    return kcol.astype(np.int32), dc.astype(np.int32), valid


def _neighbourhood_attention(q, k, v, k_ctx, v_ctx, rpb):
    B, S, H, D = q.shape
    rows = S // GRID_W
    kr = min(NA_KR, rows)
    n_cb = GRID_W // NA_QCOLS
    col_idx, dc_idx, col_valid = _na_column_tables()
    qg = (q * D ** -0.5).reshape(B, rows, GRID_W, H, D)
    kg = k.reshape(B, rows, GRID_W, H, D)
    vg = v.reshape(B, rows, GRID_W, H, D)
    k_ctx_s = k_ctx

    def row_block(r):
        rs = jnp.clip(r - kr // 2, 0, rows - kr)
        q_r = lax.dynamic_index_in_dim(qg, r, axis=1, keepdims=False).reshape(B, n_cb, NA_QCOLS, H, D)
        k_r = lax.dynamic_slice_in_dim(kg, rs, kr, axis=1)
        v_r = lax.dynamic_slice_in_dim(vg, rs, kr, axis=1)
        k_band = jnp.take(k_r, col_idx, axis=2)
        v_band = jnp.take(v_r, col_idx, axis=2)
        s_win = jnp.einsum('bjqhd,bijmhd->bhjqim', q_r, k_band).astype(jnp.float32)
        dr = rs + jnp.arange(kr) - r + (NA_KR - 1)
        bias = rpb[:, dr[:, None, None, None], dc_idx[None]]
        bias = bias.transpose(0, 2, 3, 1, 4).astype(jnp.float32)
        s_win = jnp.where(col_valid[:, :, None, :], s_win + bias, NEG_INF)
        s_ctx = jnp.einsum('bjqhd,bchd->bhjqc', q_r, k_ctx_s).astype(jnp.float32)
        s_all = jnp.concatenate([s_win.reshape(B, H, n_cb, NA_QCOLS, kr * NA_BAND), s_ctx], axis=-1)
        p = jax.nn.softmax(s_all, axis=-1).astype(v.dtype)
        p_win = p[..., :kr * NA_BAND].reshape(B, H, n_cb, NA_QCOLS, kr, NA_BAND)
        p_ctx = p[..., kr * NA_BAND:]
        o = (jnp.einsum('bhjqim,bijmhd->bjqhd', p_win, v_band)
             + jnp.einsum('bhjqc,bchd->bjqhd', p_ctx, v_ctx))
        return o.reshape(B, GRID_W, H, D)

    out = lax.map(row_block, jnp.arange(rows))
    return out.transpose(1, 0, 2, 3, 4).reshape(B, S, H, D)


def _ctx_attention(q, k, v):
    s = jnp.einsum('bqhd,bkhd->bhqk', q, k).astype(jnp.float32) * HEAD_DIM ** -0.5
    p = jax.nn.softmax(s, axis=-1).astype(v.dtype)
    return jnp.einsum('bhqk,bkhd->bqhd', p, v)


def _gated_delta_chunked(q, k, v, g, beta, s0):
    B, H, L, Dk = q.shape
    Dv = v.shape[-1]
    n = L // GDN_CHUNK
    rs = lambda t: t.reshape(B, H, n, GDN_CHUNK, *t.shape[3:])
    q, k, v, g, beta = rs(q) * Dk ** -0.5, rs(k), rs(v), rs(g), rs(beta)
    gc = jnp.cumsum(g, axis=-1)
    idx = jnp.arange(GDN_CHUNK)
    lower = idx[:, None] >= idx[None, :]
    strict = idx[:, None] > idx[None, :]
    diff = gc[..., :, None] - gc[..., None, :]
    decay = jnp.where(lower, jnp.exp(jnp.where(lower, diff, 0.0)), 0.0)
    kb = k * beta[..., None]
    vb = v * beta[..., None]
    a_mat = jnp.eye(GDN_CHUNK, dtype=jnp.float32) + jnp.where(strict, jnp.einsum('bhncd,bhnsd->bhncs', kb, k) * decay, 0.0)
    rhs = jnp.concatenate([vb, kb * jnp.exp(gc)[..., None]], axis=-1)
    sol = lax.linalg.triangular_solve(a_mat, rhs, left_side=True, lower=True, unit_diagonal=True)
    u, w = sol[..., :Dv], sol[..., Dv:]
    qk = jnp.einsum('bhncd,bhnsd->bhncs', q, k) * decay
    q_dec = q * jnp.exp(gc)[..., None]
    k_dec = k * jnp.exp(gc[..., -1:] - gc)[..., None]
    g_last = jnp.exp(gc[..., -1])

    def step(state, xs):
        u_c, w_c, qk_c, qd_c, kd_c, gl_c = xs
        v_new = u_c - jnp.einsum('bhcd,bhdv->bhcv', w_c, state)
        o = jnp.einsum('bhcd,bhdv->bhcv', qd_c, state) + jnp.einsum('bhcs,bhsv->bhcv', qk_c, v_new)
        state = state * gl_c[..., None, None] + jnp.einsum('bhcd,bhcv->bhdv', kd_c, v_new)
        return state, o

    mv = lambda t: jnp.moveaxis(t, 2, 0)
    s_fin, o = lax.scan(step, s0, (mv(u), mv(w), mv(qk), mv(q_dec), mv(k_dec), mv(g_last)))
    return jnp.moveaxis(o, 0, 2).reshape(B, H, L, Dv), s_fin


def _gdn_bidirectional(q, k, v, g, beta, s0_f, s0_b):
    o_f, s_f = _gated_delta_chunked(q, k, v, g[0], beta[0], s0_f)
    fl = lambda t: jnp.flip(t, axis=2)
    o_b, s_b = _gated_delta_chunked(fl(q), fl(k), fl(v), fl(g[1]), fl(beta[1]), s0_b)
    return o_f + fl(o_b), s_f, s_b


def _gdn_inputs(qkv, a, b, conv_w, a_log, dt_bias, rope_cos, rope_sin):
    B, L, _ = qkv.shape
    qkv = jax.nn.silu(_dwconv_centred(qkv, conv_w))
    q, k, v = [t.reshape(B, L, GDN_HEADS, HEAD_DIM).astype(jnp.float32) for t in jnp.split(qkv, 3, axis=-1)]
    q, k = _l2norm(q), _l2norm(k)
    if rope_cos is not None:
        q = _apply_axial_rope(q, rope_cos, rope_sin)
        k = _apply_axial_rope(k, rope_cos, rope_sin)
    g = -jnp.exp(a_log.astype(jnp.float32)) * jax.nn.softplus(
        a.reshape(B, L, 2, GDN_HEADS).astype(jnp.float32) + dt_bias.astype(jnp.float32))
    beta = jax.nn.sigmoid(b.reshape(B, L, 2, GDN_HEADS).astype(jnp.float32))
    bhl = lambda t: t.transpose(0, 2, 1, 3)
    return bhl(q), bhl(k), bhl(v), g.transpose(2, 0, 3, 1), beta.transpose(2, 0, 3, 1)


def _gdn_output(o, z, norm_g):
    B, H, L, D = o.shape
    o = o.transpose(0, 2, 1, 3)
    o = o * lax.rsqrt(jnp.mean(o * o, axis=-1, keepdims=True) + 1e-6) * norm_g.astype(jnp.float32)
    o = o.reshape(B, L, H * D) * jax.nn.silu(z.astype(jnp.float32))
    return o.astype(z.dtype)


def _mixer(h, h_c, w_in, w_out, rpb, conv_w, a_log, dt_bias, norm_g, rope_cos, rope_sin, with_ctx_out):
    B, S, _ = h.shape
    C = h_c.shape[1]
    splits = np.cumsum(PROJ_SIZES)[:-1].tolist()
    na_q, na_k, na_v, g_qkv, g_z, g_a, g_b = jnp.split(h @ w_in, splits, axis=-1)
    nc_q, nc_k, nc_v, gc_qkv, gc_z, gc_a, gc_b = jnp.split(h_c @ w_in, splits, axis=-1)
    heads = lambda t, L: t.reshape(B, L, NA_HEADS, HEAD_DIM)
    k_c, v_c = heads(nc_k, C), heads(nc_v, C)
    na_out = _neighbourhood_attention(heads(na_q, S), heads(na_k, S), heads(na_v, S), k_c, v_c, rpb).reshape(B, S, NA_W)
    cq, ck, cv, cg, cb = _gdn_inputs(gc_qkv, gc_a, gc_b, conv_w, a_log, dt_bias, None, None)
    zeros = jnp.zeros((B, GDN_HEADS, HEAD_DIM, HEAD_DIM), jnp.float32)
    o_ctx, s_f, s_b = _gdn_bidirectional(cq, ck, cv, cg, cb, zeros, zeros)
    lq, lk, lv, lg, lb = _gdn_inputs(g_qkv, g_a, g_b, conv_w, a_log, dt_bias, rope_cos, rope_sin)
    o_lat, _, _ = _gdn_bidirectional(lq, lk, lv, lg, lb, s_f, s_b)
    gdn_out = _gdn_output(o_lat, g_z, norm_g)
    out = jnp.concatenate([na_out, gdn_out], axis=-1) @ w_out
    if not with_ctx_out:
        return out, None
    na_ctx = _ctx_attention(heads(nc_q, C), k_c, v_c).reshape(B, C, NA_W)
    out_c = jnp.concatenate([na_ctx, _gdn_output(o_ctx, gc_z, norm_g)], axis=-1) @ w_out
    return out, out_c


def _peer(h, wq, subkeys, u_tab, v_tab):
    B, L, D = h.shape
    q = (h @ wq).reshape(B, L, PEER_HEADS, 2, PEER_QDIM // 2)
    s = jnp.einsum('blhpd,pnd->blhpn', q, subkeys).astype(jnp.float32)
    top_s, top_i = lax.top_k(s, PEER_TOPK)
    cand = top_s[..., 0, :, None] + top_s[..., 1, None, :]
    cand_id = top_i[..., 0, :, None] * PEER_NKEYS + top_i[..., 1, None, :]
    best_s, best_pos = lax.top_k(cand.reshape(B, L, PEER_HEADS, PEER_TOPK * PEER_TOPK), PEER_TOPK)
    experts = jnp.take_along_axis(cand_id.reshape(B, L, PEER_HEADS, PEER_TOPK * PEER_TOPK), best_pos, axis=-1)
    gates = jax.nn.softmax(best_s, axis=-1).astype(h.dtype)
    n_blk = (B * L) // PEER_BLOCK
    hb = h.reshape(n_blk, PEER_BLOCK, D)
    eb = experts.reshape(n_blk, PEER_BLOCK, PEER_HEADS * PEER_TOPK)
    gb = gates.reshape(n_blk, PEER_BLOCK, PEER_HEADS * PEER_TOPK)

    def block(args):
        h_t, e_t, g_t = args
        act = jax.nn.gelu(jnp.einsum('td,ted->te', h_t, u_tab[e_t]), approximate=False) * g_t
        return jnp.einsum('te,ted->td', act, v_tab[e_t])

    return lax.map(block, (hb, eb, gb)).reshape(B, L, D)


def setup_inputs(seed: int = 0) -> dict:
    key = jax.random.key(seed)
    ks = jax.random.split(key, 20)
    nrm = lambda k, shape, s: jax.random.normal(k, shape, jnp.float32) * s
    dt = jnp.exp(jax.random.uniform(ks[11], (DEPTH, 2, GDN_HEADS), jnp.float32, minval=np.log(1e-3), maxval=np.log(1e-1)))
    return {
        "x": nrm(ks[0], (BATCH, SEQ, D_MODEL), 1.0),
        "c": nrm(ks[1], (BATCH, D_MODEL), 1.0),
        "ctx": nrm(ks[2], (BATCH, CTX_LEN, D_MODEL), 1.0),
        "c_ctx": nrm(ks[3], (D_MODEL,), 1.0),
        "w_mod": nrm(ks[4], (DEPTH, D_MODEL, N_MOD * D_MODEL), D_MODEL ** -0.5),
        "b_mod": nrm(ks[5], (DEPTH, N_MOD * D_MODEL), 0.02),
        "w_in": nrm(ks[6], (DEPTH, D_MODEL, PROJ_WIDTH), D_MODEL ** -0.5),
        "w_out": nrm(ks[7], (DEPTH, MIX_WIDTH, D_MODEL), DEEPNORM_BETA * MIX_WIDTH ** -0.5),
        "na_rpb": nrm(ks[8], (DEPTH, NA_HEADS, 2 * NA_KR - 1, 2 * NA_KC - 1), 0.1),
        "gdn_conv": nrm(ks[9], (DEPTH, GDN_CONV, 3 * GDN_W), GDN_CONV ** -0.5),
        "gdn_a_log": jnp.log(jax.random.uniform(ks[10], (DEPTH, 2, GDN_HEADS), jnp.float32, minval=1.0, maxval=16.0)),
        "gdn_dt_bias": dt + jnp.log(-jnp.expm1(-dt)),
        "gdn_norm_g": 1.0 + nrm(ks[12], (DEPTH, HEAD_DIM), 0.1),
        "peer_wq": nrm(ks[13], (DEPTH, D_MODEL, PEER_HEADS * PEER_QDIM), D_MODEL ** -0.5),
        "peer_subkeys": nrm(ks[14], (DEPTH, 2, PEER_NKEYS, PEER_QDIM // 2), (PEER_QDIM // 2) ** -0.5),
        "peer_u": nrm(ks[15], (DEPTH, PEER_NEXPERTS, D_MODEL), D_MODEL ** -0.5),
        "peer_v": nrm(ks[16], (DEPTH, PEER_NEXPERTS, D_MODEL), DEEPNORM_BETA),
        "ln_g": 1.0 + nrm(ks[17], (DEPTH, 2, D_MODEL), 0.1),
        "ln_b": nrm(ks[18], (DEPTH, 2, D_MODEL), 0.02),
    }


def reference(x, c, ctx, c_ctx, w_mod, b_mod, w_in, w_out, na_rpb, gdn_conv, gdn_a_log, gdn_dt_bias,
              gdn_norm_g, peer_wq, peer_subkeys, peer_u, peer_v, ln_g, ln_b):
    S = x.shape[1]
    rope_cos, rope_sin = _axial_rope_tables(S)
    silu_c = jax.nn.silu(c)
    silu_cc = jax.nn.silu(c_ctx)
    for l in range(DEPTH):
        with_ctx = l < DEPTH - 1
        mod = silu_c @ w_mod[l] + b_mod[l]
        mod_c = silu_cc @ w_mod[l] + b_mod[l]
        sh_a, sc_a, g_a, sh_f, sc_f, g_f = jnp.split(mod[:, None, :], N_MOD, axis=-1)
        csh_a, csc_a, cg_a, csh_f, csc_f, cg_f = jnp.split(mod_c, N_MOD, axis=-1)
        h = x * (1.0 + sc_a) + sh_a
        h_c = ctx * (1.0 + csc_a) + csh_a
        mix, mix_c = _mixer(h, h_c, w_in[l], w_out[l], na_rpb[l], gdn_conv[l], gdn_a_log[l], gdn_dt_bias[l],
                            gdn_norm_g[l], rope_cos, rope_sin, with_ctx)
        x = _layer_norm(DEEPNORM_ALPHA * x + g_a * mix, ln_g[l, 0], ln_b[l, 0])
        h = x * (1.0 + sc_f) + sh_f
        x = _layer_norm(DEEPNORM_ALPHA * x + g_f * _peer(h, peer_wq[l], peer_subkeys[l], peer_u[l], peer_v[l]),
                        ln_g[l, 1], ln_b[l, 1])
        if with_ctx:
            ctx = _layer_norm(DEEPNORM_ALPHA * ctx + cg_a * mix_c, ln_g[l, 0], ln_b[l, 0])
            h_c = ctx * (1.0 + csc_f) + csh_f
            ctx = _layer_norm(DEEPNORM_ALPHA * ctx + cg_f * _peer(h_c, peer_wq[l], peer_subkeys[l], peer_u[l], peer_v[l]),
                              ln_g[l, 1], ln_b[l, 1])
    return x
```

```python
import functools

import numpy as np
import jax
import jax.numpy as jnp
from jax import lax
from jax.experimental import pallas as pl
from jax.experimental.pallas import tpu as pltpu

F32 = jnp.float32
BF16 = jnp.bfloat16
HIGHEST = lax.Precision.HIGHEST

D_MODEL = 2048
HEAD_DIM = 128
N_HEADS = 8
GROUP_W = N_HEADS * HEAD_DIM
GRID_W = 64
NA_KR = 8
NA_KC = 16
NA_QROWS = 8
NA_KROWS = 16
GDN_CHUNK = 64
GDN_CONV = 5
GDN_HB = 4
ROPE_THETA = 10000.0
PEER_HEADS = 8
PEER_TOPK = 16
PEER_NKEYS = 128
PEER_QDIM = 256
N_MOD = 6
LN_EPS = 1e-6
NEG_INF = -1e30
DEPTH_FOR_DEEPNORM = 4
DEEPNORM_ALPHA = (2 * DEPTH_FOR_DEEPNORM) ** 0.25
P_MAIN_W = 7 * GROUP_W
VMEM_LIMIT = 56 * 1024 * 1024

TM_PROJ = 512
TN_PROJ = 512
TM_OUT = 256
TM_PREP = 256
TM_PEER = 512
TE_PEER = 1024


def _cparams(sem):
    return pltpu.CompilerParams(dimension_semantics=sem, vmem_limit_bytes=VMEM_LIMIT)


def _sigmoid(x):
    return 1.0 / (1.0 + jnp.exp(-x))


def _silu(x):
    return x * _sigmoid(x)


def _dot(a, b):
    return jnp.dot(a, b, preferred_element_type=F32)


def _dot_t(a, b):
    return lax.dot_general(a, b, (((1,), (1,)), ((), ())), preferred_element_type=F32)


def _split_bf16(a):
    hi = a.astype(BF16)
    lo = (a - hi.astype(F32)).astype(BF16)
    return hi, lo


def _dot3_split(ah, al, bh, bl):
    return _dot(ah, bh) + (_dot(ah, bl) + _dot(al, bh))


def _layer_norm_rows(y, g, b):
    mu = jnp.mean(y, axis=-1, keepdims=True)
    yc = y - mu
    var = jnp.mean(yc * yc, axis=-1, keepdims=True)
    return yc * lax.rsqrt(var + LN_EPS) * g + b


def _mod_kernel(c_ref, w_ref, b_ref, o_ref):
    s = _silu(c_ref[...])
    o_ref[0] = jnp.dot(s, w_ref[0], precision=HIGHEST, preferred_element_type=F32) + b_ref[0]


def _mod_call(cs, w_mod, b_mod):
    depth, d, n = w_mod.shape
    tn = 1024
    return pl.pallas_call(
        _mod_kernel,
        grid=(depth, n // tn),
        in_specs=[
            pl.BlockSpec((8, d), lambda l, j: (0, 0)),
            pl.BlockSpec((1, d, tn), lambda l, j: (l, 0, j)),
            pl.BlockSpec((1, 1, tn), lambda l, j: (l, 0, j)),
        ],
        out_specs=pl.BlockSpec((1, 8, tn), lambda l, j: (l, 0, j)),
        out_shape=jax.ShapeDtypeStruct((depth, 8, n), F32),
        compiler_params=_cparams(("arbitrary", "arbitrary")),
        name="adaln_mod",
    )(cs, w_mod, b_mod.reshape(depth, 1, n))


def _inproj_kernel(x_ref, sh_ref, sc_ref, w_ref, wab_ref, p_ref, pab_ref, h_scr, *, tiles_per_seq, nb):
    i = pl.program_id(0)
    j = pl.program_id(1)

    @pl.when(j == 0)
    def _():
        r = jnp.minimum(i // tiles_per_seq, nb)
        sh = sh_ref[pl.ds(r, 1), :]
        sc = sc_ref[pl.ds(r, 1), :]
        hb = (x_ref[...] * (1.0 + sc) + sh).astype(BF16)
        h_scr[...] = hb
        pab_ref[...] = _dot(hb, wab_ref[...])

    p_ref[...] = _dot(h_scr[...], w_ref[...])


def _inproj_call(xa, mod_l, w_main, w_ab, *, seq, nb):
    t_all, d = xa.shape
    n = w_main.shape[1]
    nab = w_ab.shape[1]
    tm, tn = TM_PROJ, TN_PROJ
    kern = functools.partial(_inproj_kernel, tiles_per_seq=seq // tm, nb=nb)
    return pl.pallas_call(
        kern,
        grid=(t_all // tm, n // tn),
        in_specs=[
            pl.BlockSpec((tm, d), lambda i, j: (i, 0)),
            pl.BlockSpec((8, d), lambda i, j: (0, 0)),
            pl.BlockSpec((8, d), lambda i, j: (0, 1)),
            pl.BlockSpec((d, tn), lambda i, j: (0, j)),
            pl.BlockSpec((d, nab), lambda i, j: (0, 0)),
        ],
        out_specs=[
            pl.BlockSpec((tm, tn), lambda i, j: (i, j)),
            pl.BlockSpec((tm, nab), lambda i, j: (i, 0)),
        ],
        out_shape=[
            jax.ShapeDtypeStruct((t_all, n), F32),
            jax.ShapeDtypeStruct((t_all, nab), F32),
        ],
        scratch_shapes=[pltpu.VMEM((tm, d), BF16)],
        compiler_params=_cparams(("arbitrary", "arbitrary")),
        name="in_proj",
    )(xa, mod_l, mod_l, w_main, w_ab)


def _na_bias_index_tables(nrows):
    rq, rk = NA_QROWS, NA_KROWS
    big = 1 << 20
    cfg = [(0, 0, nrows), (8, 4, big), (nrows - rq, nrows - rk, nrows)]
    dr = np.zeros((3, rq, rk), np.int32)
    rv = np.zeros((3, rq, rk), bool)
    for v, (r0, start, nr) in enumerate(cfg):
        r = r0 + np.arange(rq)[:, None]
        kr = start + np.arange(rk)[None, :]
        rs = np.clip(r - NA_KR // 2, 0, nr - NA_KR)
        rv[v] = (kr >= rs) & (kr < rs + NA_KR)
        dr[v] = np.clip(kr - r + NA_KR - 1, 0, 2 * NA_KR - 2)
    qc = np.arange(GRID_W)[:, None]
    kc = np.arange(GRID_W)[None, :]
    ws = np.clip(qc - NA_KC // 2, 0, GRID_W - NA_KC)
    cv = (kc >= ws) & (kc < ws + NA_KC)
    dc = np.clip(kc - qc + NA_KC - 1, 0, 2 * NA_KC - 2).astype(np.int32)
    return dr, rv, dc, cv


def _na_bias(rpb_l, nrows):
    dr, rv, dc, cv = _na_bias_index_tables(nrows)
    tc = jnp.where(cv[None, None], rpb_l[:, :, dc], NEG_INF)
    blk = tc[:, dr]
    blk = jnp.where(rv[None, :, :, :, None, None], blk, NEG_INF)
    blk = blk.transpose(0, 1, 2, 4, 3, 5)
    return blk.reshape(N_HEADS, 3, NA_QROWS * GRID_W, NA_KROWS * GRID_W)


def _na_kernel(q_ref, k0, k1, k2, k3, v0, v1, v2, v3, kc_ref, vc_ref, bias_ref, o_ref):
    q = (q_ref[...] * (HEAD_DIM ** -0.5)).astype(BF16)
    kb = 4 * GRID_W
    s = []
    for j, kr in enumerate((k0, k1, k2, k3)):
        s.append(_dot_t(q, kr[...].astype(BF16)) + bias_ref[:, j * kb:(j + 1) * kb])
    s.append(_dot_t(q, kc_ref[...].astype(BF16)))
    m = s[0].max(axis=-1, keepdims=True)
    for t in s[1:]:
        m = jnp.maximum(m, t.max(axis=-1, keepdims=True))
    p = [jnp.exp(t - m) for t in s]
    l = p[0].sum(axis=-1, keepdims=True)
    for t in p[1:]:
        l = l + t.sum(axis=-1, keepdims=True)
    vs = (v0, v1, v2, v3, vc_ref)
    o = _dot(p[0].astype(BF16), vs[0][...].astype(BF16))
    for t, vr in zip(p[1:], vs[1:]):
        o = o + _dot(t.astype(BF16), vr[...].astype(BF16))
    o_ref[...] = o / l


def _na_call(p_all, bias, *, nb, seq, ctx_len):
    t_all = p_all.shape[0]
    nrows = seq // GRID_W
    nrb = nrows // NA_QROWS
    tq = NA_QROWS * GRID_W
    tk = 4 * GRID_W
    assert ctx_len == tk and nrows >= NA_KROWS
    kblocks_per_seq = seq // tk
    max_sb = kblocks_per_seq - 4
    ctx_blk0 = nb * seq // tk

    def sb(r):
        return jnp.clip(2 * r - 1, 0, max_sb)

    def kspec(j, colbase):
        return pl.BlockSpec((tk, HEAD_DIM), lambda h, b, r: (b * kblocks_per_seq + sb(r) + j, colbase + h))

    def variant(r):
        return jnp.where(r == 0, 0, jnp.where(r == nrb - 1, 2, 1))

    in_specs = [pl.BlockSpec((tq, HEAD_DIM), lambda h, b, r: (b * nrb + r, h))]
    in_specs += [kspec(j, N_HEADS) for j in range(4)]
    in_specs += [kspec(j, 2 * N_HEADS) for j in range(4)]
    in_specs += [
        pl.BlockSpec((tk, HEAD_DIM), lambda h, b, r: (ctx_blk0 + b, N_HEADS + h)),
        pl.BlockSpec((tk, HEAD_DIM), lambda h, b, r: (ctx_blk0 + b, 2 * N_HEADS + h)),
        pl.BlockSpec((None, None, tq, NA_KROWS * GRID_W), lambda h, b, r: (h, variant(r), 0, 0)),
    ]
    return pl.pallas_call(
        _na_kernel,
        grid=(N_HEADS, nb, nrb),
        in_specs=in_specs,
        out_specs=pl.BlockSpec((tq, HEAD_DIM), lambda h, b, r: (b * nrb + r, h)),
        out_shape=jax.ShapeDtypeStruct((t_all, GROUP_W), F32),
        compiler_params=_cparams(("arbitrary", "arbitrary", "arbitrary")),
        name="na_attention",
    )(*([p_all] * 11), bias)


def _ctx_attn_kernel(q_ref, k_ref, v_ref, na_in_ref, o_ref):
    del na_in_ref
    q = q_ref[...].astype(BF16)
    s = _dot_t(q, k_ref[...].astype(BF16)) * (HEAD_DIM ** -0.5)
    m = s.max(axis=-1, keepdims=True)
    p = jnp.exp(s - m)
    l = p.sum(axis=-1, keepdims=True)
    o_ref[...] = _dot(p.astype(BF16), v_ref[...].astype(BF16)) / l


def _ctx_attn_call(p_all, na_all, *, nb, seq, ctx_len):
    blk0 = nb * seq // ctx_len
    return pl.pallas_call(
        _ctx_attn_kernel,
        grid=(nb, N_HEADS),
        in_specs=[
            pl.BlockSpec((ctx_len, HEAD_DIM), lambda b, h: (blk0 + b, h)),
            pl.BlockSpec((ctx_len, HEAD_DIM), lambda b, h: (blk0 + b, N_HEADS + h)),
            pl.BlockSpec((ctx_len, HEAD_DIM), lambda b, h: (blk0 + b, 2 * N_HEADS + h)),
            pl.BlockSpec(memory_space=pl.ANY),
        ],
        out_specs=pl.BlockSpec((ctx_len, HEAD_DIM), lambda b, h: (blk0 + b, h)),
        out_shape=jax.ShapeDtypeStruct(na_all.shape, F32),
        input_output_aliases={3: 0},
        compiler_params=_cparams(("arbitrary", "arbitrary")),
        name="ctx_attention",
    )(p_all, p_all, p_all, na_all)


def _rope_tables(seq, ctx_len):
    t = np.arange(seq)
    row = (t // GRID_W).astype(np.float32)
    col = (t % GRID_W).astype(np.float32)
    n_freq = HEAD_DIM // 4
    inv = (ROPE_THETA ** (-np.arange(n_freq, dtype=np.float32) / n_freq)).astype(np.float32)
    ang = jnp.stack([jnp.asarray(row)[:, None] * inv, jnp.asarray(col)[:, None] * inv], axis=1)
    cos, sin = jnp.cos(ang), jnp.sin(ang)
    c = jnp.concatenate([cos, cos], axis=-1).reshape(seq, HEAD_DIM)
    s = jnp.concatenate([-sin, sin], axis=-1).reshape(seq, HEAD_DIM)
    c = jnp.concatenate([c, jnp.ones((ctx_len, HEAD_DIM), F32)], axis=0)
    s = jnp.concatenate([s, jnp.zeros((ctx_len, HEAD_DIM), F32)], axis=0)
    return c, s


def _gdn_prep_kernel(cur_ref, prev_ref, next_ref, pab_ref, cw_ref, alog_ref, dtb_ref, rc_ref, rs_ref,
                     q_ref, k_ref, v_ref, g_ref, b_ref, ext_ref, *, n_lat_tiles, tps, nab_half):
    i = pl.program_id(0)
    tm = TM_PREP
    is_ctx = i >= n_lat_tiles
    first = jnp.logical_or(is_ctx, i % tps == 0)
    last = jnp.logical_or(is_ctx, i % tps == tps - 1)
    ext_ref[8:8 + tm, :] = cur_ref[...]
    ext_ref[0:8, :] = jnp.where(first, 0.0, prev_ref[...])
    ext_ref[8 + tm:16 + tm, :] = jnp.where(last, 0.0, next_ref[...])

    lane = lax.broadcasted_iota(jnp.int32, (tm, HEAD_DIM), 1)
    half0 = (lane % (HEAD_DIM // 2)) < (HEAD_DIM // 4)
    rc = rc_ref[...]
    rs = rs_ref[...]
    base = 8 - GDN_CONV // 2
    outs = (q_ref, k_ref, v_ref)
    for part in range(3):
        for h in range(N_HEADS):
            c0 = part * GROUP_W + h * HEAD_DIM
            acc = cw_ref[0:1, c0:c0 + HEAD_DIM] * ext_ref[base:base + tm, c0:c0 + HEAD_DIM]
            for t in range(1, GDN_CONV):
                acc = acc + cw_ref[t:t + 1, c0:c0 + HEAD_DIM] * ext_ref[base + t:base + t + tm, c0:c0 + HEAD_DIM]
            y = _silu(acc)
            if part < 2:
                y = y * lax.rsqrt(jnp.sum(y * y, axis=-1, keepdims=True) + 1e-6)
                partner = jnp.where(half0, pltpu.roll(y, HEAD_DIM - HEAD_DIM // 4, 1),
                                    pltpu.roll(y, HEAD_DIM // 4, 1))
                y = y * rc + partner * rs
                if part == 0:
                    y = y * (HEAD_DIM ** -0.5)
            outs[part][:, h * HEAD_DIM:(h + 1) * HEAD_DIM] = y

    a = pab_ref[:, 0:nab_half] + dtb_ref[...]
    softplus = jnp.maximum(a, 0.0) + jnp.log1p(jnp.exp(-jnp.abs(a)))
    g_ref[...] = -jnp.exp(alog_ref[...]) * softplus
    b_ref[...] = _sigmoid(pab_ref[:, nab_half:2 * nab_half])


def _gdn_prep_call(p_all, pab, conv_w, alog, dtb, rope_c, rope_s, *, nb, seq, ctx_len):
    t_all = p_all.shape[0]
    tm = TM_PREP
    assert ctx_len == tm
    n_tiles = t_all // tm
    n_lat_tiles = nb * seq // tm
    tps = seq // tm
    nab_half = pab.shape[1] // 2
    qkv_w = 3 * GROUP_W
    last8 = t_all // 8 - 1
    kern = functools.partial(_gdn_prep_kernel, n_lat_tiles=n_lat_tiles, tps=tps, nab_half=nab_half)

    def rope_idx(i):
        return jnp.where(i < n_lat_tiles, i % tps, tps)

    return pl.pallas_call(
        kern,
        grid=(n_tiles,),
        in_specs=[
            pl.BlockSpec((tm, qkv_w), lambda i: (i, 1)),
            pl.BlockSpec((8, qkv_w), lambda i: (jnp.maximum(i * (tm // 8) - 1, 0), 1)),
            pl.BlockSpec((8, qkv_w), lambda i: (jnp.minimum((i + 1) * (tm // 8), last8), 1)),
            pl.BlockSpec((tm, 2 * nab_half), lambda i: (i, 0)),
            pl.BlockSpec((8, qkv_w), lambda i: (0, 0)),
            pl.BlockSpec((1, nab_half), lambda i: (0, 0)),
            pl.BlockSpec((1, nab_half), lambda i: (0, 0)),
            pl.BlockSpec((tm, HEAD_DIM), lambda i: (rope_idx(i), 0)),
            pl.BlockSpec((tm, HEAD_DIM), lambda i: (rope_idx(i), 0)),
        ],
        out_specs=[
            pl.BlockSpec((tm, GROUP_W), lambda i: (i, 0)),
            pl.BlockSpec((tm, GROUP_W), lambda i: (i, 0)),
            pl.BlockSpec((tm, GROUP_W), lambda i: (i, 0)),
            pl.BlockSpec((tm, nab_half), lambda i: (i, 0)),
            pl.BlockSpec((tm, nab_half), lambda i: (i, 0)),
        ],
        out_shape=[
            jax.ShapeDtypeStruct((t_all, GROUP_W), F32),
            jax.ShapeDtypeStruct((t_all, GROUP_W), F32),
            jax.ShapeDtypeStruct((t_all, GROUP_W), F32),
            jax.ShapeDtypeStruct((t_all, nab_half), F32),
            jax.ShapeDtypeStruct((t_all, nab_half), F32),
        ],
        scratch_shapes=[pltpu.VMEM((tm + 16, qkv_w), F32)],
        compiler_params=_cparams(("arbitrary",)),
        name="gdn_prep",
    )(p_all, p_all, p_all, pab, conv_w, alog, dtb, rope_c, rope_s)


def _unit_tri_inverse(n, row, col):
    c = GDN_CHUNK
    eye = (row == col).astype(F32)
    m1 = jnp.logical_and(jnp.logical_and((row >> 1) == (col >> 1), (row & 1) == 1), (col & 1) == 0)
    d = eye - jnp.where(m1, n, 0.0)
    k = 2
    while k < c:
        sh = int(np.log2(2 * k))
        mk = jnp.logical_and((row >> sh) == (col >> sh),
                             jnp.logical_and((row & (2 * k - 1)) >= k, (col & (2 * k - 1)) < k))
        lk = jnp.where(mk, n, 0.0)
        dh, dl = _split_bf16(d)
        lh, ll = _split_bf16(lk)
        x = _dot3_split(lh, ll, dh, dl)
        xh, xl = _split_bf16(x)
        d = d - _dot3_split(dh, dl, xh, xl)
        k *= 2
    return d


def _gdn_scan_kernel(qf, kf, vf, gf, bf, qb, kb, vb, gb, bb, of_ref, ob_ref, st_ref, *, hb):
    s = pl.program_id(2)
    c = GDN_CHUNK

    @pl.when(s == 0)
    def _():
        st_ref[...] = jnp.zeros(st_ref.shape, F32)

    ri = lax.broadcasted_iota(jnp.int32, (c, c), 0)
    ci = lax.broadcasted_iota(jnp.int32, (c, c), 1)
    zpad = jnp.zeros((HEAD_DIM - c, HEAD_DIM), F32)
    for d, (q_ref, k_ref, v_ref, g_ref, be_ref, o_ref) in enumerate(
            ((qf, kf, vf, gf, bf, of_ref), (qb, kb, vb, gb, bb, ob_ref))):
        later, earlier = (ri, ci) if d == 0 else (ci, ri)
        incl = (ri >= ci) if d == 0 else (ri <= ci)
        strict = (ri > ci) if d == 0 else (ri < ci)
        last = c - 1 if d == 0 else 0
        gc_all = jnp.dot(incl.astype(F32), g_ref[...], precision=HIGHEST, preferred_element_type=F32)
        gc_t = jnp.transpose(jnp.concatenate([gc_all, zpad], axis=0))
        beta_all = be_ref[...]
        for hh in range(hb):
            ln = d * hb + hh
            gcc = gc_all[:, ln:ln + 1]
            grow = gc_t[ln:ln + 1, 0:c]
            beta = beta_all[:, ln:ln + 1]
            q = q_ref[:, hh * HEAD_DIM:(hh + 1) * HEAD_DIM]
            k = k_ref[:, hh * HEAD_DIM:(hh + 1) * HEAD_DIM]
            v = v_ref[:, hh * HEAD_DIM:(hh + 1) * HEAD_DIM]
            decay = jnp.where(incl, jnp.exp(jnp.where(incl, gcc - grow, 0.0)), 0.0)
            eg = jnp.exp(gcc)
            glast = gcc[last:last + 1, :]
            kbeta = k * beta
            kk = _dot_t(kbeta, k)
            n = jnp.where(strict, kk * decay, 0.0)
            qk = _dot_t(q, k) * decay
            tinv = _unit_tri_inverse(n, later, earlier)
            rhs = jnp.concatenate([v * beta, kbeta * eg], axis=1)
            th, tl = _split_bf16(tinv)
            rh, rl = _split_bf16(rhs)
            sol = _dot3_split(th, tl, rh, rl)
            u = sol[:, :HEAD_DIM]
            w = sol[:, HEAD_DIM:]
            state = st_ref[ln]
            v_new = u - _dot(w, state)
            o_ref[:, hh * HEAD_DIM:(hh + 1) * HEAD_DIM] = _dot(q * eg, state) + _dot(qk, v_new)
            kd_t = jnp.transpose(jnp.concatenate([k * jnp.exp(glast - gcc), zpad], axis=0))[:, 0:c]
            st_ref[ln] = state * jnp.exp(glast) + _dot(kd_t, v_new)


def _gdn_scan_call(qn, kn, vn, g, beta, *, nb, seq, ctx_len):
    t_all = qn.shape[0]
    hb = GDN_HB
    c = GDN_CHUNK
    ncc = ctx_len // c
    ncl = seq // c
    ctx0 = nb * ncl
    n_hg = N_HEADS // hb

    def row_f(b, s):
        return jnp.where(s < ncc, ctx0 + b * ncc + s, b * ncl + (s - ncc))

    def row_b(b, s):
        return jnp.where(s < ncc, ctx0 + b * ncc + (ncc - 1 - s), b * ncl + (ncl - 1 - (s - ncc)))

    def specs(rowfn):
        big = pl.BlockSpec((c, hb * HEAD_DIM), lambda b, hg, s: (rowfn(b, s), hg))
        small = pl.BlockSpec((c, HEAD_DIM), lambda b, hg, s: (rowfn(b, s), hg))
        return [big, big, big, small, small]

    out_f = pl.BlockSpec((c, hb * HEAD_DIM), lambda b, hg, s: (row_f(b, s), hg))
    out_b = pl.BlockSpec((c, hb * HEAD_DIM), lambda b, hg, s: (row_b(b, s), hg))
    return pl.pallas_call(
        functools.partial(_gdn_scan_kernel, hb=hb),
        grid=(nb, n_hg, ncc + ncl),
        in_specs=specs(row_f) + specs(row_b),
        out_specs=[out_f, out_b],
        out_shape=[jax.ShapeDtypeStruct((t_all, GROUP_W), F32)] * 2,
        scratch_shapes=[pltpu.VMEM((2 * hb, HEAD_DIM, HEAD_DIM), F32)],
        compiler_params=_cparams(("arbitrary", "arbitrary", "arbitrary")),
        name="gdn_scan",
    )(qn, kn, vn, g, beta, qn, kn, vn, g, beta)


def _outproj_kernel(na_ref, of_ref, ob_ref, z_ref, x_ref, ga_ref, shf_ref, scf_ref, w_ref, ng_ref, lg_ref, lb_ref,
                    x1_ref, h2t_ref, gdn_scr, *, tiles_per_seq, nb):
    i = pl.program_id(0)
    r = jnp.minimum(i // tiles_per_seq, nb)
    o = of_ref[...] + ob_ref[...]
    ng = ng_ref[...]
    for h in range(N_HEADS):
        sl = slice(h * HEAD_DIM, (h + 1) * HEAD_DIM)
        oh = o[:, sl]
        oh = oh * lax.rsqrt(jnp.mean(oh * oh, axis=-1, keepdims=True) + 1e-6) * ng
        gdn_scr[:, sl] = (oh * _silu(z_ref[:, sl])).astype(BF16)
    mix = _dot(na_ref[...].astype(BF16), w_ref[0:GROUP_W, :]) + _dot(gdn_scr[...], w_ref[GROUP_W:2 * GROUP_W, :])
    y = DEEPNORM_ALPHA * x_ref[...] + ga_ref[pl.ds(r, 1), :] * mix
    x1 = _layer_norm_rows(y, lg_ref[...], lb_ref[...])
    x1_ref[...] = x1
    h2 = x1 * (1.0 + scf_ref[pl.ds(r, 1), :]) + shf_ref[pl.ds(r, 1), :]
    h2t_ref[...] = jnp.transpose(h2)


def _outproj_call(na_all, o_f, o_b, p_all, xa, mod_l, w_out, norm_g, ln_g, ln_b, *, seq, nb):
    t_all, d = xa.shape
    tm = TM_OUT
    kern = functools.partial(_outproj_kernel, tiles_per_seq=seq // tm, nb=nb)
    row = lambda i: (i, 0)
    const = lambda i: (0, 0)
    return pl.pallas_call(
        kern,
        grid=(t_all // tm,),
        in_specs=[
            pl.BlockSpec((tm, GROUP_W), row),
            pl.BlockSpec((tm, GROUP_W), row),
            pl.BlockSpec((tm, GROUP_W), row),
            pl.BlockSpec((tm, GROUP_W), lambda i: (i, 6)),
            pl.BlockSpec((tm, d), row),
            pl.BlockSpec((8, d), lambda i: (0, 2)),
            pl.BlockSpec((8, d), lambda i: (0, 3)),
            pl.BlockSpec((8, d), lambda i: (0, 4)),
            pl.BlockSpec((d, d), const),
            pl.BlockSpec((1, HEAD_DIM), const),
            pl.BlockSpec((1, d), const),
            pl.BlockSpec((1, d), const),
        ],
        out_specs=[
            pl.BlockSpec((tm, d), row),
            pl.BlockSpec((d, tm), lambda i: (0, i)),
        ],
        out_shape=[
            jax.ShapeDtypeStruct((t_all, d), F32),
            jax.ShapeDtypeStruct((d, t_all), F32),
        ],
        scratch_shapes=[pltpu.VMEM((tm, GROUP_W), BF16)],
        compiler_params=_cparams(("arbitrary",)),
        name="out_proj_ln",
    )(na_all, o_f, o_b, p_all, xa, mod_l, mod_l, mod_l, w_out, norm_g, ln_g, ln_b)


def _peer_score_kernel(ht_ref, wh_ref, wl_ref, sk_ref, a_ref, b_ref, st_ref, hb_ref, atop, btop, cand):
    head = pl.program_id(1)
    h = ht_ref[...]
    hh, hl = _split_bf16(h)

    @pl.when(head == 0)
    def _():
        hb_ref[...] = hh

    qt = _dot3_split(wh_ref[...], wl_ref[...], hh, hl)
    half = PEER_QDIM // 2
    s0 = jnp.dot(sk_ref[0], qt[0:half], precision=HIGHEST, preferred_element_type=F32)
    s1 = jnp.dot(sk_ref[1], qt[half:2 * half], precision=HIGHEST, preferred_element_type=F32)
    a_ref[0] = s0
    b_ref[0] = s1

    def top_vals(sc, out_ref):
        for k in range(PEER_TOPK):
            m = jnp.max(sc, axis=0, keepdims=True)
            out_ref[k:k + 1, :] = m
            sc = jnp.where(sc == m, -jnp.inf, sc)

    top_vals(s0, atop)
    top_vals(s1, btop)
    bt = btop[...]
    for i in range(PEER_TOPK // 2):
        cand[i * PEER_TOPK:(i + 1) * PEER_TOPK, :] = atop[i:i + 1, :] + bt
    cand[PEER_TOPK * PEER_TOPK // 2:PEER_TOPK * PEER_TOPK // 2 + PEER_TOPK // 2, :] = (
        atop[PEER_TOPK // 2:PEER_TOPK, :] + bt[0:1, :])
    cv = cand[...]
    m0 = jnp.max(cv, axis=0, keepdims=True)
    z = jnp.zeros_like(m0)
    m = m0
    for k in range(PEER_TOPK):
        m = jnp.max(cv, axis=0, keepdims=True)
        z = z + jnp.exp(m - m0)
        cv = jnp.where(cv == m, -jnp.inf, cv)
    st_ref[0, 0:1, :] = m
    st_ref[0, 1:2, :] = atop[0:1, :]
    st_ref[0, 2:3, :] = bt[0:1, :]
    st_ref[0, 3:4, :] = 1.0 / z
    st_ref[0, 4:8, :] = jnp.zeros((4, m.shape[1]), F32)


def _peer_score_call(h2t, wq_hi, wq_lo, subkeys):
    d, t_all = h2t.shape
    tm = TM_PEER
    ncand = PEER_TOPK * PEER_TOPK // 2 + PEER_TOPK // 2
    tok = lambda i, h: (h, 0, i)
    return pl.pallas_call(
        _peer_score_kernel,
        grid=(t_all // tm, PEER_HEADS),
        in_specs=[
            pl.BlockSpec((d, tm), lambda i, h: (0, i)),
            pl.BlockSpec((PEER_QDIM, d), lambda i, h: (h, 0)),
            pl.BlockSpec((PEER_QDIM, d), lambda i, h: (h, 0)),
            pl.BlockSpec((2, PEER_NKEYS, PEER_QDIM // 2), lambda i, h: (0, 0, 0)),
        ],
        out_specs=[
            pl.BlockSpec((1, PEER_NKEYS, tm), tok),
            pl.BlockSpec((1, PEER_NKEYS, tm), tok),
            pl.BlockSpec((1, 8, tm), tok),
            pl.BlockSpec((d, tm), lambda i, h: (0, i)),
        ],
        out_shape=[
            jax.ShapeDtypeStruct((PEER_HEADS, PEER_NKEYS, t_all), F32),
            jax.ShapeDtypeStruct((PEER_HEADS, PEER_NKEYS, t_all), F32),
            jax.ShapeDtypeStruct((PEER_HEADS, 8, t_all), F32),
            jax.ShapeDtypeStruct((d, t_all), BF16),
        ],
        scratch_shapes=[
            pltpu.VMEM((PEER_TOPK, tm), F32),
            pltpu.VMEM((PEER_TOPK, tm), F32),
            pltpu.VMEM((ncand, tm), F32),
        ],
        compiler_params=_cparams(("arbitrary", "arbitrary")),
        name="peer_scores",
    )(h2t, wq_hi, wq_lo, subkeys)


def _peer_dense_kernel(h_ref, u_ref, vt_ref, a_ref, b_ref, st_ref, o_ref, eb_ref, w_scr):
    e = pl.program_id(1)
    tm = TM_PEER
    nk = PEER_NKEYS
    rows_per_step = TE_PEER // nk

    @pl.when(e == 0)
    def _():
        o_ref[...] = jnp.zeros(o_ref.shape, F32)
        for h in range(PEER_HEADS):
            eb_ref[h] = jnp.exp(b_ref[h] - st_ref[h, 2:3, :])

    act = _dot(u_ref[...], h_ref[...])
    sqrt_half = np.float32(np.sqrt(0.5))
    assert rows_per_step == 8
    i0 = pl.multiple_of(e * rows_per_step, rows_per_step)
    for tc in range(tm // 128):
        ls = slice(tc * 128, (tc + 1) * 128)
        arows, earows = [], []
        for h in range(PEER_HEADS):
            ar = a_ref[h, pl.ds(i0, rows_per_step), ls]
            arows.append(ar)
            earows.append(jnp.exp(ar - st_ref[h, 1:2, ls]) * st_ref[h, 3:4, ls])
        for ii in range(rows_per_step):
            g = jnp.zeros((nk, 128), F32)
            for h in range(PEER_HEADS):
                ssum = arows[h][ii:ii + 1, :] + b_ref[h, :, ls]
                g = g + jnp.where(ssum >= st_ref[h, 0:1, ls], eb_ref[h, :, ls] * earows[h][ii:ii + 1, :], 0.0)
            xa = act[ii * nk:(ii + 1) * nk, ls]
            ge = 0.5 * xa * (1.0 + lax.erf(xa * sqrt_half))
            w_scr[ii * nk:(ii + 1) * nk, ls] = (ge * g).astype(BF16)
    o_ref[...] += _dot(vt_ref[...], w_scr[...])


def _peer_dense_call(hb, u_bf, vt_bf, a_t, b_t, stats):
    d, t_all = hb.shape
    ne = u_bf.shape[0]
    tm, te = TM_PEER, TE_PEER
    tok3 = lambda i, e: (0, 0, i)
    return pl.pallas_call(
        _peer_dense_kernel,
        grid=(t_all // tm, ne // te),
        in_specs=[
            pl.BlockSpec((d, tm), lambda i, e: (0, i)),
            pl.BlockSpec((te, d), lambda i, e: (e, 0)),
            pl.BlockSpec((d, te), lambda i, e: (0, e)),
            pl.BlockSpec((PEER_HEADS, PEER_NKEYS, tm), tok3),
            pl.BlockSpec((PEER_HEADS, PEER_NKEYS, tm), tok3),
            pl.BlockSpec((PEER_HEADS, 8, tm), tok3),
        ],
        out_specs=pl.BlockSpec((d, tm), lambda i, e: (0, i)),
        out_shape=jax.ShapeDtypeStruct((d, t_all), F32),
        scratch_shapes=[
            pltpu.VMEM((PEER_HEADS, PEER_NKEYS, tm), F32),
            pltpu.VMEM((te, tm), BF16),
        ],
        compiler_params=_cparams(("arbitrary", "arbitrary")),
        name="peer_dense",
    )(hb, u_bf, vt_bf, a_t, b_t, stats)


def _peer_out_kernel(ft_ref, x1_ref, gf_ref, lg_ref, lb_ref, o_ref, *, tiles_per_seq, nb):
    i = pl.program_id(0)
    r = jnp.minimum(i // tiles_per_seq, nb)
    y = DEEPNORM_ALPHA * x1_ref[...] + gf_ref[pl.ds(r, 1), :] * jnp.transpose(ft_ref[...])
    o_ref[...] = _layer_norm_rows(y, lg_ref[...], lb_ref[...])


def _peer_out_call(ffn_t, x1, mod_l, ln_g, ln_b, *, seq, nb):
    t_all, d = x1.shape
    tm = TM_OUT
    kern = functools.partial(_peer_out_kernel, tiles_per_seq=seq // tm, nb=nb)
    return pl.pallas_call(
        kern,
        grid=(t_all // tm,),
        in_specs=[
            pl.BlockSpec((d, tm), lambda i: (0, i)),
            pl.BlockSpec((tm, d), lambda i: (i, 0)),
            pl.BlockSpec((8, d), lambda i: (0, 5)),
            pl.BlockSpec((1, d), lambda i: (0, 0)),
            pl.BlockSpec((1, d), lambda i: (0, 0)),
        ],
        out_specs=pl.BlockSpec((tm, d), lambda i: (i, 0)),
        out_shape=jax.ShapeDtypeStruct((t_all, d), F32),
        compiler_params=_cparams(("arbitrary",)),
        name="peer_out_ln",
    )(ffn_t, x1, mod_l, ln_g, ln_b)


def _ab_columns(w_tail):
    depth, d, _ = w_tail.shape
    hb = GDN_HB
    n_hg = N_HEADS // hb
    wa = w_tail[:, :, :2 * N_HEADS].reshape(depth, d, 2, n_hg, hb)
    wb = w_tail[:, :, 2 * N_HEADS:].reshape(depth, d, 2, n_hg, hb)

    def lay(w):
        w = w.transpose(0, 1, 3, 2, 4).reshape(depth, d, n_hg, 2 * hb)
        w = jnp.pad(w, ((0, 0), (0, 0), (0, 0), (0, HEAD_DIM - 2 * hb)))
        return w.reshape(depth, d, n_hg * HEAD_DIM)

    return jnp.concatenate([lay(wa), lay(wb)], axis=-1)


def _head_param_lanes(p):
    depth = p.shape[0]
    hb = GDN_HB
    n_hg = N_HEADS // hb
    p = p.reshape(depth, 2, n_hg, hb).transpose(0, 2, 1, 3).reshape(depth, n_hg, 2 * hb)
    p = jnp.pad(p, ((0, 0), (0, 0), (0, HEAD_DIM - 2 * hb)))
    return p.reshape(depth, 1, n_hg * HEAD_DIM)


def kernel(x, c, ctx, c_ctx, w_mod, b_mod, w_in, w_out, na_rpb, gdn_conv, gdn_a_log, gdn_dt_bias, gdn_norm_g,
           peer_wq, peer_subkeys, peer_u, peer_v, ln_g, ln_b):
    nb, seq, d = x.shape
    ctx_len = ctx.shape[1]
    depth = w_mod.shape[0]
    assert d == D_MODEL and nb + 1 <= 8
    assert seq % (NA_QROWS * GRID_W) == 0 and seq % TM_PROJ == 0
    t_lat = nb * seq
    dims = dict(nb=nb, seq=seq, ctx_len=ctx_len)

    xa = jnp.concatenate([x.reshape(t_lat, d), ctx.reshape(nb * ctx_len, d)], axis=0)
    cs = jnp.concatenate([c, c_ctx[None, :], jnp.zeros((8 - nb - 1, d), F32)], axis=0)
    mod = _mod_call(cs, w_mod, b_mod)

    w_main = w_in[:, :, :P_MAIN_W].astype(BF16)
    w_ab = _ab_columns(w_in[:, :, P_MAIN_W:]).astype(BF16)
    w_out_bf = w_out.astype(BF16)
    alog = _head_param_lanes(gdn_a_log)
    dtb = _head_param_lanes(gdn_dt_bias)
    conv_w = jnp.pad(gdn_conv, ((0, 0), (0, 8 - GDN_CONV), (0, 0)))
    rope_c, rope_s = _rope_tables(seq, ctx_len)
    wq_t = peer_wq.transpose(0, 2, 1)
    wq_hi = wq_t.astype(BF16)
    wq_lo = (wq_t - wq_hi.astype(F32)).astype(BF16)
    u_bf = peer_u.astype(BF16)
    vt_bf = peer_v.astype(BF16).transpose(0, 2, 1)

    for l in range(depth):
        p_all, pab = _inproj_call(xa, mod[l], w_main[l], w_ab[l], seq=seq, nb=nb)
        bias = _na_bias(na_rpb[l], seq // GRID_W)
        na_all = _na_call(p_all, bias, **dims)
        na_all = _ctx_attn_call(p_all, na_all, **dims)
        qn, kn, vn, g, beta = _gdn_prep_call(p_all, pab, conv_w[l], alog[l], dtb[l], rope_c, rope_s, **dims)
        o_f, o_b = _gdn_scan_call(qn, kn, vn, g, beta, **dims)
        x1, h2t = _outproj_call(na_all, o_f, o_b, p_all, xa, mod[l], w_out_bf[l], gdn_norm_g[l][None, :],
                                ln_g[l, 0][None, :], ln_b[l, 0][None, :], seq=seq, nb=nb)
        a_t, b_t, stats, hb = _peer_score_call(h2t, wq_hi[l], wq_lo[l], peer_subkeys[l])
        ffn_t = _peer_dense_call(hb, u_bf[l], vt_bf[l], a_t, b_t, stats)
        xa = _peer_out_call(ffn_t, x1, mod[l], ln_g[l, 1][None, :], ln_b[l, 1][None, :], seq=seq, nb=nb)
    return xa[:t_lat].reshape(nb, seq, d)
```

```python
import functools

import numpy as np
import jax
import jax.numpy as jnp
from jax import lax
from jax.experimental import pallas as pl
from jax.experimental.pallas import tpu as pltpu

F32 = jnp.float32
BF16 = jnp.bfloat16
HIGHEST = lax.Precision.HIGHEST

D_MODEL = 2048
HEAD_DIM = 128
N_HEADS = 8
GROUP_W = N_HEADS * HEAD_DIM
GRID_W = 64
NA_KR = 8
NA_KC = 16
NA_QROWS = 8
NA_KROWS = 16
GDN_CHUNK = 64
GDN_CONV = 5
GDN_HB = 4
ROPE_THETA = 10000.0
PEER_HEADS = 8
PEER_TOPK = 16
PEER_NKEYS = 128
PEER_QDIM = 256
N_MOD = 6
LN_EPS = 1e-6
NEG_INF = -1e30
DEPTH_FOR_DEEPNORM = 4
DEEPNORM_ALPHA = (2 * DEPTH_FOR_DEEPNORM) ** 0.25
P_MAIN_W = 7 * GROUP_W
VMEM_LIMIT = 56 * 1024 * 1024

TM_PROJ = 512
TN_PROJ = 512
TM_OUT = 256
TM_PREP = 256
TM_PEER = 512
TE_PEER = 1024


def _cparams(sem):
    return pltpu.CompilerParams(dimension_semantics=sem, vmem_limit_bytes=VMEM_LIMIT)


def _sigmoid(x):
    return 1.0 / (1.0 + jnp.exp(-x))


def _silu(x):
    return x * _sigmoid(x)


def _dot(a, b):
    return jnp.dot(a, b, preferred_element_type=F32)


def _dot_t(a, b):
    return lax.dot_general(a, b, (((1,), (1,)), ((), ())), preferred_element_type=F32)


def _split_bf16(a):
    hi = a.astype(BF16)
    lo = (a - hi.astype(F32)).astype(BF16)
    return hi, lo


def _dot3_split(ah, al, bh, bl):
    return _dot(ah, bh) + (_dot(ah, bl) + _dot(al, bh))


def _layer_norm_rows(y, g, b):
    mu = jnp.mean(y, axis=-1, keepdims=True)
    yc = y - mu
    var = jnp.mean(yc * yc, axis=-1, keepdims=True)
    return yc * lax.rsqrt(var + LN_EPS) * g + b


def _mod_kernel(c_ref, w_ref, b_ref, o_ref):
    s = _silu(c_ref[...])
    o_ref[0] = jnp.dot(s, w_ref[0], precision=HIGHEST, preferred_element_type=F32) + b_ref[0]


def _mod_call(cs, w_mod, b_mod):
    depth, d, n = w_mod.shape
    tn = 1024
    return pl.pallas_call(
        _mod_kernel,
        grid=(depth, n // tn),
        in_specs=[
            pl.BlockSpec((8, d), lambda l, j: (0, 0)),
            pl.BlockSpec((1, d, tn), lambda l, j: (l, 0, j)),
            pl.BlockSpec((1, 1, tn), lambda l, j: (l, 0, j)),
        ],
        out_specs=pl.BlockSpec((1, 8, tn), lambda l, j: (l, 0, j)),
        out_shape=jax.ShapeDtypeStruct((depth, 8, n), F32),
        compiler_params=_cparams(("arbitrary", "arbitrary")),
        name="adaln_mod",
    )(cs, w_mod, b_mod.reshape(depth, 1, n))


def _inproj_kernel(x_ref, sh_ref, sc_ref, w_ref, wab_ref, p_ref, pab_ref, h_scr, *, tiles_per_seq, nb):
    i = pl.program_id(0)
    j = pl.program_id(1)

    @pl.when(j == 0)
    def _():
        r = jnp.minimum(i // tiles_per_seq, nb)
        sh = sh_ref[pl.ds(r, 1), :]
        sc = sc_ref[pl.ds(r, 1), :]
        hb = (x_ref[...] * (1.0 + sc) + sh).astype(BF16)
        h_scr[...] = hb
        pab_ref[...] = _dot(hb, wab_ref[...])

    p_ref[...] = _dot(h_scr[...], w_ref[...])


def _inproj_call(xa, mod_l, w_main, w_ab, *, seq, nb):
    t_all, d = xa.shape
    n = w_main.shape[1]
    nab = w_ab.shape[1]
    tm, tn = TM_PROJ, TN_PROJ
    kern = functools.partial(_inproj_kernel, tiles_per_seq=seq // tm, nb=nb)
    return pl.pallas_call(
        kern,
        grid=(t_all // tm, n // tn),
        in_specs=[
            pl.BlockSpec((tm, d), lambda i, j: (i, 0)),
            pl.BlockSpec((8, d), lambda i, j: (0, 0)),
            pl.BlockSpec((8, d), lambda i, j: (0, 1)),
            pl.BlockSpec((d, tn), lambda i, j: (0, j)),
            pl.BlockSpec((d, nab), lambda i, j: (0, 0)),
        ],
        out_specs=[
            pl.BlockSpec((tm, tn), lambda i, j: (i, j)),
            pl.BlockSpec((tm, nab), lambda i, j: (i, 0)),
        ],
        out_shape=[
            jax.ShapeDtypeStruct((t_all, n), F32),
            jax.ShapeDtypeStruct((t_all, nab), F32),
        ],
        scratch_shapes=[pltpu.VMEM((tm, d), BF16)],
        compiler_params=_cparams(("arbitrary", "arbitrary")),
        name="in_proj",
    )(xa, mod_l, mod_l, w_main, w_ab)


def _na_bias_index_tables(nrows):
    rq, rk = NA_QROWS, NA_KROWS
    big = 1 << 20
    cfg = [(0, 0, nrows), (8, 4, big), (nrows - rq, nrows - rk, nrows)]
    dr = np.zeros((3, rq, rk), np.int32)
    rv = np.zeros((3, rq, rk), bool)
    for v, (r0, start, nr) in enumerate(cfg):
        r = r0 + np.arange(rq)[:, None]
        kr = start + np.arange(rk)[None, :]
        rs = np.clip(r - NA_KR // 2, 0, nr - NA_KR)
        rv[v] = (kr >= rs) & (kr < rs + NA_KR)
        dr[v] = np.clip(kr - r + NA_KR - 1, 0, 2 * NA_KR - 2)
    qc = np.arange(GRID_W)[:, None]
    kc = np.arange(GRID_W)[None, :]
    ws = np.clip(qc - NA_KC // 2, 0, GRID_W - NA_KC)
    cv = (kc >= ws) & (kc < ws + NA_KC)
    dc = np.clip(kc - qc + NA_KC - 1, 0, 2 * NA_KC - 2).astype(np.int32)
    return dr, rv, dc, cv


def _na_bias(rpb_l, nrows):
    dr, rv, dc, cv = _na_bias_index_tables(nrows)
    tc = jnp.where(cv[None, None], rpb_l[:, :, dc], NEG_INF)
    blk = tc[:, dr]
    blk = jnp.where(rv[None, :, :, :, None, None], blk, NEG_INF)
    blk = blk.transpose(0, 1, 2, 4, 3, 5)
    return blk.reshape(N_HEADS, 3, NA_QROWS * GRID_W, NA_KROWS * GRID_W)


def _na_kernel(q_ref, k0, k1, k2, k3, v0, v1, v2, v3, kc_ref, vc_ref, bias_ref, o_ref):
    q = (q_ref[...] * (HEAD_DIM ** -0.5)).astype(BF16)
    kb = 4 * GRID_W
    s = []
    for j, kr in enumerate((k0, k1, k2, k3)):
        s.append(_dot_t(q, kr[...].astype(BF16)) + bias_ref[:, j * kb:(j + 1) * kb])
    s.append(_dot_t(q, kc_ref[...].astype(BF16)))
    m = s[0].max(axis=-1, keepdims=True)
    for t in s[1:]:
        m = jnp.maximum(m, t.max(axis=-1, keepdims=True))
    p = [jnp.exp(t - m) for t in s]
    l = p[0].sum(axis=-1, keepdims=True)
    for t in p[1:]:
        l = l + t.sum(axis=-1, keepdims=True)
    vs = (v0, v1, v2, v3, vc_ref)
    o = _dot(p[0].astype(BF16), vs[0][...].astype(BF16))
    for t, vr in zip(p[1:], vs[1:]):
        o = o + _dot(t.astype(BF16), vr[...].astype(BF16))
    o_ref[...] = o / l


def _na_call(p_all, bias, *, nb, seq, ctx_len):
    t_all = p_all.shape[0]
    nrows = seq // GRID_W
    nrb = nrows // NA_QROWS
    tq = NA_QROWS * GRID_W
    tk = 4 * GRID_W
    assert ctx_len == tk and nrows >= NA_KROWS
    kblocks_per_seq = seq // tk
    max_sb = kblocks_per_seq - 4
    ctx_blk0 = nb * seq // tk

    def sb(r):
        return jnp.clip(2 * r - 1, 0, max_sb)

    def kspec(j, colbase):
        return pl.BlockSpec((tk, HEAD_DIM), lambda h, b, r: (b * kblocks_per_seq + sb(r) + j, colbase + h))

    def variant(r):
        return jnp.where(r == 0, 0, jnp.where(r == nrb - 1, 2, 1))

    in_specs = [pl.BlockSpec((tq, HEAD_DIM), lambda h, b, r: (b * nrb + r, h))]
    in_specs += [kspec(j, N_HEADS) for j in range(4)]
    in_specs += [kspec(j, 2 * N_HEADS) for j in range(4)]
    in_specs += [
        pl.BlockSpec((tk, HEAD_DIM), lambda h, b, r: (ctx_blk0 + b, N_HEADS + h)),
        pl.BlockSpec((tk, HEAD_DIM), lambda h, b, r: (ctx_blk0 + b, 2 * N_HEADS + h)),
        pl.BlockSpec((None, None, tq, NA_KROWS * GRID_W), lambda h, b, r: (h, variant(r), 0, 0)),
    ]
    return pl.pallas_call(
        _na_kernel,
        grid=(N_HEADS, nb, nrb),
        in_specs=in_specs,
        out_specs=pl.BlockSpec((tq, HEAD_DIM), lambda h, b, r: (b * nrb + r, h)),
        out_shape=jax.ShapeDtypeStruct((t_all, GROUP_W), F32),
        compiler_params=_cparams(("arbitrary", "arbitrary", "arbitrary")),
        name="na_attention",
    )(*([p_all] * 11), bias)


def _ctx_attn_kernel(q_ref, k_ref, v_ref, na_in_ref, o_ref):
    del na_in_ref
    q = q_ref[...].astype(BF16)
    s = _dot_t(q, k_ref[...].astype(BF16)) * (HEAD_DIM ** -0.5)
    m = s.max(axis=-1, keepdims=True)
    p = jnp.exp(s - m)
    l = p.sum(axis=-1, keepdims=True)
    o_ref[...] = _dot(p.astype(BF16), v_ref[...].astype(BF16)) / l


def _ctx_attn_call(p_all, na_all, *, nb, seq, ctx_len):
    blk0 = nb * seq // ctx_len
    return pl.pallas_call(
        _ctx_attn_kernel,
        grid=(nb, N_HEADS),
        in_specs=[
            pl.BlockSpec((ctx_len, HEAD_DIM), lambda b, h: (blk0 + b, h)),
            pl.BlockSpec((ctx_len, HEAD_DIM), lambda b, h: (blk0 + b, N_HEADS + h)),
            pl.BlockSpec((ctx_len, HEAD_DIM), lambda b, h: (blk0 + b, 2 * N_HEADS + h)),
            pl.BlockSpec(memory_space=pl.ANY),
        ],
        out_specs=pl.BlockSpec((ctx_len, HEAD_DIM), lambda b, h: (blk0 + b, h)),
        out_shape=jax.ShapeDtypeStruct(na_all.shape, F32),
        input_output_aliases={3: 0},
        compiler_params=_cparams(("arbitrary", "arbitrary")),
        name="ctx_attention",
    )(p_all, p_all, p_all, na_all)


def _rope_tables(seq, ctx_len):
    t = np.arange(seq)
    row = (t // GRID_W).astype(np.float32)
    col = (t % GRID_W).astype(np.float32)
    n_freq = HEAD_DIM // 4
    inv = (ROPE_THETA ** (-np.arange(n_freq, dtype=np.float32) / n_freq)).astype(np.float32)
    ang = jnp.stack([jnp.asarray(row)[:, None] * inv, jnp.asarray(col)[:, None] * inv], axis=1)
    cos, sin = jnp.cos(ang), jnp.sin(ang)
    c = jnp.concatenate([cos, cos], axis=-1).reshape(seq, HEAD_DIM)
    s = jnp.concatenate([-sin, sin], axis=-1).reshape(seq, HEAD_DIM)
    c = jnp.concatenate([c, jnp.ones((ctx_len, HEAD_DIM), F32)], axis=0)
    s = jnp.concatenate([s, jnp.zeros((ctx_len, HEAD_DIM), F32)], axis=0)
    return c, s


def _gdn_prep_kernel(cur_ref, prev_ref, next_ref, pab_ref, cw_ref, alog_ref, dtb_ref, rc_ref, rs_ref,
                     q_ref, k_ref, v_ref, g_ref, b_ref, ext_ref, *, n_lat_tiles, tps, nab_half):
    i = pl.program_id(0)
    tm = TM_PREP
    is_ctx = i >= n_lat_tiles
    first = jnp.logical_or(is_ctx, i % tps == 0)
    last = jnp.logical_or(is_ctx, i % tps == tps - 1)
    ext_ref[8:8 + tm, :] = cur_ref[...]
    ext_ref[0:8, :] = jnp.where(first, 0.0, prev_ref[...])
    ext_ref[8 + tm:16 + tm, :] = jnp.where(last, 0.0, next_ref[...])

    lane = lax.broadcasted_iota(jnp.int32, (tm, HEAD_DIM), 1)
    half0 = (lane % (HEAD_DIM // 2)) < (HEAD_DIM // 4)
    rc = rc_ref[...]
    rs = rs_ref[...]
    base = 8 - GDN_CONV // 2
    outs = (q_ref, k_ref, v_ref)
    for part in range(3):
        for h in range(N_HEADS):
            c0 = part * GROUP_W + h * HEAD_DIM
            acc = cw_ref[0:1, c0:c0 + HEAD_DIM] * ext_ref[base:base + tm, c0:c0 + HEAD_DIM]
            for t in range(1, GDN_CONV):
                acc = acc + cw_ref[t:t + 1, c0:c0 + HEAD_DIM] * ext_ref[base + t:base + t + tm, c0:c0 + HEAD_DIM]
            y = _silu(acc)
            if part < 2:
                y = y * lax.rsqrt(jnp.sum(y * y, axis=-1, keepdims=True) + 1e-6)
                partner = jnp.where(half0, pltpu.roll(y, HEAD_DIM - HEAD_DIM // 4, 1),
                                    pltpu.roll(y, HEAD_DIM // 4, 1))
                y = y * rc + partner * rs
                if part == 0:
                    y = y * (HEAD_DIM ** -0.5)
            outs[part][:, h * HEAD_DIM:(h + 1) * HEAD_DIM] = y

    a = pab_ref[:, 0:nab_half] + dtb_ref[...]
    softplus = jnp.maximum(a, 0.0) + jnp.log1p(jnp.exp(-jnp.abs(a)))
    g_ref[...] = -jnp.exp(alog_ref[...]) * softplus
    b_ref[...] = _sigmoid(pab_ref[:, nab_half:2 * nab_half])


def _gdn_prep_call(p_all, pab, conv_w, alog, dtb, rope_c, rope_s, *, nb, seq, ctx_len):
    t_all = p_all.shape[0]
    tm = TM_PREP
    assert ctx_len == tm
    n_tiles = t_all // tm
    n_lat_tiles = nb * seq // tm
    tps = seq // tm
    nab_half = pab.shape[1] // 2
    qkv_w = 3 * GROUP_W
    last8 = t_all // 8 - 1
    kern = functools.partial(_gdn_prep_kernel, n_lat_tiles=n_lat_tiles, tps=tps, nab_half=nab_half)

    def rope_idx(i):
        return jnp.where(i < n_lat_tiles, i % tps, tps)

    return pl.pallas_call(
        kern,
        grid=(n_tiles,),
        in_specs=[
            pl.BlockSpec((tm, qkv_w), lambda i: (i, 1)),
            pl.BlockSpec((8, qkv_w), lambda i: (jnp.maximum(i * (tm // 8) - 1, 0), 1)),
            pl.BlockSpec((8, qkv_w), lambda i: (jnp.minimum((i + 1) * (tm // 8), last8), 1)),
            pl.BlockSpec((tm, 2 * nab_half), lambda i: (i, 0)),
            pl.BlockSpec((8, qkv_w), lambda i: (0, 0)),
            pl.BlockSpec((1, nab_half), lambda i: (0, 0)),
            pl.BlockSpec((1, nab_half), lambda i: (0, 0)),
            pl.BlockSpec((tm, HEAD_DIM), lambda i: (rope_idx(i), 0)),
            pl.BlockSpec((tm, HEAD_DIM), lambda i: (rope_idx(i), 0)),
        ],
        out_specs=[
            pl.BlockSpec((tm, GROUP_W), lambda i: (i, 0)),
            pl.BlockSpec((tm, GROUP_W), lambda i: (i, 0)),
            pl.BlockSpec((tm, GROUP_W), lambda i: (i, 0)),
            pl.BlockSpec((tm, nab_half), lambda i: (i, 0)),
            pl.BlockSpec((tm, nab_half), lambda i: (i, 0)),
        ],
        out_shape=[
            jax.ShapeDtypeStruct((t_all, GROUP_W), F32),
            jax.ShapeDtypeStruct((t_all, GROUP_W), F32),
            jax.ShapeDtypeStruct((t_all, GROUP_W), F32),
            jax.ShapeDtypeStruct((t_all, nab_half), F32),
            jax.ShapeDtypeStruct((t_all, nab_half), F32),
        ],
        scratch_shapes=[pltpu.VMEM((tm + 16, qkv_w), F32)],
        compiler_params=_cparams(("arbitrary",)),
        name="gdn_prep",
    )(p_all, p_all, p_all, pab, conv_w, alog, dtb, rope_c, rope_s)


def _unit_tri_inverse(ns, orders):
    c = GDN_CHUNK
    ds = []
    for n, (row, col) in zip(ns, orders):
        m1 = jnp.logical_and(jnp.logical_and((row >> 1) == (col >> 1), (row & 1) == 1), (col & 1) == 0)
        ds.append((row == col).astype(F32) - jnp.where(m1, n, 0.0))
    k = 2
    while k < c:
        sh = int(np.log2(2 * k))
        dsp, xs = [], []
        for n, d, (row, col) in zip(ns, ds, orders):
            mk = jnp.logical_and((row >> sh) == (col >> sh),
                                 jnp.logical_and((row & (2 * k - 1)) >= k, (col & (2 * k - 1)) < k))
            lh, ll = _split_bf16(jnp.where(mk, n, 0.0))
            dh, dl = _split_bf16(d)
            dsp.append((dh, dl))
            xs.append(_dot3_split(lh, ll, dh, dl))
        ds = [d - _dot3_split(dh, dl, *_split_bf16(x)) for d, (dh, dl), x in zip(ds, dsp, xs)]
        k *= 2
    return ds


def _gdn_scan_kernel(qf, kf, vf, gf, bf, qb, kb, vb, gb, bb, of_ref, ob_ref, st_ref, *, hb):
    s = pl.program_id(2)
    c = GDN_CHUNK

    @pl.when(s == 0)
    def _():
        st_ref[...] = jnp.zeros(st_ref.shape, F32)

    ri = lax.broadcasted_iota(jnp.int32, (c, c), 0)
    ci = lax.broadcasted_iota(jnp.int32, (c, c), 1)
    zpad = jnp.zeros((HEAD_DIM - c, HEAD_DIM), F32)
    ch = []
    for d, (q_ref, k_ref, v_ref, g_ref, be_ref, o_ref) in enumerate(
            ((qf, kf, vf, gf, bf, of_ref), (qb, kb, vb, gb, bb, ob_ref))):
        incl = (ri >= ci) if d == 0 else (ri <= ci)
        gc_all = jnp.dot(incl.astype(F32), g_ref[...], precision=HIGHEST, preferred_element_type=F32)
        gc_t = jnp.transpose(jnp.concatenate([gc_all, zpad], axis=0))
        beta_all = be_ref[...]
        for hh in range(hb):
            ln = d * hb + hh
            sl = slice(hh * HEAD_DIM, (hh + 1) * HEAD_DIM)
            ch.append(dict(
                ln=ln, sl=sl, o_ref=o_ref, incl=incl,
                strict=(ri > ci) if d == 0 else (ri < ci),
                order=(ri, ci) if d == 0 else (ci, ri),
                last=c - 1 if d == 0 else 0,
                gcc=gc_all[:, ln:ln + 1], grow=gc_t[ln:ln + 1, 0:c], beta=beta_all[:, ln:ln + 1],
                q=q_ref[:, sl], k=k_ref[:, sl], v=v_ref[:, sl]))
    for t in ch:
        t["decay"] = jnp.where(t["incl"], jnp.exp(jnp.where(t["incl"], t["gcc"] - t["grow"], 0.0)), 0.0)
        t["kbeta"] = t["k"] * t["beta"]
    for t in ch:
        t["kk"] = _dot_t(t["kbeta"], t["k"])
    for t in ch:
        t["qk"] = _dot_t(t["q"], t["k"]) * t["decay"]
    ns = [jnp.where(t["strict"], t["kk"] * t["decay"], 0.0) for t in ch]
    tinvs = _unit_tri_inverse(ns, [t["order"] for t in ch])
    for t, tinv in zip(ch, tinvs):
        eg = jnp.exp(t["gcc"])
        t["eg"] = eg
        rhs = jnp.concatenate([t["v"] * t["beta"], t["kbeta"] * eg], axis=1)
        t["sol"] = _dot3_split(*_split_bf16(tinv), *_split_bf16(rhs))
    for t in ch:
        t["state"] = st_ref[t["ln"]]
        t["v_new"] = t["sol"][:, :HEAD_DIM] - _dot(t["sol"][:, HEAD_DIM:], t["state"])
    for t in ch:
        t["o_ref"][:, t["sl"]] = _dot(t["q"] * t["eg"], t["state"]) + _dot(t["qk"], t["v_new"])
    for t in ch:
        glast = t["gcc"][t["last"]:t["last"] + 1, :]
        kd_t = jnp.transpose(jnp.concatenate([t["k"] * jnp.exp(glast - t["gcc"]), zpad], axis=0))[:, 0:c]
        st_ref[t["ln"]] = t["state"] * jnp.exp(glast) + _dot(kd_t, t["v_new"])


def _gdn_scan_call(qn, kn, vn, g, beta, *, nb, seq, ctx_len):
    t_all = qn.shape[0]
    hb = GDN_HB
    c = GDN_CHUNK
    ncc = ctx_len // c
    ncl = seq // c
    ctx0 = nb * ncl
    n_hg = N_HEADS // hb

    def row_f(b, s):
        return jnp.where(s < ncc, ctx0 + b * ncc + s, b * ncl + (s - ncc))

    def row_b(b, s):
        return jnp.where(s < ncc, ctx0 + b * ncc + (ncc - 1 - s), b * ncl + (ncl - 1 - (s - ncc)))

    def specs(rowfn):
        big = pl.BlockSpec((c, hb * HEAD_DIM), lambda b, hg, s: (rowfn(b, s), hg))
        small = pl.BlockSpec((c, HEAD_DIM), lambda b, hg, s: (rowfn(b, s), hg))
        return [big, big, big, small, small]

    out_f = pl.BlockSpec((c, hb * HEAD_DIM), lambda b, hg, s: (row_f(b, s), hg))
    out_b = pl.BlockSpec((c, hb * HEAD_DIM), lambda b, hg, s: (row_b(b, s), hg))
    return pl.pallas_call(
        functools.partial(_gdn_scan_kernel, hb=hb),
        grid=(nb, n_hg, ncc + ncl),
        in_specs=specs(row_f) + specs(row_b),
        out_specs=[out_f, out_b],
        out_shape=[jax.ShapeDtypeStruct((t_all, GROUP_W), F32)] * 2,
        scratch_shapes=[pltpu.VMEM((2 * hb, HEAD_DIM, HEAD_DIM), F32)],
        compiler_params=_cparams(("arbitrary", "arbitrary", "arbitrary")),
        name="gdn_scan",
    )(qn, kn, vn, g, beta, qn, kn, vn, g, beta)


def _outproj_kernel(na_ref, of_ref, ob_ref, z_ref, x_ref, ga_ref, shf_ref, scf_ref, w_ref, ng_ref, lg_ref, lb_ref,
                    x1_ref, h2t_ref, gdn_scr, *, tiles_per_seq, nb):
    i = pl.program_id(0)
    r = jnp.minimum(i // tiles_per_seq, nb)
    o = of_ref[...] + ob_ref[...]
    ng = ng_ref[...]
    for h in range(N_HEADS):
        sl = slice(h * HEAD_DIM, (h + 1) * HEAD_DIM)
        oh = o[:, sl]
        oh = oh * lax.rsqrt(jnp.mean(oh * oh, axis=-1, keepdims=True) + 1e-6) * ng
        gdn_scr[:, sl] = (oh * _silu(z_ref[:, sl])).astype(BF16)
    mix = _dot(na_ref[...].astype(BF16), w_ref[0:GROUP_W, :]) + _dot(gdn_scr[...], w_ref[GROUP_W:2 * GROUP_W, :])
    y = DEEPNORM_ALPHA * x_ref[...] + ga_ref[pl.ds(r, 1), :] * mix
    x1 = _layer_norm_rows(y, lg_ref[...], lb_ref[...])
    x1_ref[...] = x1
    h2 = x1 * (1.0 + scf_ref[pl.ds(r, 1), :]) + shf_ref[pl.ds(r, 1), :]
    h2t_ref[...] = jnp.transpose(h2)


def _outproj_call(na_all, o_f, o_b, p_all, xa, mod_l, w_out, norm_g, ln_g, ln_b, *, seq, nb):
    t_all, d = xa.shape
    tm = TM_OUT
    kern = functools.partial(_outproj_kernel, tiles_per_seq=seq // tm, nb=nb)
    row = lambda i: (i, 0)
    const = lambda i: (0, 0)
    return pl.pallas_call(
        kern,
        grid=(t_all // tm,),
        in_specs=[
            pl.BlockSpec((tm, GROUP_W), row),
            pl.BlockSpec((tm, GROUP_W), row),
            pl.BlockSpec((tm, GROUP_W), row),
            pl.BlockSpec((tm, GROUP_W), lambda i: (i, 6)),
            pl.BlockSpec((tm, d), row),
            pl.BlockSpec((8, d), lambda i: (0, 2)),
            pl.BlockSpec((8, d), lambda i: (0, 3)),
            pl.BlockSpec((8, d), lambda i: (0, 4)),
            pl.BlockSpec((d, d), const),
            pl.BlockSpec((1, HEAD_DIM), const),
            pl.BlockSpec((1, d), const),
            pl.BlockSpec((1, d), const),
        ],
        out_specs=[
            pl.BlockSpec((tm, d), row),
            pl.BlockSpec((d, tm), lambda i: (0, i)),
        ],
        out_shape=[
            jax.ShapeDtypeStruct((t_all, d), F32),
            jax.ShapeDtypeStruct((d, t_all), F32),
        ],
        scratch_shapes=[pltpu.VMEM((tm, GROUP_W), BF16)],
        compiler_params=_cparams(("arbitrary",)),
        name="out_proj_ln",
    )(na_all, o_f, o_b, p_all, xa, mod_l, mod_l, mod_l, w_out, norm_g, ln_g, ln_b)


def _peer_score_kernel(ht_ref, wh_ref, wl_ref, sk_ref, a_ref, b_ref, st_ref, hb_ref, atop, btop, cand):
    head = pl.program_id(1)
    h = ht_ref[...]
    hh, hl = _split_bf16(h)

    @pl.when(head == 0)
    def _():
        hb_ref[...] = hh

    qt = _dot3_split(wh_ref[...], wl_ref[...], hh, hl)
    half = PEER_QDIM // 2
    s0 = jnp.dot(sk_ref[0], qt[0:half], precision=HIGHEST, preferred_element_type=F32)
    s1 = jnp.dot(sk_ref[1], qt[half:2 * half], precision=HIGHEST, preferred_element_type=F32)
    a_ref[0] = s0
    b_ref[0] = s1

    def top_vals(sc, out_ref):
        for k in range(PEER_TOPK):
            m = jnp.max(sc, axis=0, keepdims=True)
            out_ref[k:k + 1, :] = m
            sc = jnp.where(sc == m, -jnp.inf, sc)

    top_vals(s0, atop)
    top_vals(s1, btop)
    bt = btop[...]
    for i in range(PEER_TOPK // 2):
        cand[i * PEER_TOPK:(i + 1) * PEER_TOPK, :] = atop[i:i + 1, :] + bt
    cand[PEER_TOPK * PEER_TOPK // 2:PEER_TOPK * PEER_TOPK // 2 + PEER_TOPK // 2, :] = (
        atop[PEER_TOPK // 2:PEER_TOPK, :] + bt[0:1, :])
    cv = cand[...]
    m0 = jnp.max(cv, axis=0, keepdims=True)
    z = jnp.zeros_like(m0)
    m = m0
    for k in range(PEER_TOPK):
        m = jnp.max(cv, axis=0, keepdims=True)
        z = z + jnp.exp(m - m0)
        cv = jnp.where(cv == m, -jnp.inf, cv)
    st_ref[0, 0:1, :] = m
    st_ref[0, 1:2, :] = atop[0:1, :]
    st_ref[0, 2:3, :] = bt[0:1, :]
    st_ref[0, 3:4, :] = 1.0 / z
    st_ref[0, 4:8, :] = jnp.zeros((4, m.shape[1]), F32)


def _peer_score_call(h2t, wq_hi, wq_lo, subkeys):
    d, t_all = h2t.shape
    tm = TM_PEER
    ncand = PEER_TOPK * PEER_TOPK // 2 + PEER_TOPK // 2
    tok = lambda i, h: (h, 0, i)
    return pl.pallas_call(
        _peer_score_kernel,
        grid=(t_all // tm, PEER_HEADS),
        in_specs=[
            pl.BlockSpec((d, tm), lambda i, h: (0, i)),
            pl.BlockSpec((PEER_QDIM, d), lambda i, h: (h, 0)),
            pl.BlockSpec((PEER_QDIM, d), lambda i, h: (h, 0)),
            pl.BlockSpec((2, PEER_NKEYS, PEER_QDIM // 2), lambda i, h: (0, 0, 0)),
        ],
        out_specs=[
            pl.BlockSpec((1, PEER_NKEYS, tm), tok),
            pl.BlockSpec((1, PEER_NKEYS, tm), tok),
            pl.BlockSpec((1, 8, tm), tok),
            pl.BlockSpec((d, tm), lambda i, h: (0, i)),
        ],
        out_shape=[
            jax.ShapeDtypeStruct((PEER_HEADS, PEER_NKEYS, t_all), F32),
            jax.ShapeDtypeStruct((PEER_HEADS, PEER_NKEYS, t_all), F32),
            jax.ShapeDtypeStruct((PEER_HEADS, 8, t_all), F32),
            jax.ShapeDtypeStruct((d, t_all), BF16),
        ],
        scratch_shapes=[
            pltpu.VMEM((PEER_TOPK, tm), F32),
            pltpu.VMEM((PEER_TOPK, tm), F32),
            pltpu.VMEM((ncand, tm), F32),
        ],
        compiler_params=_cparams(("arbitrary", "arbitrary")),
        name="peer_scores",
    )(h2t, wq_hi, wq_lo, subkeys)


def _peer_dense_kernel(h_ref, u_ref, vt_ref, a_ref, b_ref, st_ref, o_ref, eb_ref, act_scr, w_scr):
    e = pl.program_id(1)
    tm = TM_PEER
    nk = PEER_NKEYS
    rows_per_step = TE_PEER // nk
    jq_rows = 32
    n_jq = nk // jq_rows

    @pl.when(e == 0)
    def _():
        o_ref[...] = jnp.zeros(o_ref.shape, F32)
        for h in range(PEER_HEADS):
            eb_ref[h] = jnp.exp(b_ref[h] - st_ref[h, 2:3, :])

    act_scr[...] = _dot(u_ref[...], h_ref[...])
    sqrt_half = np.float32(np.sqrt(0.5))
    assert rows_per_step == 8
    i0 = pl.multiple_of(e * rows_per_step, rows_per_step)
    for tc in range(tm // 128):
        ls = slice(tc * 128, (tc + 1) * 128)
        a_bc, ea_bc, tau_bc = [], [], []
        for h in range(PEER_HEADS):
            ar = a_ref[h, pl.ds(i0, rows_per_step), ls]
            er = jnp.exp(ar - st_ref[h, 1:2, ls]) * st_ref[h, 3:4, ls]
            a_bc.append([jnp.broadcast_to(ar[ii:ii + 1, :], (8, 128)) for ii in range(rows_per_step)])
            ea_bc.append([jnp.broadcast_to(er[ii:ii + 1, :], (8, 128)) for ii in range(rows_per_step)])
            tau_bc.append(jnp.broadcast_to(st_ref[h, 0:1, ls], (8, 128)))
        for jq in range(n_jq):
            js = slice(jq * (jq_rows // 8), (jq + 1) * (jq_rows // 8))
            g = [jnp.zeros((jq_rows // 8, 8, 128), F32) for _ in range(rows_per_step)]
            for h in range(PEER_HEADS):
                bb = b_ref[h, js, :, ls]
                ee = eb_ref[h, js, :, ls]
                for ii in range(rows_per_step):
                    g[ii] = g[ii] + jnp.where(bb + a_bc[h][ii] >= tau_bc[h], ee * ea_bc[h][ii], 0.0)
            for ii in range(rows_per_step):
                r0 = ii * nk + jq * jq_rows
                xa = act_scr[r0:r0 + jq_rows, ls]
                ge = 0.5 * xa * (1.0 + lax.erf(xa * sqrt_half))
                w_scr[r0:r0 + jq_rows, ls] = (ge * g[ii].reshape(jq_rows, 128)).astype(BF16)
    o_ref[...] += _dot(vt_ref[...], w_scr[...])


def _peer_dense_call(hb, u_bf, vt_bf, a_t, b_t, stats):
    d, t_all = hb.shape
    ne = u_bf.shape[0]
    tm, te = TM_PEER, TE_PEER
    tok3 = lambda i, e: (0, 0, i)
    b4 = b_t.reshape(PEER_HEADS, PEER_NKEYS // 8, 8, t_all)
    return pl.pallas_call(
        _peer_dense_kernel,
        grid=(t_all // tm, ne // te),
        in_specs=[
            pl.BlockSpec((d, tm), lambda i, e: (0, i)),
            pl.BlockSpec((te, d), lambda i, e: (e, 0)),
            pl.BlockSpec((d, te), lambda i, e: (0, e)),
            pl.BlockSpec((PEER_HEADS, PEER_NKEYS, tm), tok3),
            pl.BlockSpec((PEER_HEADS, PEER_NKEYS // 8, 8, tm), lambda i, e: (0, 0, 0, i)),
            pl.BlockSpec((PEER_HEADS, 8, tm), tok3),
        ],
        out_specs=pl.BlockSpec((d, tm), lambda i, e: (0, i)),
        out_shape=jax.ShapeDtypeStruct((d, t_all), F32),
        scratch_shapes=[
            pltpu.VMEM((PEER_HEADS, PEER_NKEYS // 8, 8, tm), F32),
            pltpu.VMEM((te, tm), F32),
            pltpu.VMEM((te, tm), BF16),
        ],
        compiler_params=_cparams(("arbitrary", "arbitrary")),
        name="peer_dense",
    )(hb, u_bf, vt_bf, a_t, b4, stats)


def _peer_out_kernel(ft_ref, x1_ref, gf_ref, lg_ref, lb_ref, o_ref, *, tiles_per_seq, nb):
    i = pl.program_id(0)
    r = jnp.minimum(i // tiles_per_seq, nb)
    y = DEEPNORM_ALPHA * x1_ref[...] + gf_ref[pl.ds(r, 1), :] * jnp.transpose(ft_ref[...])
    o_ref[...] = _layer_norm_rows(y, lg_ref[...], lb_ref[...])


def _peer_out_call(ffn_t, x1, mod_l, ln_g, ln_b, *, seq, nb):
    t_all, d = x1.shape
    tm = TM_OUT
    kern = functools.partial(_peer_out_kernel, tiles_per_seq=seq // tm, nb=nb)
    return pl.pallas_call(
        kern,
        grid=(t_all // tm,),
        in_specs=[
            pl.BlockSpec((d, tm), lambda i: (0, i)),
            pl.BlockSpec((tm, d), lambda i: (i, 0)),
            pl.BlockSpec((8, d), lambda i: (0, 5)),
            pl.BlockSpec((1, d), lambda i: (0, 0)),
            pl.BlockSpec((1, d), lambda i: (0, 0)),
        ],
        out_specs=pl.BlockSpec((tm, d), lambda i: (i, 0)),
        out_shape=jax.ShapeDtypeStruct((t_all, d), F32),
        compiler_params=_cparams(("arbitrary",)),
        name="peer_out_ln",
    )(ffn_t, x1, mod_l, ln_g, ln_b)


def _ab_columns(w_tail):
    depth, d, _ = w_tail.shape
    hb = GDN_HB
    n_hg = N_HEADS // hb
    wa = w_tail[:, :, :2 * N_HEADS].reshape(depth, d, 2, n_hg, hb)
    wb = w_tail[:, :, 2 * N_HEADS:].reshape(depth, d, 2, n_hg, hb)

    def lay(w):
        w = w.transpose(0, 1, 3, 2, 4).reshape(depth, d, n_hg, 2 * hb)
        w = jnp.pad(w, ((0, 0), (0, 0), (0, 0), (0, HEAD_DIM - 2 * hb)))
        return w.reshape(depth, d, n_hg * HEAD_DIM)

    return jnp.concatenate([lay(wa), lay(wb)], axis=-1)


def _head_param_lanes(p):
    depth = p.shape[0]
    hb = GDN_HB
    n_hg = N_HEADS // hb
    p = p.reshape(depth, 2, n_hg, hb).transpose(0, 2, 1, 3).reshape(depth, n_hg, 2 * hb)
    p = jnp.pad(p, ((0, 0), (0, 0), (0, HEAD_DIM - 2 * hb)))
    return p.reshape(depth, 1, n_hg * HEAD_DIM)


def kernel(x, c, ctx, c_ctx, w_mod, b_mod, w_in, w_out, na_rpb, gdn_conv, gdn_a_log, gdn_dt_bias, gdn_norm_g,
           peer_wq, peer_subkeys, peer_u, peer_v, ln_g, ln_b):
    nb, seq, d = x.shape
    ctx_len = ctx.shape[1]
    depth = w_mod.shape[0]
    assert d == D_MODEL and nb + 1 <= 8
    assert seq % (NA_QROWS * GRID_W) == 0 and seq % TM_PROJ == 0
    t_lat = nb * seq
    dims = dict(nb=nb, seq=seq, ctx_len=ctx_len)

    xa = jnp.concatenate([x.reshape(t_lat, d), ctx.reshape(nb * ctx_len, d)], axis=0)
    cs = jnp.concatenate([c, c_ctx[None, :], jnp.zeros((8 - nb - 1, d), F32)], axis=0)
    mod = _mod_call(cs, w_mod, b_mod)

    w_main = w_in[:, :, :P_MAIN_W].astype(BF16)
    w_ab = _ab_columns(w_in[:, :, P_MAIN_W:]).astype(BF16)
    w_out_bf = w_out.astype(BF16)
    alog = _head_param_lanes(gdn_a_log)
    dtb = _head_param_lanes(gdn_dt_bias)
    conv_w = jnp.pad(gdn_conv, ((0, 0), (0, 8 - GDN_CONV), (0, 0)))
    rope_c, rope_s = _rope_tables(seq, ctx_len)
    wq_t = peer_wq.transpose(0, 2, 1)
    wq_hi = wq_t.astype(BF16)
    wq_lo = (wq_t - wq_hi.astype(F32)).astype(BF16)
    u_bf = peer_u.astype(BF16)
    vt_bf = peer_v.astype(BF16).transpose(0, 2, 1)

    for l in range(depth):
        p_all, pab = _inproj_call(xa, mod[l], w_main[l], w_ab[l], seq=seq, nb=nb)
        bias = _na_bias(na_rpb[l], seq // GRID_W)
        na_all = _na_call(p_all, bias, **dims)
        na_all = _ctx_attn_call(p_all, na_all, **dims)
        qn, kn, vn, g, beta = _gdn_prep_call(p_all, pab, conv_w[l], alog[l], dtb[l], rope_c, rope_s, **dims)
        o_f, o_b = _gdn_scan_call(qn, kn, vn, g, beta, **dims)
        x1, h2t = _outproj_call(na_all, o_f, o_b, p_all, xa, mod[l], w_out_bf[l], gdn_norm_g[l][None, :],
                                ln_g[l, 0][None, :], ln_b[l, 0][None, :], seq=seq, nb=nb)
        a_t, b_t, stats, hb = _peer_score_call(h2t, wq_hi[l], wq_lo[l], peer_subkeys[l])
        ffn_t = _peer_dense_call(hb, u_bf[l], vt_bf[l], a_t, b_t, stats)
        xa = _peer_out_call(ffn_t, x1, mod[l], ln_g[l, 1][None, :], ln_b[l, 1][None, :], seq=seq, nb=nb)
    return xa[:t_lat].reshape(nb, seq, d)
```

```python
import functools

import numpy as np
import jax
import jax.numpy as jnp
from jax import lax
from jax.experimental import pallas as pl
from jax.experimental.pallas import tpu as pltpu

F32 = jnp.float32
BF16 = jnp.bfloat16
HIGHEST = lax.Precision.HIGHEST

D_MODEL = 2048
HEAD_DIM = 128
N_HEADS = 8
GROUP_W = N_HEADS * HEAD_DIM
GRID_W = 64
NA_KR = 8
NA_KC = 16
NA_QROWS = 8
NA_KROWS = 16
GDN_CHUNK = 64
GDN_CONV = 5
GDN_HB = 4
ROPE_THETA = 10000.0
PEER_HEADS = 8
PEER_TOPK = 16
PEER_NKEYS = 128
PEER_QDIM = 256
N_MOD = 6
LN_EPS = 1e-6
NEG_INF = -1e30
DEPTH_FOR_DEEPNORM = 4
DEEPNORM_ALPHA = (2 * DEPTH_FOR_DEEPNORM) ** 0.25
P_MAIN_W = 7 * GROUP_W
VMEM_LIMIT = 56 * 1024 * 1024

TM_PROJ = 512
TN_PROJ = 1024
TM_OUT = 256
TM_PREP = 256
TM_PEER = 512
TE_PEER = 512


def _cparams(sem):
    return pltpu.CompilerParams(dimension_semantics=sem, vmem_limit_bytes=VMEM_LIMIT)


def _sigmoid(x):
    return 1.0 / (1.0 + jnp.exp(-x))


def _silu(x):
    return x * _sigmoid(x)


def _dot(a, b):
    return jnp.dot(a, b, preferred_element_type=F32)


def _dot_t(a, b):
    return lax.dot_general(a, b, (((1,), (1,)), ((), ())), preferred_element_type=F32)


def _split_bf16(a):
    hi = a.astype(BF16)
    lo = (a - hi.astype(F32)).astype(BF16)
    return hi, lo


def _dot3_split(ah, al, bh, bl):
    return _dot(ah, bh) + (_dot(ah, bl) + _dot(al, bh))


def _layer_norm_rows(y, g, b):
    mu = jnp.mean(y, axis=-1, keepdims=True)
    yc = y - mu
    var = jnp.mean(yc * yc, axis=-1, keepdims=True)
    return yc * lax.rsqrt(var + LN_EPS) * g + b


def _mod_kernel(c_ref, w_ref, b_ref, o_ref):
    s = _silu(c_ref[...])
    o_ref[0] = jnp.dot(s, w_ref[0], precision=HIGHEST, preferred_element_type=F32) + b_ref[0]


def _mod_call(cs, w_mod, b_mod):
    depth, d, n = w_mod.shape
    tn = 1024
    return pl.pallas_call(
        _mod_kernel,
        grid=(depth, n // tn),
        in_specs=[
            pl.BlockSpec((8, d), lambda l, j: (0, 0)),
            pl.BlockSpec((1, d, tn), lambda l, j: (l, 0, j)),
            pl.BlockSpec((1, 1, tn), lambda l, j: (l, 0, j)),
        ],
        out_specs=pl.BlockSpec((1, 8, tn), lambda l, j: (l, 0, j)),
        out_shape=jax.ShapeDtypeStruct((depth, 8, n), F32),
        compiler_params=_cparams(("arbitrary", "arbitrary")),
        name="adaln_mod",
    )(cs, w_mod, b_mod.reshape(depth, 1, n))


def _inproj_kernel(x_ref, sh_ref, sc_ref, w_ref, wab_ref, p_ref, pab_ref, h_scr, *, tiles_per_seq, nb):
    i = pl.program_id(0)
    j = pl.program_id(1)

    @pl.when(j == 0)
    def _():
        r = jnp.minimum(i // tiles_per_seq, nb)
        sh = sh_ref[pl.ds(r, 1), :]
        sc = sc_ref[pl.ds(r, 1), :]
        hb = (x_ref[...] * (1.0 + sc) + sh).astype(BF16)
        h_scr[...] = hb
        pab_ref[...] = _dot(hb, wab_ref[...])

    p_ref[...] = _dot(h_scr[...], w_ref[...])


def _inproj_call(xa, mod_l, w_main, w_ab, *, seq, nb):
    t_all, d = xa.shape
    n = w_main.shape[1]
    nab = w_ab.shape[1]
    tm, tn = TM_PROJ, TN_PROJ
    kern = functools.partial(_inproj_kernel, tiles_per_seq=seq // tm, nb=nb)
    return pl.pallas_call(
        kern,
        grid=(t_all // tm, n // tn),
        in_specs=[
            pl.BlockSpec((tm, d), lambda i, j: (i, 0)),
            pl.BlockSpec((8, d), lambda i, j: (0, 0)),
            pl.BlockSpec((8, d), lambda i, j: (0, 1)),
            pl.BlockSpec((d, tn), lambda i, j: (0, j)),
            pl.BlockSpec((d, nab), lambda i, j: (0, 0)),
        ],
        out_specs=[
            pl.BlockSpec((tm, tn), lambda i, j: (i, j)),
            pl.BlockSpec((tm, nab), lambda i, j: (i, 0)),
        ],
        out_shape=[
            jax.ShapeDtypeStruct((t_all, n), F32),
            jax.ShapeDtypeStruct((t_all, nab), F32),
        ],
        scratch_shapes=[pltpu.VMEM((tm, d), BF16)],
        compiler_params=_cparams(("arbitrary", "arbitrary")),
        name="in_proj",
    )(xa, mod_l, mod_l, w_main, w_ab)


def _na_bias_index_tables(nrows):
    rq, rk = NA_QROWS, NA_KROWS
    big = 1 << 20
    cfg = [(0, 0, nrows), (8, 4, big), (nrows - rq, nrows - rk, nrows)]
    dr = np.zeros((3, rq, rk), np.int32)
    rv = np.zeros((3, rq, rk), bool)
    for v, (r0, start, nr) in enumerate(cfg):
        r = r0 + np.arange(rq)[:, None]
        kr = start + np.arange(rk)[None, :]
        rs = np.clip(r - NA_KR // 2, 0, nr - NA_KR)
        rv[v] = (kr >= rs) & (kr < rs + NA_KR)
        dr[v] = np.clip(kr - r + NA_KR - 1, 0, 2 * NA_KR - 2)
    qc = np.arange(GRID_W)[:, None]
    kc = np.arange(GRID_W)[None, :]
    ws = np.clip(qc - NA_KC // 2, 0, GRID_W - NA_KC)
    cv = (kc >= ws) & (kc < ws + NA_KC)
    dc = np.clip(kc - qc + NA_KC - 1, 0, 2 * NA_KC - 2).astype(np.int32)
    return dr, rv, dc, cv


def _na_bias(rpb_l, nrows):
    dr, rv, dc, cv = _na_bias_index_tables(nrows)
    tc = jnp.where(cv[None, None], rpb_l[:, :, dc], NEG_INF)
    blk = tc[:, dr]
    blk = jnp.where(rv[None, :, :, :, None, None], blk, NEG_INF)
    blk = blk.transpose(0, 1, 2, 4, 3, 5)
    return blk.reshape(N_HEADS, 3, NA_QROWS * GRID_W, NA_KROWS * GRID_W)


def _na_kernel(q_ref, k0, k1, k2, k3, v0, v1, v2, v3, kc_ref, vc_ref, bias_ref, o_ref):
    q = (q_ref[...] * (HEAD_DIM ** -0.5)).astype(BF16)
    kb = 4 * GRID_W
    s = []
    for j, kr in enumerate((k0, k1, k2, k3)):
        s.append(_dot_t(q, kr[...].astype(BF16)) + bias_ref[:, j * kb:(j + 1) * kb])
    s.append(_dot_t(q, kc_ref[...].astype(BF16)))
    m = s[0].max(axis=-1, keepdims=True)
    for t in s[1:]:
        m = jnp.maximum(m, t.max(axis=-1, keepdims=True))
    p = [jnp.exp(t - m) for t in s]
    l = p[0].sum(axis=-1, keepdims=True)
    for t in p[1:]:
        l = l + t.sum(axis=-1, keepdims=True)
    vs = (v0, v1, v2, v3, vc_ref)
    o = _dot(p[0].astype(BF16), vs[0][...].astype(BF16))
    for t, vr in zip(p[1:], vs[1:]):
        o = o + _dot(t.astype(BF16), vr[...].astype(BF16))
    o_ref[...] = o / l


def _na_call(p_all, bias, *, nb, seq, ctx_len):
    t_all = p_all.shape[0]
    nrows = seq // GRID_W
    nrb = nrows // NA_QROWS
    tq = NA_QROWS * GRID_W
    tk = 4 * GRID_W
    assert ctx_len == tk and nrows >= NA_KROWS
    kblocks_per_seq = seq // tk
    max_sb = kblocks_per_seq - 4
    ctx_blk0 = nb * seq // tk

    def sb(r):
        return jnp.clip(2 * r - 1, 0, max_sb)

    def kspec(j, colbase):
        return pl.BlockSpec((tk, HEAD_DIM), lambda h, b, r: (b * kblocks_per_seq + sb(r) + j, colbase + h))

    def variant(r):
        return jnp.where(r == 0, 0, jnp.where(r == nrb - 1, 2, 1))

    in_specs = [pl.BlockSpec((tq, HEAD_DIM), lambda h, b, r: (b * nrb + r, h))]
    in_specs += [kspec(j, N_HEADS) for j in range(4)]
    in_specs += [kspec(j, 2 * N_HEADS) for j in range(4)]
    in_specs += [
        pl.BlockSpec((tk, HEAD_DIM), lambda h, b, r: (ctx_blk0 + b, N_HEADS + h)),
        pl.BlockSpec((tk, HEAD_DIM), lambda h, b, r: (ctx_blk0 + b, 2 * N_HEADS + h)),
        pl.BlockSpec((None, None, tq, NA_KROWS * GRID_W), lambda h, b, r: (h, variant(r), 0, 0)),
    ]
    return pl.pallas_call(
        _na_kernel,
        grid=(N_HEADS, nb, nrb),
        in_specs=in_specs,
        out_specs=pl.BlockSpec((tq, HEAD_DIM), lambda h, b, r: (b * nrb + r, h)),
        out_shape=jax.ShapeDtypeStruct((nb * seq, GROUP_W), F32),
        compiler_params=_cparams(("arbitrary", "arbitrary", "arbitrary")),
        name="na_attention",
    )(*([p_all] * 11), bias)


def _ctx_attn_kernel(q_ref, k_ref, v_ref, o_ref):
    q = q_ref[...].astype(BF16)
    s = _dot_t(q, k_ref[...].astype(BF16)) * (HEAD_DIM ** -0.5)
    m = s.max(axis=-1, keepdims=True)
    p = jnp.exp(s - m)
    l = p.sum(axis=-1, keepdims=True)
    o_ref[...] = _dot(p.astype(BF16), v_ref[...].astype(BF16)) / l


def _ctx_attn_call(p_all, *, nb, seq, ctx_len):
    blk0 = nb * seq // ctx_len
    return pl.pallas_call(
        _ctx_attn_kernel,
        grid=(nb, N_HEADS),
        in_specs=[
            pl.BlockSpec((ctx_len, HEAD_DIM), lambda b, h: (blk0 + b, h)),
            pl.BlockSpec((ctx_len, HEAD_DIM), lambda b, h: (blk0 + b, N_HEADS + h)),
            pl.BlockSpec((ctx_len, HEAD_DIM), lambda b, h: (blk0 + b, 2 * N_HEADS + h)),
        ],
        out_specs=pl.BlockSpec((ctx_len, HEAD_DIM), lambda b, h: (b, h)),
        out_shape=jax.ShapeDtypeStruct((nb * ctx_len, GROUP_W), F32),
        compiler_params=_cparams(("arbitrary", "arbitrary")),
        name="ctx_attention",
    )(p_all, p_all, p_all)


def _rope_tables(seq, ctx_len):
    t = np.arange(seq)
    row = (t // GRID_W).astype(np.float32)
    col = (t % GRID_W).astype(np.float32)
    n_freq = HEAD_DIM // 4
    inv = (ROPE_THETA ** (-np.arange(n_freq, dtype=np.float32) / n_freq)).astype(np.float32)
    ang = jnp.stack([jnp.asarray(row)[:, None] * inv, jnp.asarray(col)[:, None] * inv], axis=1)
    cos, sin = jnp.cos(ang), jnp.sin(ang)
    c = jnp.concatenate([cos, cos], axis=-1).reshape(seq, HEAD_DIM)
    s = jnp.concatenate([-sin, sin], axis=-1).reshape(seq, HEAD_DIM)
    c = jnp.concatenate([c, jnp.ones((ctx_len, HEAD_DIM), F32)], axis=0)
    s = jnp.concatenate([s, jnp.zeros((ctx_len, HEAD_DIM), F32)], axis=0)
    return c, s


def _gdn_prep_kernel(cur_ref, prev_ref, next_ref, pab_ref, cw_ref, alog_ref, dtb_ref, rc_ref, rs_ref,
                     q_ref, k_ref, v_ref, g_ref, b_ref, ext_ref, *, n_lat_tiles, tps, nab_half):
    i = pl.program_id(0)
    tm = TM_PREP
    is_ctx = i >= n_lat_tiles
    first = jnp.logical_or(is_ctx, i % tps == 0)
    last = jnp.logical_or(is_ctx, i % tps == tps - 1)
    ext_ref[8:8 + tm, :] = cur_ref[...]
    ext_ref[0:8, :] = jnp.where(first, 0.0, prev_ref[...])
    ext_ref[8 + tm:16 + tm, :] = jnp.where(last, 0.0, next_ref[...])

    lane = lax.broadcasted_iota(jnp.int32, (tm, HEAD_DIM), 1)
    half0 = (lane % (HEAD_DIM // 2)) < (HEAD_DIM // 4)
    rc = rc_ref[...]
    rs = rs_ref[...]
    base = 8 - GDN_CONV // 2
    outs = (q_ref, k_ref, v_ref)
    for part in range(3):
        for h in range(N_HEADS):
            c0 = part * GROUP_W + h * HEAD_DIM
            acc = cw_ref[0:1, c0:c0 + HEAD_DIM] * ext_ref[base:base + tm, c0:c0 + HEAD_DIM]
            for t in range(1, GDN_CONV):
                acc = acc + cw_ref[t:t + 1, c0:c0 + HEAD_DIM] * ext_ref[base + t:base + t + tm, c0:c0 + HEAD_DIM]
            y = _silu(acc)
            if part < 2:
                y = y * lax.rsqrt(jnp.sum(y * y, axis=-1, keepdims=True) + 1e-6)
                partner = jnp.where(half0, pltpu.roll(y, HEAD_DIM - HEAD_DIM // 4, 1),
                                    pltpu.roll(y, HEAD_DIM // 4, 1))
                y = y * rc + partner * rs
                if part == 0:
                    y = y * (HEAD_DIM ** -0.5)
            outs[part][:, h * HEAD_DIM:(h + 1) * HEAD_DIM] = y

    a = pab_ref[:, 0:nab_half] + dtb_ref[...]
    softplus = jnp.maximum(a, 0.0) + jnp.log1p(jnp.exp(-jnp.abs(a)))
    g_ref[...] = -jnp.exp(alog_ref[...]) * softplus
    b_ref[...] = _sigmoid(pab_ref[:, nab_half:2 * nab_half])


def _gdn_prep_call(p_all, pab, conv_w, alog, dtb, rope_c, rope_s, *, nb, seq, ctx_len):
    t_all = p_all.shape[0]
    tm = TM_PREP
    assert ctx_len == tm
    n_tiles = t_all // tm
    n_lat_tiles = nb * seq // tm
    tps = seq // tm
    nab_half = pab.shape[1] // 2
    qkv_w = 3 * GROUP_W
    last8 = t_all // 8 - 1
    kern = functools.partial(_gdn_prep_kernel, n_lat_tiles=n_lat_tiles, tps=tps, nab_half=nab_half)

    def rope_idx(i):
        return jnp.where(i < n_lat_tiles, i % tps, tps)

    return pl.pallas_call(
        kern,
        grid=(n_tiles,),
        in_specs=[
            pl.BlockSpec((tm, qkv_w), lambda i: (i, 1)),
            pl.BlockSpec((8, qkv_w), lambda i: (jnp.maximum(i * (tm // 8) - 1, 0), 1)),
            pl.BlockSpec((8, qkv_w), lambda i: (jnp.minimum((i + 1) * (tm // 8), last8), 1)),
            pl.BlockSpec((tm, 2 * nab_half), lambda i: (i, 0)),
            pl.BlockSpec((8, qkv_w), lambda i: (0, 0)),
            pl.BlockSpec((1, nab_half), lambda i: (0, 0)),
            pl.BlockSpec((1, nab_half), lambda i: (0, 0)),
            pl.BlockSpec((tm, HEAD_DIM), lambda i: (rope_idx(i), 0)),
            pl.BlockSpec((tm, HEAD_DIM), lambda i: (rope_idx(i), 0)),
        ],
        out_specs=[
            pl.BlockSpec((tm, GROUP_W), lambda i: (i, 0)),
            pl.BlockSpec((tm, GROUP_W), lambda i: (i, 0)),
            pl.BlockSpec((tm, GROUP_W), lambda i: (i, 0)),
            pl.BlockSpec((tm, nab_half), lambda i: (i, 0)),
            pl.BlockSpec((tm, nab_half), lambda i: (i, 0)),
        ],
        out_shape=[
            jax.ShapeDtypeStruct((t_all, GROUP_W), F32),
            jax.ShapeDtypeStruct((t_all, GROUP_W), F32),
            jax.ShapeDtypeStruct((t_all, GROUP_W), F32),
            jax.ShapeDtypeStruct((t_all, nab_half), F32),
            jax.ShapeDtypeStruct((t_all, nab_half), F32),
        ],
        scratch_shapes=[pltpu.VMEM((tm + 16, qkv_w), F32)],
        compiler_params=_cparams(("arbitrary",)),
        name="gdn_prep",
    )(p_all, p_all, p_all, pab, conv_w, alog, dtb, rope_c, rope_s)


def _unit_tri_inverse(ns, orders):
    c = GDN_CHUNK
    ds = []
    for n, (row, col) in zip(ns, orders):
        m1 = jnp.logical_and(jnp.logical_and((row >> 1) == (col >> 1), (row & 1) == 1), (col & 1) == 0)
        ds.append((row == col).astype(F32) - jnp.where(m1, n, 0.0))
    k = 2
    while k < c:
        sh = int(np.log2(2 * k))
        dsp, xs = [], []
        for n, d, (row, col) in zip(ns, ds, orders):
            mk = jnp.logical_and((row >> sh) == (col >> sh),
                                 jnp.logical_and((row & (2 * k - 1)) >= k, (col & (2 * k - 1)) < k))
            lh, ll = _split_bf16(jnp.where(mk, n, 0.0))
            dh, dl = _split_bf16(d)
            dsp.append((dh, dl))
            xs.append(_dot3_split(lh, ll, dh, dl))
        ds = [d - _dot3_split(dh, dl, *_split_bf16(x)) for d, (dh, dl), x in zip(ds, dsp, xs)]
        k *= 2
    return ds


def _gdn_scan_kernel(qf, kf, vf, gf, bf, qb, kb, vb, gb, bb, of_ref, ob_ref, st_ref, *, hb):
    s = pl.program_id(2)
    c = GDN_CHUNK

    @pl.when(s == 0)
    def _():
        st_ref[...] = jnp.zeros(st_ref.shape, F32)

    ri = lax.broadcasted_iota(jnp.int32, (c, c), 0)
    ci = lax.broadcasted_iota(jnp.int32, (c, c), 1)
    zpad = jnp.zeros((HEAD_DIM - c, HEAD_DIM), F32)
    ch = []
    for d, (q_ref, k_ref, v_ref, g_ref, be_ref, o_ref) in enumerate(
            ((qf, kf, vf, gf, bf, of_ref), (qb, kb, vb, gb, bb, ob_ref))):
        incl = (ri >= ci) if d == 0 else (ri <= ci)
        gc_all = jnp.dot(incl.astype(F32), g_ref[...], precision=HIGHEST, preferred_element_type=F32)
        gc_t = jnp.transpose(jnp.concatenate([gc_all, zpad], axis=0))
        beta_all = be_ref[...]
        for hh in range(hb):
            ln = d * hb + hh
            sl = slice(hh * HEAD_DIM, (hh + 1) * HEAD_DIM)
            ch.append(dict(
                ln=ln, sl=sl, o_ref=o_ref, incl=incl,
                strict=(ri > ci) if d == 0 else (ri < ci),
                order=(ri, ci) if d == 0 else (ci, ri),
                last=c - 1 if d == 0 else 0,
                gcc=gc_all[:, ln:ln + 1], grow=gc_t[ln:ln + 1, 0:c], beta=beta_all[:, ln:ln + 1],
                q=q_ref[:, sl], k=k_ref[:, sl], v=v_ref[:, sl]))
    for t in ch:
        t["decay"] = jnp.where(t["incl"], jnp.exp(jnp.where(t["incl"], t["gcc"] - t["grow"], 0.0)), 0.0)
        t["kbeta"] = t["k"] * t["beta"]
    for t in ch:
        t["kk"] = _dot_t(t["kbeta"], t["k"])
    for t in ch:
        t["qk"] = _dot_t(t["q"], t["k"]) * t["decay"]
    ns = [jnp.where(t["strict"], t["kk"] * t["decay"], 0.0) for t in ch]
    tinvs = _unit_tri_inverse(ns, [t["order"] for t in ch])
    for t, tinv in zip(ch, tinvs):
        eg = jnp.exp(t["gcc"])
        t["eg"] = eg
        rhs = jnp.concatenate([t["v"] * t["beta"], t["kbeta"] * eg], axis=1)
        t["sol"] = _dot3_split(*_split_bf16(tinv), *_split_bf16(rhs))
    for t in ch:
        t["state"] = st_ref[t["ln"]]
        t["v_new"] = t["sol"][:, :HEAD_DIM] - _dot(t["sol"][:, HEAD_DIM:], t["state"])
    for t in ch:
        t["o_ref"][:, t["sl"]] = _dot(t["q"] * t["eg"], t["state"]) + _dot(t["qk"], t["v_new"])
    for t in ch:
        glast = t["gcc"][t["last"]:t["last"] + 1, :]
        kd_t = jnp.transpose(jnp.concatenate([t["k"] * jnp.exp(glast - t["gcc"]), zpad], axis=0))[:, 0:c]
        st_ref[t["ln"]] = t["state"] * jnp.exp(glast) + _dot(kd_t, t["v_new"])


def _gdn_scan_call(qn, kn, vn, g, beta, *, nb, seq, ctx_len):
    t_all = qn.shape[0]
    hb = GDN_HB
    c = GDN_CHUNK
    ncc = ctx_len // c
    ncl = seq // c
    ctx0 = nb * ncl
    n_hg = N_HEADS // hb

    def row_f(b, s):
        return jnp.where(s < ncc, ctx0 + b * ncc + s, b * ncl + (s - ncc))

    def row_b(b, s):
        return jnp.where(s < ncc, ctx0 + b * ncc + (ncc - 1 - s), b * ncl + (ncl - 1 - (s - ncc)))

    def specs(rowfn):
        big = pl.BlockSpec((c, hb * HEAD_DIM), lambda b, hg, s: (rowfn(b, s), hg))
        small = pl.BlockSpec((c, HEAD_DIM), lambda b, hg, s: (rowfn(b, s), hg))
        return [big, big, big, small, small]

    out_f = pl.BlockSpec((c, hb * HEAD_DIM), lambda b, hg, s: (row_f(b, s), hg))
    out_b = pl.BlockSpec((c, hb * HEAD_DIM), lambda b, hg, s: (row_b(b, s), hg))
    return pl.pallas_call(
        functools.partial(_gdn_scan_kernel, hb=hb),
        grid=(nb, n_hg, ncc + ncl),
        in_specs=specs(row_f) + specs(row_b),
        out_specs=[out_f, out_b],
        out_shape=[jax.ShapeDtypeStruct((t_all, GROUP_W), F32)] * 2,
        scratch_shapes=[pltpu.VMEM((2 * hb, HEAD_DIM, HEAD_DIM), F32)],
        compiler_params=_cparams(("arbitrary", "arbitrary", "arbitrary")),
        name="gdn_scan",
    )(qn, kn, vn, g, beta, qn, kn, vn, g, beta)


def _outproj_kernel(nal_ref, nac_ref, of_ref, ob_ref, z_ref, x_ref, ga_ref, shf_ref, scf_ref, w_ref, ng_ref, lg_ref,
                    lb_ref, x1_ref, h2t_ref, gdn_scr, *, tiles_per_seq, nb):
    i = pl.program_id(0)
    r = jnp.minimum(i // tiles_per_seq, nb)
    na = jnp.where(i < nb * tiles_per_seq, nal_ref[...], nac_ref[...])
    o = of_ref[...] + ob_ref[...]
    ng = ng_ref[...]
    for h in range(N_HEADS):
        sl = slice(h * HEAD_DIM, (h + 1) * HEAD_DIM)
        oh = o[:, sl]
        oh = oh * lax.rsqrt(jnp.mean(oh * oh, axis=-1, keepdims=True) + 1e-6) * ng
        gdn_scr[:, sl] = (oh * _silu(z_ref[:, sl])).astype(BF16)
    mix = _dot(na.astype(BF16), w_ref[0:GROUP_W, :]) + _dot(gdn_scr[...], w_ref[GROUP_W:2 * GROUP_W, :])
    y = DEEPNORM_ALPHA * x_ref[...] + ga_ref[pl.ds(r, 1), :] * mix
    x1 = _layer_norm_rows(y, lg_ref[...], lb_ref[...])
    x1_ref[...] = x1
    h2 = x1 * (1.0 + scf_ref[pl.ds(r, 1), :]) + shf_ref[pl.ds(r, 1), :]
    h2t_ref[...] = jnp.transpose(h2).astype(BF16)


def _outproj_call(na_lat, na_ctx, o_f, o_b, p_all, xa, mod_l, w_out, norm_g, ln_g, ln_b, *, seq, nb):
    t_all, d = xa.shape
    tm = TM_OUT
    n_lat_tiles = nb * seq // tm
    kern = functools.partial(_outproj_kernel, tiles_per_seq=seq // tm, nb=nb)
    row = lambda i: (i, 0)
    const = lambda i: (0, 0)
    return pl.pallas_call(
        kern,
        grid=(t_all // tm,),
        in_specs=[
            pl.BlockSpec((tm, GROUP_W), lambda i: (jnp.minimum(i, n_lat_tiles - 1), 0)),
            pl.BlockSpec((tm, GROUP_W), lambda i: (jnp.maximum(i - n_lat_tiles, 0), 0)),
            pl.BlockSpec((tm, GROUP_W), row),
            pl.BlockSpec((tm, GROUP_W), row),
            pl.BlockSpec((tm, GROUP_W), lambda i: (i, 6)),
            pl.BlockSpec((tm, d), row),
            pl.BlockSpec((8, d), lambda i: (0, 2)),
            pl.BlockSpec((8, d), lambda i: (0, 3)),
            pl.BlockSpec((8, d), lambda i: (0, 4)),
            pl.BlockSpec((d, d), const),
            pl.BlockSpec((1, HEAD_DIM), const),
            pl.BlockSpec((1, d), const),
            pl.BlockSpec((1, d), const),
        ],
        out_specs=[
            pl.BlockSpec((tm, d), row),
            pl.BlockSpec((d, tm), lambda i: (0, i)),
        ],
        out_shape=[
            jax.ShapeDtypeStruct((t_all, d), F32),
            jax.ShapeDtypeStruct((d, t_all), BF16),
        ],
        scratch_shapes=[pltpu.VMEM((tm, GROUP_W), BF16)],
        compiler_params=_cparams(("arbitrary",)),
        name="out_proj_ln",
    )(na_lat, na_ctx, o_f, o_b, p_all, xa, mod_l, mod_l, mod_l, w_out, norm_g, ln_g, ln_b)


def _peer_score_kernel(ht_ref, w_ref, sk_ref, a_ref, b_ref, st_ref, s_scr, atop, btop, cand):
    tm = TM_PEER
    qt = _dot(w_ref[...], ht_ref[...])
    half = PEER_QDIM // 2
    s_scr[0] = _dot(sk_ref[0], qt[0:half])
    s_scr[1] = _dot(sk_ref[1], qt[half:2 * half])
    a_ref[0] = s_scr[0]
    b_ref[0] = s_scr[1]
    n_half = PEER_TOPK // 2
    for tc in range(tm // 128):
        ls = slice(tc * 128, (tc + 1) * 128)
        for p, out_ref in enumerate((atop, btop)):
            sc = s_scr[p, :, ls]
            for k in range(PEER_TOPK):
                m = jnp.max(sc, axis=0, keepdims=True)
                out_ref[k:k + 1, ls] = m
                sc = jnp.where(sc == m, -jnp.inf, sc)
        bt = btop[:, ls]
        for i in range(n_half):
            cand[i * PEER_TOPK:(i + 1) * PEER_TOPK, ls] = atop[i:i + 1, ls] + bt
        cand[n_half * PEER_TOPK:n_half * PEER_TOPK + n_half, ls] = atop[n_half:PEER_TOPK, ls] + bt[0:1, :]
        cv = cand[:, ls]
        m0 = jnp.max(cv, axis=0, keepdims=True)
        z = jnp.zeros_like(m0)
        m = m0
        for k in range(PEER_TOPK):
            m = jnp.max(cv, axis=0, keepdims=True)
            z = z + jnp.exp(m - m0)
            cv = jnp.where(cv == m, -jnp.inf, cv)
        st_ref[0, 0:1, ls] = m
        st_ref[0, 1:2, ls] = atop[0:1, ls]
        st_ref[0, 2:3, ls] = bt[0:1, :]
        st_ref[0, 3:4, ls] = 1.0 / z
    st_ref[0, 4:8, :] = jnp.zeros((4, tm), F32)


def _peer_score_call(h2t, wq_t, subkeys):
    d, t_all = h2t.shape
    tm = TM_PEER
    ncand = PEER_TOPK * PEER_TOPK // 2 + PEER_TOPK // 2
    tok = lambda i, h: (h, 0, i)
    return pl.pallas_call(
        _peer_score_kernel,
        grid=(t_all // tm, PEER_HEADS),
        in_specs=[
            pl.BlockSpec((d, tm), lambda i, h: (0, i)),
            pl.BlockSpec((PEER_QDIM, d), lambda i, h: (h, 0)),
            pl.BlockSpec((2, PEER_NKEYS, PEER_QDIM // 2), lambda i, h: (0, 0, 0)),
        ],
        out_specs=[
            pl.BlockSpec((1, PEER_NKEYS, tm), tok),
            pl.BlockSpec((1, PEER_NKEYS, tm), tok),
            pl.BlockSpec((1, 8, tm), tok),
        ],
        out_shape=[
            jax.ShapeDtypeStruct((PEER_HEADS, PEER_NKEYS, t_all), F32),
            jax.ShapeDtypeStruct((PEER_HEADS, PEER_NKEYS, t_all), F32),
            jax.ShapeDtypeStruct((PEER_HEADS, 8, t_all), F32),
        ],
        scratch_shapes=[
            pltpu.VMEM((2, PEER_NKEYS, tm), F32),
            pltpu.VMEM((PEER_TOPK, tm), F32),
            pltpu.VMEM((PEER_TOPK, tm), F32),
            pltpu.VMEM((ncand, tm), F32),
        ],
        compiler_params=_cparams(("arbitrary", "arbitrary")),
        name="peer_scores",
    )(h2t, wq_t, subkeys)


def _peer_dense_kernel(h_ref, u_ref, vt_ref, a_ref, b_ref, st_ref, o_ref, eb_ref, act0, act1, w0, w1):
    m = pl.program_id(1)
    n_tiles = pl.num_programs(1) - 2
    tm = TM_PEER
    nk = PEER_NKEYS
    rows_per_step = TE_PEER // nk
    jq_rows = 32
    n_jq = nk // jq_rows

    @pl.when(m == 0)
    def _():
        o_ref[...] = jnp.zeros(o_ref.shape, F32)
        for r in (act0, act1):
            r[...] = jnp.zeros(r.shape, F32)
        for r in (w0, w1):
            r[...] = jnp.zeros(r.shape, BF16)
        for h in range(PEER_HEADS):
            eb_ref[h] = jnp.exp(b_ref[h] - st_ref[h, 2:3, :])

    valid_b = jnp.logical_and(m >= 1, m <= n_tiles)
    sqrt_half = np.float32(np.sqrt(0.5))

    def stages(act_a, w_c, act_b, w_b):
        n_tc = tm // 128

        def stage_a(n, r):
            cs = slice(n * 256, (n + 1) * 256)
            rs = slice(r * 256, (r + 1) * 256)
            act_a[rs, cs] = _dot(u_ref[rs, :], h_ref[:, cs])

        def stage_c(q):
            rs = slice(q * 256, (q + 1) * 256)
            o_ref[rs, :] += _dot(vt_ref[rs, :], w_c[...])

        def stage_b(tc, jq):
            ls = slice(tc * 128, (tc + 1) * 128)
            js = slice(jq * (jq_rows // 8), (jq + 1) * (jq_rows // 8))
            g = [jnp.zeros((jq_rows // 8, 8, 128), F32) for _ in range(rows_per_step)]
            for h in range(PEER_HEADS):
                ar = a_ref[h, :, ls]
                er = jnp.exp(ar - st_ref[h, 1:2, ls]) * st_ref[h, 3:4, ls]
                tau = jnp.broadcast_to(st_ref[h, 0:1, ls], (8, 128))
                bb = b_ref[h, js, :, ls]
                ee = eb_ref[h, js, :, ls]
                for ii in range(rows_per_step):
                    a_b = jnp.broadcast_to(ar[ii:ii + 1, :], (8, 128))
                    e_b = jnp.broadcast_to(er[ii:ii + 1, :], (8, 128))
                    g[ii] = g[ii] + jnp.where(bb + a_b >= tau, ee * e_b, 0.0)
            for ii in range(rows_per_step):
                r0 = ii * nk + jq * jq_rows
                xa = act_b[r0:r0 + jq_rows, ls]
                ge = 0.5 * xa * (1.0 + lax.erf(xa * sqrt_half))
                wv = jnp.where(valid_b, ge * g[ii].reshape(jq_rows, 128), 0.0)
                w_b[r0:r0 + jq_rows, ls] = wv.astype(BF16)

        mxu = [lambda: stage_a(0, 0), lambda: stage_c(0), lambda: stage_c(1), lambda: stage_a(0, 1),
               lambda: stage_c(2), lambda: stage_c(3), lambda: stage_a(1, 0), lambda: stage_c(4),
               lambda: stage_c(5), lambda: stage_a(1, 1), lambda: stage_c(6), lambda: stage_c(7)]
        blocks = [(tc, jq) for tc in range(n_tc) for jq in range(n_jq)]
        mi = 0
        for bi, (tc, jq) in enumerate(blocks):
            while mi < len(mxu) and mi * len(blocks) <= bi * len(mxu):
                mxu[mi]()
                mi += 1
            stage_b(tc, jq)
        while mi < len(mxu):
            mxu[mi]()
            mi += 1

    @pl.when(m % 2 == 0)
    def _():
        stages(act0, w0, act1, w1)

    @pl.when(m % 2 == 1)
    def _():
        stages(act1, w1, act0, w0)


def _peer_dense_call(hb, u_bf, vt_bf, a_t, b_t, stats):
    d, t_all = hb.shape
    ne = u_bf.shape[0]
    tm, te = TM_PEER, TE_PEER
    n_tiles = ne // te
    rows = te // PEER_NKEYS
    tok3 = lambda i, m: (0, 0, i)
    a4 = a_t.reshape(PEER_HEADS, n_tiles, rows, t_all)
    b4 = b_t.reshape(PEER_HEADS, PEER_NKEYS // 8, 8, t_all)
    return pl.pallas_call(
        _peer_dense_kernel,
        grid=(t_all // tm, n_tiles + 2),
        in_specs=[
            pl.BlockSpec((d, tm), lambda i, m: (0, i)),
            pl.BlockSpec((te, d), lambda i, m: (jnp.minimum(m, n_tiles - 1), 0)),
            pl.BlockSpec((d, te), lambda i, m: (0, jnp.maximum(m - 2, 0))),
            pl.BlockSpec((PEER_HEADS, None, rows, tm), lambda i, m: (0, jnp.clip(m - 1, 0, n_tiles - 1), 0, i)),
            pl.BlockSpec((PEER_HEADS, PEER_NKEYS // 8, 8, tm), lambda i, m: (0, 0, 0, i)),
            pl.BlockSpec((PEER_HEADS, 8, tm), tok3),
        ],
        out_specs=pl.BlockSpec((d, tm), lambda i, m: (0, i)),
        out_shape=jax.ShapeDtypeStruct((d, t_all), F32),
        scratch_shapes=[
            pltpu.VMEM((PEER_HEADS, PEER_NKEYS // 8, 8, tm), F32),
            pltpu.VMEM((te, tm), F32),
            pltpu.VMEM((te, tm), F32),
            pltpu.VMEM((te, tm), BF16),
            pltpu.VMEM((te, tm), BF16),
        ],
        compiler_params=_cparams(("arbitrary", "arbitrary")),
        name="peer_dense",
    )(hb, u_bf, vt_bf, a4, b4, stats)


def _peer_out_kernel(ft_ref, x1_ref, gf_ref, lg_ref, lb_ref, o_ref, *, tiles_per_seq, nb):
    i = pl.program_id(0)
    r = jnp.minimum(i // tiles_per_seq, nb)
    y = DEEPNORM_ALPHA * x1_ref[...] + gf_ref[pl.ds(r, 1), :] * jnp.transpose(ft_ref[...])
    o_ref[...] = _layer_norm_rows(y, lg_ref[...], lb_ref[...])


def _peer_out_call(ffn_t, x1, mod_l, ln_g, ln_b, *, seq, nb):
    t_all, d = x1.shape
    tm = TM_OUT
    kern = functools.partial(_peer_out_kernel, tiles_per_seq=seq // tm, nb=nb)
    return pl.pallas_call(
        kern,
        grid=(t_all // tm,),
        in_specs=[
            pl.BlockSpec((d, tm), lambda i: (0, i)),
            pl.BlockSpec((tm, d), lambda i: (i, 0)),
            pl.BlockSpec((8, d), lambda i: (0, 5)),
            pl.BlockSpec((1, d), lambda i: (0, 0)),
            pl.BlockSpec((1, d), lambda i: (0, 0)),
        ],
        out_specs=pl.BlockSpec((tm, d), lambda i: (i, 0)),
        out_shape=jax.ShapeDtypeStruct((t_all, d), F32),
        compiler_params=_cparams(("arbitrary",)),
        name="peer_out_ln",
    )(ffn_t, x1, mod_l, ln_g, ln_b)


def _ab_columns(w_tail):
    depth, d, _ = w_tail.shape
    hb = GDN_HB
    n_hg = N_HEADS // hb
    wa = w_tail[:, :, :2 * N_HEADS].reshape(depth, d, 2, n_hg, hb)
    wb = w_tail[:, :, 2 * N_HEADS:].reshape(depth, d, 2, n_hg, hb)

    def lay(w):
        w = w.transpose(0, 1, 3, 2, 4).reshape(depth, d, n_hg, 2 * hb)
        w = jnp.pad(w, ((0, 0), (0, 0), (0, 0), (0, HEAD_DIM - 2 * hb)))
        return w.reshape(depth, d, n_hg * HEAD_DIM)

    return jnp.concatenate([lay(wa), lay(wb)], axis=-1)


def _head_param_lanes(p):
    depth = p.shape[0]
    hb = GDN_HB
    n_hg = N_HEADS // hb
    p = p.reshape(depth, 2, n_hg, hb).transpose(0, 2, 1, 3).reshape(depth, n_hg, 2 * hb)
    p = jnp.pad(p, ((0, 0), (0, 0), (0, HEAD_DIM - 2 * hb)))
    return p.reshape(depth, 1, n_hg * HEAD_DIM)


def kernel(x, c, ctx, c_ctx, w_mod, b_mod, w_in, w_out, na_rpb, gdn_conv, gdn_a_log, gdn_dt_bias, gdn_norm_g,
           peer_wq, peer_subkeys, peer_u, peer_v, ln_g, ln_b):
    nb, seq, d = x.shape
    ctx_len = ctx.shape[1]
    depth = w_mod.shape[0]
    assert d == D_MODEL and nb + 1 <= 8
    assert seq % (NA_QROWS * GRID_W) == 0 and seq % TM_PROJ == 0
    t_lat = nb * seq
    dims = dict(nb=nb, seq=seq, ctx_len=ctx_len)

    xa = jnp.concatenate([x.reshape(t_lat, d), ctx.reshape(nb * ctx_len, d)], axis=0)
    cs = jnp.concatenate([c, c_ctx[None, :], jnp.zeros((8 - nb - 1, d), F32)], axis=0)
    mod = _mod_call(cs, w_mod, b_mod)

    w_main = w_in[:, :, :P_MAIN_W].astype(BF16)
    w_ab = _ab_columns(w_in[:, :, P_MAIN_W:]).astype(BF16)
    w_out_bf = w_out.astype(BF16)
    alog = _head_param_lanes(gdn_a_log)
    dtb = _head_param_lanes(gdn_dt_bias)
    conv_w = jnp.pad(gdn_conv, ((0, 0), (0, 8 - GDN_CONV), (0, 0)))
    rope_c, rope_s = _rope_tables(seq, ctx_len)
    wq_t = peer_wq.astype(BF16).transpose(0, 2, 1)
    u_bf = peer_u.astype(BF16)
    vt_bf = peer_v.astype(BF16).transpose(0, 2, 1)

    for l in range(depth):
        p_all, pab = _inproj_call(xa, mod[l], w_main[l], w_ab[l], seq=seq, nb=nb)
        bias = _na_bias(na_rpb[l], seq // GRID_W)
        na_lat = _na_call(p_all, bias, **dims)
        na_ctx = _ctx_attn_call(p_all, **dims)
        qn, kn, vn, g, beta = _gdn_prep_call(p_all, pab, conv_w[l], alog[l], dtb[l], rope_c, rope_s, **dims)
        o_f, o_b = _gdn_scan_call(qn, kn, vn, g, beta, **dims)
        x1, h2t = _outproj_call(na_lat, na_ctx, o_f, o_b, p_all, xa, mod[l], w_out_bf[l], gdn_norm_g[l][None, :],
                                ln_g[l, 0][None, :], ln_b[l, 0][None, :], seq=seq, nb=nb)
        a_t, b_t, stats = _peer_score_call(h2t, wq_t[l], peer_subkeys[l])
        ffn_t = _peer_dense_call(h2t, u_bf[l], vt_bf[l], a_t, b_t, stats)
        xa = _peer_out_call(ffn_t, x1, mod[l], ln_g[l, 1][None, :], ln_b[l, 1][None, :], seq=seq, nb=nb)
    return xa[:t_lat].reshape(nb, seq, d)
```

```python
import functools

import numpy as np
import jax
import jax.numpy as jnp
from jax import lax
from jax.experimental import pallas as pl
from jax.experimental.pallas import tpu as pltpu

F32 = jnp.float32
BF16 = jnp.bfloat16
HIGHEST = lax.Precision.HIGHEST

D_MODEL = 2048
HEAD_DIM = 128
N_HEADS = 8
GROUP_W = N_HEADS * HEAD_DIM
GRID_W = 64
NA_KR = 8
NA_KC = 16
NA_QROWS = 8
NA_KROWS = 16
GDN_CHUNK = 64
GDN_CONV = 5
GDN_HB = 8
ROPE_THETA = 10000.0
PEER_HEADS = 8
PEER_TOPK = 16
PEER_NKEYS = 128
PEER_QDIM = 256
N_MOD = 6
LN_EPS = 1e-6
NEG_INF = -1e30
DEPTH_FOR_DEEPNORM = 4
DEEPNORM_ALPHA = (2 * DEPTH_FOR_DEEPNORM) ** 0.25
P_MAIN_W = 7 * GROUP_W
VMEM_LIMIT = 56 * 1024 * 1024

TM_PROJ = 512
TN_PROJ = 1024
TM_OUT = 256
TM_PREP = 256
TM_PEER = 512
TE_PEER = 512


def _cparams(sem):
    return pltpu.CompilerParams(dimension_semantics=sem, vmem_limit_bytes=VMEM_LIMIT)


def _sigmoid(x):
    return 1.0 / (1.0 + jnp.exp(-x))


def _silu(x):
    return x * _sigmoid(x)


def _dot(a, b):
    return jnp.dot(a, b, preferred_element_type=F32)


def _dot_t(a, b):
    return lax.dot_general(a, b, (((1,), (1,)), ((), ())), preferred_element_type=F32)


def _split_bf16(a):
    hi = a.astype(BF16)
    lo = (a - hi.astype(F32)).astype(BF16)
    return hi, lo


def _dot3_split(ah, al, bh, bl):
    return _dot(ah, bh) + (_dot(ah, bl) + _dot(al, bh))


def _layer_norm_rows(y, g, b):
    mu = jnp.mean(y, axis=-1, keepdims=True)
    yc = y - mu
    var = jnp.mean(yc * yc, axis=-1, keepdims=True)
    return yc * lax.rsqrt(var + LN_EPS) * g + b


def _mod_kernel(c_ref, w_ref, b_ref, o_ref):
    s = _silu(c_ref[...])
    o_ref[0] = jnp.dot(s, w_ref[0], precision=HIGHEST, preferred_element_type=F32) + b_ref[0]


def _mod_call(cs, w_mod, b_mod):
    depth, d, n = w_mod.shape
    tn = 1024
    return pl.pallas_call(
        _mod_kernel,
        grid=(depth, n // tn),
        in_specs=[
            pl.BlockSpec((8, d), lambda l, j: (0, 0)),
            pl.BlockSpec((1, d, tn), lambda l, j: (l, 0, j)),
            pl.BlockSpec((1, 1, tn), lambda l, j: (l, 0, j)),
        ],
        out_specs=pl.BlockSpec((1, 8, tn), lambda l, j: (l, 0, j)),
        out_shape=jax.ShapeDtypeStruct((depth, 8, n), F32),
        compiler_params=_cparams(("arbitrary", "arbitrary")),
        name="adaln_mod",
    )(cs, w_mod, b_mod.reshape(depth, 1, n))


def _inproj_kernel(x_ref, sh_ref, sc_ref, w_ref, wab_ref, p_ref, pab_ref, h_scr, *, tiles_per_seq, nb):
    i = pl.program_id(0)
    j = pl.program_id(1)

    @pl.when(j == 0)
    def _():
        r = jnp.minimum(i // tiles_per_seq, nb)
        sh = sh_ref[pl.ds(r, 1), :]
        sc = sc_ref[pl.ds(r, 1), :]
        hb = (x_ref[...] * (1.0 + sc) + sh).astype(BF16)
        h_scr[...] = hb
        pab_ref[...] = _dot(hb, wab_ref[...])

    p_ref[...] = _dot(h_scr[...], w_ref[...])


def _inproj_call(xa, mod_l, w_main, w_ab, *, seq, nb):
    t_all, d = xa.shape
    n = w_main.shape[1]
    nab = w_ab.shape[1]
    tm, tn = TM_PROJ, TN_PROJ
    kern = functools.partial(_inproj_kernel, tiles_per_seq=seq // tm, nb=nb)
    return pl.pallas_call(
        kern,
        grid=(t_all // tm, n // tn),
        in_specs=[
            pl.BlockSpec((tm, d), lambda i, j: (i, 0)),
            pl.BlockSpec((8, d), lambda i, j: (0, 0)),
            pl.BlockSpec((8, d), lambda i, j: (0, 1)),
            pl.BlockSpec((d, tn), lambda i, j: (0, j)),
            pl.BlockSpec((d, nab), lambda i, j: (0, 0)),
        ],
        out_specs=[
            pl.BlockSpec((tm, tn), lambda i, j: (i, j)),
            pl.BlockSpec((tm, nab), lambda i, j: (i, 0)),
        ],
        out_shape=[
            jax.ShapeDtypeStruct((t_all, n), F32),
            jax.ShapeDtypeStruct((t_all, nab), F32),
        ],
        scratch_shapes=[pltpu.VMEM((tm, d), BF16)],
        compiler_params=_cparams(("arbitrary", "arbitrary")),
        name="in_proj",
    )(xa, mod_l, mod_l, w_main, w_ab)


def _na_bias_index_tables(nrows):
    rq, rk = NA_QROWS, NA_KROWS
    big = 1 << 20
    cfg = [(0, 0, nrows), (8, 4, big), (nrows - rq, nrows - rk, nrows)]
    dr = np.zeros((3, rq, rk), np.int32)
    rv = np.zeros((3, rq, rk), bool)
    for v, (r0, start, nr) in enumerate(cfg):
        r = r0 + np.arange(rq)[:, None]
        kr = start + np.arange(rk)[None, :]
        rs = np.clip(r - NA_KR // 2, 0, nr - NA_KR)
        rv[v] = (kr >= rs) & (kr < rs + NA_KR)
        dr[v] = np.clip(kr - r + NA_KR - 1, 0, 2 * NA_KR - 2)
    qc = np.arange(GRID_W)[:, None]
    kc = np.arange(GRID_W)[None, :]
    ws = np.clip(qc - NA_KC // 2, 0, GRID_W - NA_KC)
    cv = (kc >= ws) & (kc < ws + NA_KC)
    dc = np.clip(kc - qc + NA_KC - 1, 0, 2 * NA_KC - 2).astype(np.int32)
    return dr, rv, dc, cv


def _na_bias(rpb_l, nrows):
    dr, rv, dc, cv = _na_bias_index_tables(nrows)
    tc = jnp.where(cv[None, None], rpb_l[:, :, dc], NEG_INF)
    blk = tc[:, dr]
    blk = jnp.where(rv[None, :, :, :, None, None], blk, NEG_INF)
    blk = blk.transpose(0, 1, 2, 4, 3, 5)
    return blk.reshape(N_HEADS, 3, NA_QROWS * GRID_W, NA_KROWS * GRID_W)


def _na_kernel(q_ref, k0, k1, k2, k3, v0, v1, v2, v3, kc_ref, vc_ref, bias_ref, o_ref):
    q = (q_ref[...] * (HEAD_DIM ** -0.5)).astype(BF16)
    kb = 4 * GRID_W
    s = []
    for j, kr in enumerate((k0, k1, k2, k3)):
        s.append(_dot_t(q, kr[...].astype(BF16)) + bias_ref[:, j * kb:(j + 1) * kb])
    s.append(_dot_t(q, kc_ref[...].astype(BF16)))
    m = s[0].max(axis=-1, keepdims=True)
    for t in s[1:]:
        m = jnp.maximum(m, t.max(axis=-1, keepdims=True))
    p = [jnp.exp(t - m) for t in s]
    l = p[0].sum(axis=-1, keepdims=True)
    for t in p[1:]:
        l = l + t.sum(axis=-1, keepdims=True)
    vs = (v0, v1, v2, v3, vc_ref)
    o = _dot(p[0].astype(BF16), vs[0][...].astype(BF16))
    for t, vr in zip(p[1:], vs[1:]):
        o = o + _dot(t.astype(BF16), vr[...].astype(BF16))
    o_ref[...] = o / l


def _na_call(p_all, bias, *, nb, seq, ctx_len):
    t_all = p_all.shape[0]
    nrows = seq // GRID_W
    nrb = nrows // NA_QROWS
    tq = NA_QROWS * GRID_W
    tk = 4 * GRID_W
    assert ctx_len == tk and nrows >= NA_KROWS
    kblocks_per_seq = seq // tk
    max_sb = kblocks_per_seq - 4
    ctx_blk0 = nb * seq // tk

    def sb(r):
        return jnp.clip(2 * r - 1, 0, max_sb)

    def kspec(j, colbase):
        return pl.BlockSpec((tk, HEAD_DIM), lambda h, b, r: (b * kblocks_per_seq + sb(r) + j, colbase + h))

    def variant(r):
        return jnp.where(r == 0, 0, jnp.where(r == nrb - 1, 2, 1))

    in_specs = [pl.BlockSpec((tq, HEAD_DIM), lambda h, b, r: (b * nrb + r, h))]
    in_specs += [kspec(j, N_HEADS) for j in range(4)]
    in_specs += [kspec(j, 2 * N_HEADS) for j in range(4)]
    in_specs += [
        pl.BlockSpec((tk, HEAD_DIM), lambda h, b, r: (ctx_blk0 + b, N_HEADS + h)),
        pl.BlockSpec((tk, HEAD_DIM), lambda h, b, r: (ctx_blk0 + b, 2 * N_HEADS + h)),
        pl.BlockSpec((None, None, tq, NA_KROWS * GRID_W), lambda h, b, r: (h, variant(r), 0, 0)),
    ]
    return pl.pallas_call(
        _na_kernel,
        grid=(N_HEADS, nb, nrb),
        in_specs=in_specs,
        out_specs=pl.BlockSpec((tq, HEAD_DIM), lambda h, b, r: (b * nrb + r, h)),
        out_shape=jax.ShapeDtypeStruct((nb * seq, GROUP_W), F32),
        compiler_params=_cparams(("arbitrary", "arbitrary", "arbitrary")),
        name="na_attention",
    )(*([p_all] * 11), bias)


def _ctx_attn_kernel(q_ref, k_ref, v_ref, o_ref):
    q = q_ref[...].astype(BF16)
    s = _dot_t(q, k_ref[...].astype(BF16)) * (HEAD_DIM ** -0.5)
    m = s.max(axis=-1, keepdims=True)
    p = jnp.exp(s - m)
    l = p.sum(axis=-1, keepdims=True)
    o_ref[...] = _dot(p.astype(BF16), v_ref[...].astype(BF16)) / l


def _ctx_attn_call(p_all, *, nb, seq, ctx_len):
    blk0 = nb * seq // ctx_len
    return pl.pallas_call(
        _ctx_attn_kernel,
        grid=(nb, N_HEADS),
        in_specs=[
            pl.BlockSpec((ctx_len, HEAD_DIM), lambda b, h: (blk0 + b, h)),
            pl.BlockSpec((ctx_len, HEAD_DIM), lambda b, h: (blk0 + b, N_HEADS + h)),
            pl.BlockSpec((ctx_len, HEAD_DIM), lambda b, h: (blk0 + b, 2 * N_HEADS + h)),
        ],
        out_specs=pl.BlockSpec((ctx_len, HEAD_DIM), lambda b, h: (b, h)),
        out_shape=jax.ShapeDtypeStruct((nb * ctx_len, GROUP_W), F32),
        compiler_params=_cparams(("arbitrary", "arbitrary")),
        name="ctx_attention",
    )(p_all, p_all, p_all)


def _rope_tables(seq, ctx_len):
    t = np.arange(seq)
    row = (t // GRID_W).astype(np.float32)
    col = (t % GRID_W).astype(np.float32)
    n_freq = HEAD_DIM // 4
    inv = (ROPE_THETA ** (-np.arange(n_freq, dtype=np.float32) / n_freq)).astype(np.float32)
    ang = jnp.stack([jnp.asarray(row)[:, None] * inv, jnp.asarray(col)[:, None] * inv], axis=1)
    cos, sin = jnp.cos(ang), jnp.sin(ang)
    c = jnp.concatenate([cos, cos], axis=-1).reshape(seq, HEAD_DIM)
    s = jnp.concatenate([-sin, sin], axis=-1).reshape(seq, HEAD_DIM)
    c = jnp.concatenate([c, jnp.ones((ctx_len, HEAD_DIM), F32)], axis=0)
    s = jnp.concatenate([s, jnp.zeros((ctx_len, HEAD_DIM), F32)], axis=0)
    return c, s


def _gdn_prep_kernel(cur_ref, prev_ref, next_ref, pab_ref, cw_ref, alog_ref, dtb_ref, rc_ref, rs_ref,
                     q_ref, k_ref, v_ref, g_ref, b_ref, ext_ref, *, n_lat_tiles, tps, nab_half):
    i = pl.program_id(0)
    tm = TM_PREP
    is_ctx = i >= n_lat_tiles
    first = jnp.logical_or(is_ctx, i % tps == 0)
    last = jnp.logical_or(is_ctx, i % tps == tps - 1)
    ext_ref[8:8 + tm, :] = cur_ref[...]
    ext_ref[0:8, :] = jnp.where(first, 0.0, prev_ref[...])
    ext_ref[8 + tm:16 + tm, :] = jnp.where(last, 0.0, next_ref[...])

    lane = lax.broadcasted_iota(jnp.int32, (tm, HEAD_DIM), 1)
    half0 = (lane % (HEAD_DIM // 2)) < (HEAD_DIM // 4)
    rc = rc_ref[...]
    rs = rs_ref[...]
    base = 8 - GDN_CONV // 2
    outs = (q_ref, k_ref, v_ref)
    for part in range(3):
        for h in range(N_HEADS):
            c0 = part * GROUP_W + h * HEAD_DIM
            acc = cw_ref[0:1, c0:c0 + HEAD_DIM] * ext_ref[base:base + tm, c0:c0 + HEAD_DIM]
            for t in range(1, GDN_CONV):
                acc = acc + cw_ref[t:t + 1, c0:c0 + HEAD_DIM] * ext_ref[base + t:base + t + tm, c0:c0 + HEAD_DIM]
            y = _silu(acc)
            if part < 2:
                y = y * lax.rsqrt(jnp.sum(y * y, axis=-1, keepdims=True) + 1e-6)
                partner = jnp.where(half0, pltpu.roll(y, HEAD_DIM - HEAD_DIM // 4, 1),
                                    pltpu.roll(y, HEAD_DIM // 4, 1))
                y = y * rc + partner * rs
                if part == 0:
                    y = y * (HEAD_DIM ** -0.5)
            outs[part][:, h * HEAD_DIM:(h + 1) * HEAD_DIM] = y

    a = pab_ref[:, 0:nab_half] + dtb_ref[...]
    softplus = jnp.maximum(a, 0.0) + jnp.log1p(jnp.exp(-jnp.abs(a)))
    g_ref[...] = -jnp.exp(alog_ref[...]) * softplus
    b_ref[...] = _sigmoid(pab_ref[:, nab_half:2 * nab_half])


def _gdn_prep_call(p_all, pab, conv_w, alog, dtb, rope_c, rope_s, *, nb, seq, ctx_len):
    t_all = p_all.shape[0]
    tm = TM_PREP
    assert ctx_len == tm
    n_tiles = t_all // tm
    n_lat_tiles = nb * seq // tm
    tps = seq // tm
    nab_half = pab.shape[1] // 2
    qkv_w = 3 * GROUP_W
    last8 = t_all // 8 - 1
    kern = functools.partial(_gdn_prep_kernel, n_lat_tiles=n_lat_tiles, tps=tps, nab_half=nab_half)

    def rope_idx(i):
        return jnp.where(i < n_lat_tiles, i % tps, tps)

    return pl.pallas_call(
        kern,
        grid=(n_tiles,),
        in_specs=[
            pl.BlockSpec((tm, qkv_w), lambda i: (i, 1)),
            pl.BlockSpec((8, qkv_w), lambda i: (jnp.maximum(i * (tm // 8) - 1, 0), 1)),
            pl.BlockSpec((8, qkv_w), lambda i: (jnp.minimum((i + 1) * (tm // 8), last8), 1)),
            pl.BlockSpec((tm, 2 * nab_half), lambda i: (i, 0)),
            pl.BlockSpec((8, qkv_w), lambda i: (0, 0)),
            pl.BlockSpec((1, nab_half), lambda i: (0, 0)),
            pl.BlockSpec((1, nab_half), lambda i: (0, 0)),
            pl.BlockSpec((tm, HEAD_DIM), lambda i: (rope_idx(i), 0)),
            pl.BlockSpec((tm, HEAD_DIM), lambda i: (rope_idx(i), 0)),
        ],
        out_specs=[
            pl.BlockSpec((tm, GROUP_W), lambda i: (i, 0)),
            pl.BlockSpec((tm, GROUP_W), lambda i: (i, 0)),
            pl.BlockSpec((tm, GROUP_W), lambda i: (i, 0)),
            pl.BlockSpec((tm, nab_half), lambda i: (i, 0)),
            pl.BlockSpec((tm, nab_half), lambda i: (i, 0)),
        ],
        out_shape=[
            jax.ShapeDtypeStruct((t_all, GROUP_W), F32),
            jax.ShapeDtypeStruct((t_all, GROUP_W), F32),
            jax.ShapeDtypeStruct((t_all, GROUP_W), F32),
            jax.ShapeDtypeStruct((t_all, nab_half), F32),
            jax.ShapeDtypeStruct((t_all, nab_half), F32),
        ],
        scratch_shapes=[pltpu.VMEM((tm + 16, qkv_w), F32)],
        compiler_params=_cparams(("arbitrary",)),
        name="gdn_prep",
    )(p_all, p_all, p_all, pab, conv_w, alog, dtb, rope_c, rope_s)


def _unit_tri_inverse(ns, orders):
    c = GDN_CHUNK
    ds = []
    for n, (row, col) in zip(ns, orders):
        m1 = jnp.logical_and(jnp.logical_and((row >> 1) == (col >> 1), (row & 1) == 1), (col & 1) == 0)
        ds.append((row == col).astype(F32) - jnp.where(m1, n, 0.0))
    k = 2
    while k < c:
        sh = int(np.log2(2 * k))
        dsp, xs = [], []
        for n, d, (row, col) in zip(ns, ds, orders):
            mk = jnp.logical_and((row >> sh) == (col >> sh),
                                 jnp.logical_and((row & (2 * k - 1)) >= k, (col & (2 * k - 1)) < k))
            lh, ll = _split_bf16(jnp.where(mk, n, 0.0))
            dh, dl = _split_bf16(d)
            dsp.append((dh, dl))
            xs.append(_dot3_split(lh, ll, dh, dl))
        ds = [d - _dot3_split(dh, dl, *_split_bf16(x)) for d, (dh, dl), x in zip(ds, dsp, xs)]
        k *= 2
    return ds


def _gdn_scan_kernel(qf, kf, vf, gf, bf, qb, kb, vb, gb, bb, of_ref, ob_ref, st_ref, *, hb):
    s = pl.program_id(2)
    c = GDN_CHUNK

    @pl.when(s == 0)
    def _():
        st_ref[...] = jnp.zeros(st_ref.shape, F32)

    ri = lax.broadcasted_iota(jnp.int32, (c, c), 0)
    ci = lax.broadcasted_iota(jnp.int32, (c, c), 1)
    zpad = jnp.zeros((HEAD_DIM - c, HEAD_DIM), F32)
    ch = []
    for d, (q_ref, k_ref, v_ref, g_ref, be_ref, o_ref) in enumerate(
            ((qf, kf, vf, gf, bf, of_ref), (qb, kb, vb, gb, bb, ob_ref))):
        incl = (ri >= ci) if d == 0 else (ri <= ci)
        gc_all = jnp.dot(incl.astype(F32), g_ref[...], precision=HIGHEST, preferred_element_type=F32)
        gc_t = jnp.transpose(jnp.concatenate([gc_all, zpad], axis=0))
        beta_all = be_ref[...]
        for hh in range(hb):
            ln = d * hb + hh
            sl = slice(hh * HEAD_DIM, (hh + 1) * HEAD_DIM)
            ch.append(dict(
                ln=ln, sl=sl, o_ref=o_ref, incl=incl,
                strict=(ri > ci) if d == 0 else (ri < ci),
                order=(ri, ci) if d == 0 else (ci, ri),
                last=c - 1 if d == 0 else 0,
                gcc=gc_all[:, ln:ln + 1], grow=gc_t[ln:ln + 1, 0:c], beta=beta_all[:, ln:ln + 1],
                q=q_ref[:, sl], k=k_ref[:, sl], v=v_ref[:, sl]))
    for t in ch:
        t["decay"] = jnp.where(t["incl"], jnp.exp(jnp.where(t["incl"], t["gcc"] - t["grow"], 0.0)), 0.0)
        t["kbeta"] = t["k"] * t["beta"]
    for t in ch:
        t["kk"] = _dot_t(t["kbeta"], t["k"])
    for t in ch:
        t["qk"] = _dot_t(t["q"], t["k"]) * t["decay"]
    ns = [jnp.where(t["strict"], t["kk"] * t["decay"], 0.0) for t in ch]
    tinvs = _unit_tri_inverse(ns, [t["order"] for t in ch])
    for t, tinv in zip(ch, tinvs):
        eg = jnp.exp(t["gcc"])
        t["eg"] = eg
        rhs = jnp.concatenate([t["v"] * t["beta"], t["kbeta"] * eg], axis=1)
        t["sol"] = _dot3_split(*_split_bf16(tinv), *_split_bf16(rhs))
    for t in ch:
        t["state"] = st_ref[t["ln"]]
        t["v_new"] = t["sol"][:, :HEAD_DIM] - _dot(t["sol"][:, HEAD_DIM:], t["state"])
    for t in ch:
        t["o_ref"][:, t["sl"]] = _dot(t["q"] * t["eg"], t["state"]) + _dot(t["qk"], t["v_new"])
    for t in ch:
        glast = t["gcc"][t["last"]:t["last"] + 1, :]
        kd_t = jnp.transpose(jnp.concatenate([t["k"] * jnp.exp(glast - t["gcc"]), zpad], axis=0))[:, 0:c]
        st_ref[t["ln"]] = t["state"] * jnp.exp(glast) + _dot(kd_t, t["v_new"])


def _gdn_scan_call(qn, kn, vn, g, beta, *, nb, seq, ctx_len):
    t_all = qn.shape[0]
    hb = GDN_HB
    c = GDN_CHUNK
    ncc = ctx_len // c
    ncl = seq // c
    ctx0 = nb * ncl
    n_hg = N_HEADS // hb

    def row_f(b, s):
        return jnp.where(s < ncc, ctx0 + b * ncc + s, b * ncl + (s - ncc))

    def row_b(b, s):
        return jnp.where(s < ncc, ctx0 + b * ncc + (ncc - 1 - s), b * ncl + (ncl - 1 - (s - ncc)))

    def specs(rowfn):
        big = pl.BlockSpec((c, hb * HEAD_DIM), lambda b, hg, s: (rowfn(b, s), hg))
        small = pl.BlockSpec((c, HEAD_DIM), lambda b, hg, s: (rowfn(b, s), hg))
        return [big, big, big, small, small]

    out_f = pl.BlockSpec((c, hb * HEAD_DIM), lambda b, hg, s: (row_f(b, s), hg))
    out_b = pl.BlockSpec((c, hb * HEAD_DIM), lambda b, hg, s: (row_b(b, s), hg))
    return pl.pallas_call(
        functools.partial(_gdn_scan_kernel, hb=hb),
        grid=(nb, n_hg, ncc + ncl),
        in_specs=specs(row_f) + specs(row_b),
        out_specs=[out_f, out_b],
        out_shape=[jax.ShapeDtypeStruct((t_all, GROUP_W), F32)] * 2,
        scratch_shapes=[pltpu.VMEM((2 * hb, HEAD_DIM, HEAD_DIM), F32)],
        compiler_params=_cparams(("arbitrary", "arbitrary", "arbitrary")),
        name="gdn_scan",
    )(qn, kn, vn, g, beta, qn, kn, vn, g, beta)


def _outproj_kernel(nal_ref, nac_ref, of_ref, ob_ref, z_ref, x_ref, ga_ref, shf_ref, scf_ref, w_ref, ng_ref, lg_ref,
                    lb_ref, x1_ref, h2t_ref, gdn_scr, *, tiles_per_seq, nb):
    i = pl.program_id(0)
    r = jnp.minimum(i // tiles_per_seq, nb)
    na = jnp.where(i < nb * tiles_per_seq, nal_ref[...], nac_ref[...])
    o = of_ref[...] + ob_ref[...]
    ng = ng_ref[...]
    for h in range(N_HEADS):
        sl = slice(h * HEAD_DIM, (h + 1) * HEAD_DIM)
        oh = o[:, sl]
        oh = oh * lax.rsqrt(jnp.mean(oh * oh, axis=-1, keepdims=True) + 1e-6) * ng
        gdn_scr[:, sl] = (oh * _silu(z_ref[:, sl])).astype(BF16)
    mix = _dot(na.astype(BF16), w_ref[0:GROUP_W, :]) + _dot(gdn_scr[...], w_ref[GROUP_W:2 * GROUP_W, :])
    y = DEEPNORM_ALPHA * x_ref[...] + ga_ref[pl.ds(r, 1), :] * mix
    x1 = _layer_norm_rows(y, lg_ref[...], lb_ref[...])
    x1_ref[...] = x1
    h2 = x1 * (1.0 + scf_ref[pl.ds(r, 1), :]) + shf_ref[pl.ds(r, 1), :]
    h2t_ref[...] = jnp.transpose(h2).astype(BF16)


def _outproj_call(na_lat, na_ctx, o_f, o_b, p_all, xa, mod_l, w_out, norm_g, ln_g, ln_b, *, seq, nb, rows):
    t_all, d = rows, xa.shape[1]
    tm = TM_OUT
    n_lat_tiles = nb * seq // tm
    kern = functools.partial(_outproj_kernel, tiles_per_seq=seq // tm, nb=nb)
    row = lambda i: (i, 0)
    const = lambda i: (0, 0)
    return pl.pallas_call(
        kern,
        grid=(t_all // tm,),
        in_specs=[
            pl.BlockSpec((tm, GROUP_W), lambda i: (jnp.minimum(i, n_lat_tiles - 1), 0)),
            pl.BlockSpec((tm, GROUP_W), lambda i: (jnp.maximum(i - n_lat_tiles, 0), 0)),
            pl.BlockSpec((tm, GROUP_W), row),
            pl.BlockSpec((tm, GROUP_W), row),
            pl.BlockSpec((tm, GROUP_W), lambda i: (i, 6)),
            pl.BlockSpec((tm, d), row),
            pl.BlockSpec((8, d), lambda i: (0, 2)),
            pl.BlockSpec((8, d), lambda i: (0, 3)),
            pl.BlockSpec((8, d), lambda i: (0, 4)),
            pl.BlockSpec((d, d), const),
            pl.BlockSpec((1, HEAD_DIM), const),
            pl.BlockSpec((1, d), const),
            pl.BlockSpec((1, d), const),
        ],
        out_specs=[
            pl.BlockSpec((tm, d), row),
            pl.BlockSpec((d, tm), lambda i: (0, i)),
        ],
        out_shape=[
            jax.ShapeDtypeStruct((t_all, d), F32),
            jax.ShapeDtypeStruct((d, t_all), BF16),
        ],
        scratch_shapes=[pltpu.VMEM((tm, GROUP_W), BF16)],
        compiler_params=_cparams(("arbitrary",)),
        name="out_proj_ln",
    )(na_lat, na_ctx, o_f, o_b, p_all, xa, mod_l, mod_l, mod_l, w_out, norm_g, ln_g, ln_b)


def _peer_score_kernel(ht_ref, w_ref, sk_ref, a_ref, b_ref, st_ref, s_scr, atop, btop, cand):
    tm = TM_PEER
    qt = _dot(w_ref[...], ht_ref[...])
    half = PEER_QDIM // 2
    s_scr[0] = _dot(sk_ref[0], qt[0:half])
    s_scr[1] = _dot(sk_ref[1], qt[half:2 * half])
    a_ref[0] = s_scr[0]
    b_ref[0] = s_scr[1]
    n_half = PEER_TOPK // 2
    for tc in range(tm // 128):
        ls = slice(tc * 128, (tc + 1) * 128)
        for p, out_ref in enumerate((atop, btop)):
            sc = s_scr[p, :, ls]
            for k in range(PEER_TOPK):
                m = jnp.max(sc, axis=0, keepdims=True)
                out_ref[k:k + 1, ls] = m
                sc = jnp.where(sc == m, -jnp.inf, sc)
        bt = btop[:, ls]
        for i in range(n_half):
            cand[i * PEER_TOPK:(i + 1) * PEER_TOPK, ls] = atop[i:i + 1, ls] + bt
        cand[n_half * PEER_TOPK:n_half * PEER_TOPK + n_half, ls] = atop[n_half:PEER_TOPK, ls] + bt[0:1, :]
        cv = cand[:, ls]
        m0 = jnp.max(cv, axis=0, keepdims=True)
        z = jnp.zeros_like(m0)
        m = m0
        for k in range(PEER_TOPK):
            m = jnp.max(cv, axis=0, keepdims=True)
            z = z + jnp.exp(m - m0)
            cv = jnp.where(cv == m, -jnp.inf, cv)
        st_ref[0, 0:1, ls] = m
        st_ref[0, 1:2, ls] = atop[0:1, ls]
        st_ref[0, 2:3, ls] = bt[0:1, :]
        st_ref[0, 3:4, ls] = 1.0 / z
    st_ref[0, 4:8, :] = jnp.zeros((4, tm), F32)


def _peer_score_call(h2t, wq_t, subkeys):
    d, t_all = h2t.shape
    tm = TM_PEER
    ncand = PEER_TOPK * PEER_TOPK // 2 + PEER_TOPK // 2
    tok = lambda i, h: (h, 0, i)
    return pl.pallas_call(
        _peer_score_kernel,
        grid=(t_all // tm, PEER_HEADS),
        in_specs=[
            pl.BlockSpec((d, tm), lambda i, h: (0, i)),
            pl.BlockSpec((PEER_QDIM, d), lambda i, h: (h, 0)),
            pl.BlockSpec((2, PEER_NKEYS, PEER_QDIM // 2), lambda i, h: (0, 0, 0)),
        ],
        out_specs=[
            pl.BlockSpec((1, PEER_NKEYS, tm), tok),
            pl.BlockSpec((1, PEER_NKEYS, tm), tok),
            pl.BlockSpec((1, 8, tm), tok),
        ],
        out_shape=[
            jax.ShapeDtypeStruct((PEER_HEADS, PEER_NKEYS, t_all), F32),
            jax.ShapeDtypeStruct((PEER_HEADS, PEER_NKEYS, t_all), F32),
            jax.ShapeDtypeStruct((PEER_HEADS, 8, t_all), F32),
        ],
        scratch_shapes=[
            pltpu.VMEM((2, PEER_NKEYS, tm), F32),
            pltpu.VMEM((PEER_TOPK, tm), F32),
            pltpu.VMEM((PEER_TOPK, tm), F32),
            pltpu.VMEM((ncand, tm), F32),
        ],
        compiler_params=_cparams(("arbitrary", "arbitrary")),
        name="peer_scores",
    )(h2t, wq_t, subkeys)


def _peer_dense_kernel(h_ref, u_ref, vt_ref, a_ref, b_ref, st_ref, o_ref, eb_ref, act0, act1, w0, w1):
    m = pl.program_id(1)
    n_tiles = pl.num_programs(1) - 2
    tm = TM_PEER
    nk = PEER_NKEYS
    rows_per_step = TE_PEER // nk
    jq_rows = 16
    n_jq = nk // jq_rows

    @pl.when(m == 0)
    def _():
        o_ref[...] = jnp.zeros(o_ref.shape, F32)
        for r in (act0, act1):
            r[...] = jnp.zeros(r.shape, F32)
        for r in (w0, w1):
            r[...] = jnp.zeros(r.shape, BF16)
        for h in range(PEER_HEADS):
            eb_ref[h] = jnp.exp(b_ref[h] - st_ref[h, 2:3, :])

    valid_b = jnp.logical_and(m >= 1, m <= n_tiles)
    sqrt_half = np.float32(np.sqrt(0.5))

    def stages(act_a, w_c, act_b, w_b):
        n_tc = tm // 128

        def stage_a(n, r):
            cs = slice(n * 256, (n + 1) * 256)
            rs = slice(r * 128, (r + 1) * 128)
            act_a[rs, cs] = _dot(u_ref[rs, :], h_ref[:, cs])

        def stage_c(q):
            rs = slice(q * 128, (q + 1) * 128)
            o_ref[rs, :] += _dot(vt_ref[rs, :], w_c[...])

        def stage_b(tc, jq):
            ls = slice(tc * 128, (tc + 1) * 128)
            js = slice(jq * (jq_rows // 8), (jq + 1) * (jq_rows // 8))
            g = [jnp.zeros((jq_rows // 8, 8, 128), F32) for _ in range(rows_per_step)]
            for h in range(PEER_HEADS):
                ar = a_ref[h, :, ls]
                er = jnp.exp(ar - st_ref[h, 1:2, ls]) * st_ref[h, 3:4, ls]
                tau = jnp.broadcast_to(st_ref[h, 0:1, ls], (8, 128))
                bb = b_ref[h, js, :, ls]
                ee = eb_ref[h, js, :, ls]
                for ii in range(rows_per_step):
                    a_b = jnp.broadcast_to(ar[ii:ii + 1, :], (8, 128))
                    e_b = jnp.broadcast_to(er[ii:ii + 1, :], (8, 128))
                    g[ii] = g[ii] + jnp.where(bb + a_b >= tau, ee * e_b, 0.0)
            for ii in range(rows_per_step):
                r0 = ii * nk + jq * jq_rows
                xa = act_b[r0:r0 + jq_rows, ls]
                ge = 0.5 * xa * (1.0 + lax.erf(xa * sqrt_half))
                wv = jnp.where(valid_b, ge * g[ii].reshape(jq_rows, 128), 0.0)
                w_b[r0:r0 + jq_rows, ls] = wv.astype(BF16)

        mxu = []
        for q in range(8):
            mxu += [functools.partial(stage_a, q // 4, q % 4), functools.partial(stage_c, 2 * q),
                    functools.partial(stage_c, 2 * q + 1)]
        blocks = [(tc, jq) for tc in range(n_tc) for jq in range(n_jq)]
        mi = 0
        for bi, (tc, jq) in enumerate(blocks):
            while mi < len(mxu) and mi * len(blocks) <= bi * len(mxu):
                mxu[mi]()
                mi += 1
            stage_b(tc, jq)
        while mi < len(mxu):
            mxu[mi]()
            mi += 1

    @pl.when(m % 2 == 0)
    def _():
        stages(act0, w0, act1, w1)

    @pl.when(m % 2 == 1)
    def _():
        stages(act1, w1, act0, w0)


def _peer_dense_call(hb, u_bf, vt_bf, a_t, b_t, stats):
    d, t_all = hb.shape
    ne = u_bf.shape[0]
    tm, te = TM_PEER, TE_PEER
    n_tiles = ne // te
    rows = te // PEER_NKEYS
    tok3 = lambda i, m: (0, 0, i)
    a4 = a_t.reshape(PEER_HEADS, n_tiles, rows, t_all)
    b4 = b_t.reshape(PEER_HEADS, PEER_NKEYS // 8, 8, t_all)
    return pl.pallas_call(
        _peer_dense_kernel,
        grid=(t_all // tm, n_tiles + 2),
        in_specs=[
            pl.BlockSpec((d, tm), lambda i, m: (0, i)),
            pl.BlockSpec((te, d), lambda i, m: (jnp.minimum(m, n_tiles - 1), 0)),
            pl.BlockSpec((d, te), lambda i, m: (0, jnp.maximum(m - 2, 0))),
            pl.BlockSpec((PEER_HEADS, None, rows, tm), lambda i, m: (0, jnp.clip(m - 1, 0, n_tiles - 1), 0, i)),
            pl.BlockSpec((PEER_HEADS, PEER_NKEYS // 8, 8, tm), lambda i, m: (0, 0, 0, i)),
            pl.BlockSpec((PEER_HEADS, 8, tm), tok3),
        ],
        out_specs=pl.BlockSpec((d, tm), lambda i, m: (0, i)),
        out_shape=jax.ShapeDtypeStruct((d, t_all), F32),
        scratch_shapes=[
            pltpu.VMEM((PEER_HEADS, PEER_NKEYS // 8, 8, tm), F32),
            pltpu.VMEM((te, tm), F32),
            pltpu.VMEM((te, tm), F32),
            pltpu.VMEM((te, tm), BF16),
            pltpu.VMEM((te, tm), BF16),
        ],
        compiler_params=_cparams(("arbitrary", "arbitrary")),
        name="peer_dense",
    )(hb, u_bf, vt_bf, a4, b4, stats)


def _peer_out_kernel(ft_ref, x1_ref, gf_ref, lg_ref, lb_ref, o_ref, *, tiles_per_seq, nb):
    i = pl.program_id(0)
    r = jnp.minimum(i // tiles_per_seq, nb)
    y = DEEPNORM_ALPHA * x1_ref[...] + gf_ref[pl.ds(r, 1), :] * jnp.transpose(ft_ref[...])
    o_ref[...] = _layer_norm_rows(y, lg_ref[...], lb_ref[...])


def _peer_out_call(ffn_t, x1, mod_l, ln_g, ln_b, *, seq, nb):
    t_all, d = x1.shape
    tm = TM_OUT
    kern = functools.partial(_peer_out_kernel, tiles_per_seq=seq // tm, nb=nb)
    return pl.pallas_call(
        kern,
        grid=(t_all // tm,),
        in_specs=[
            pl.BlockSpec((d, tm), lambda i: (0, i)),
            pl.BlockSpec((tm, d), lambda i: (i, 0)),
            pl.BlockSpec((8, d), lambda i: (0, 5)),
            pl.BlockSpec((1, d), lambda i: (0, 0)),
            pl.BlockSpec((1, d), lambda i: (0, 0)),
        ],
        out_specs=pl.BlockSpec((tm, d), lambda i: (i, 0)),
        out_shape=jax.ShapeDtypeStruct((t_all, d), F32),
        compiler_params=_cparams(("arbitrary",)),
        name="peer_out_ln",
    )(ffn_t, x1, mod_l, ln_g, ln_b)


def _ab_columns(w_tail):
    depth, d, _ = w_tail.shape
    hb = GDN_HB
    n_hg = N_HEADS // hb
    wa = w_tail[:, :, :2 * N_HEADS].reshape(depth, d, 2, n_hg, hb)
    wb = w_tail[:, :, 2 * N_HEADS:].reshape(depth, d, 2, n_hg, hb)

    def lay(w):
        w = w.transpose(0, 1, 3, 2, 4).reshape(depth, d, n_hg, 2 * hb)
        w = jnp.pad(w, ((0, 0), (0, 0), (0, 0), (0, HEAD_DIM - 2 * hb)))
        return w.reshape(depth, d, n_hg * HEAD_DIM)

    return jnp.concatenate([lay(wa), lay(wb)], axis=-1)


def _head_param_lanes(p):
    depth = p.shape[0]
    hb = GDN_HB
    n_hg = N_HEADS // hb
    p = p.reshape(depth, 2, n_hg, hb).transpose(0, 2, 1, 3).reshape(depth, n_hg, 2 * hb)
    p = jnp.pad(p, ((0, 0), (0, 0), (0, HEAD_DIM - 2 * hb)))
    return p.reshape(depth, 1, n_hg * HEAD_DIM)


def kernel(x, c, ctx, c_ctx, w_mod, b_mod, w_in, w_out, na_rpb, gdn_conv, gdn_a_log, gdn_dt_bias, gdn_norm_g,
           peer_wq, peer_subkeys, peer_u, peer_v, ln_g, ln_b):
    nb, seq, d = x.shape
    ctx_len = ctx.shape[1]
    depth = w_mod.shape[0]
    assert d == D_MODEL and nb + 1 <= 8
    assert seq % (NA_QROWS * GRID_W) == 0 and seq % TM_PROJ == 0
    t_lat = nb * seq
    dims = dict(nb=nb, seq=seq, ctx_len=ctx_len)

    xa = jnp.concatenate([x.reshape(t_lat, d), ctx.reshape(nb * ctx_len, d)], axis=0)
    cs = jnp.concatenate([c, c_ctx[None, :], jnp.zeros((8 - nb - 1, d), F32)], axis=0)
    mod = _mod_call(cs, w_mod, b_mod)

    w_main = w_in[:, :, :P_MAIN_W].astype(BF16)
    w_ab = _ab_columns(w_in[:, :, P_MAIN_W:]).astype(BF16)
    w_out_bf = w_out.astype(BF16)
    alog = _head_param_lanes(gdn_a_log)
    dtb = _head_param_lanes(gdn_dt_bias)
    conv_w = jnp.pad(gdn_conv, ((0, 0), (0, 8 - GDN_CONV), (0, 0)))
    rope_c, rope_s = _rope_tables(seq, ctx_len)
    wq_t = peer_wq.astype(BF16).transpose(0, 2, 1)
    u_bf = peer_u.astype(BF16)
    vt_bf = peer_v.astype(BF16).transpose(0, 2, 1)

    for l in range(depth):
        p_all, pab = _inproj_call(xa, mod[l], w_main[l], w_ab[l], seq=seq, nb=nb)
        bias = _na_bias(na_rpb[l], seq // GRID_W)
        na_lat = _na_call(p_all, bias, **dims)
        na_ctx = _ctx_attn_call(p_all, **dims)
        qn, kn, vn, g, beta = _gdn_prep_call(p_all, pab, conv_w[l], alog[l], dtb[l], rope_c, rope_s, **dims)
        o_f, o_b = _gdn_scan_call(qn, kn, vn, g, beta, **dims)
        x1, h2t = _outproj_call(na_lat, na_ctx, o_f, o_b, p_all, xa, mod[l], w_out_bf[l], gdn_norm_g[l][None, :],
                                ln_g[l, 0][None, :], ln_b[l, 0][None, :], seq=seq, nb=nb,
                                rows=xa.shape[0] if l < depth - 1 else t_lat)
        a_t, b_t, stats = _peer_score_call(h2t, wq_t[l], peer_subkeys[l])
        ffn_t = _peer_dense_call(h2t, u_bf[l], vt_bf[l], a_t, b_t, stats)
        xa = _peer_out_call(ffn_t, x1, mod[l], ln_g[l, 1][None, :], ln_b[l, 1][None, :], seq=seq, nb=nb)
    return xa.reshape(nb, seq, d)
```

```python
import functools

import numpy as np
import jax
import jax.numpy as jnp
from jax import lax
from jax.experimental import pallas as pl
from jax.experimental.pallas import tpu as pltpu

F32 = jnp.float32
BF16 = jnp.bfloat16
HIGHEST = lax.Precision.HIGHEST

D_MODEL = 2048
HEAD_DIM = 128
N_HEADS = 8
GROUP_W = N_HEADS * HEAD_DIM
GRID_W = 64
NA_KR = 8
NA_KC = 16
NA_QROWS = 8
NA_KROWS = 16
GDN_CHUNK = 64
GDN_CONV = 5
GDN_HB = 8
ROPE_THETA = 10000.0
PEER_HEADS = 8
PEER_TOPK = 16
PEER_NKEYS = 128
PEER_QDIM = 256
N_MOD = 6
LN_EPS = 1e-6
NEG_INF = -1e30
DEPTH_FOR_DEEPNORM = 4
DEEPNORM_ALPHA = (2 * DEPTH_FOR_DEEPNORM) ** 0.25
P_MAIN_W = 7 * GROUP_W
VMEM_LIMIT = 56 * 1024 * 1024

TM_PROJ = 512
TN_PROJ = 1024
TM_OUT = 256
TM_PREP = 256
TM_PEER = 512
TE_PEER = 512


def _cparams(sem):
    return pltpu.CompilerParams(dimension_semantics=sem, vmem_limit_bytes=VMEM_LIMIT)


def _sigmoid(x):
    return 1.0 / (1.0 + jnp.exp(-x))


def _silu(x):
    return x * _sigmoid(x)


def _dot(a, b):
    return jnp.dot(a, b, preferred_element_type=F32)


def _dot_t(a, b):
    return lax.dot_general(a, b, (((1,), (1,)), ((), ())), preferred_element_type=F32)


def _split_bf16(a):
    hi = a.astype(BF16)
    lo = (a - hi.astype(F32)).astype(BF16)
    return hi, lo


def _dot3_split(ah, al, bh, bl):
    return _dot(ah, bh) + (_dot(ah, bl) + _dot(al, bh))


def _layer_norm_rows(y, g, b):
    mu = jnp.mean(y, axis=-1, keepdims=True)
    yc = y - mu
    var = jnp.mean(yc * yc, axis=-1, keepdims=True)
    return yc * lax.rsqrt(var + LN_EPS) * g + b


def _mod_kernel(c_ref, w_ref, b_ref, o_ref):
    s = _silu(c_ref[...])
    o_ref[0] = jnp.dot(s, w_ref[0], precision=HIGHEST, preferred_element_type=F32) + b_ref[0]


def _mod_call(cs, w_mod, b_mod):
    depth, d, n = w_mod.shape
    tn = 1024
    return pl.pallas_call(
        _mod_kernel,
        grid=(depth, n // tn),
        in_specs=[
            pl.BlockSpec((8, d), lambda l, j: (0, 0)),
            pl.BlockSpec((1, d, tn), lambda l, j: (l, 0, j)),
            pl.BlockSpec((1, 1, tn), lambda l, j: (l, 0, j)),
        ],
        out_specs=pl.BlockSpec((1, 8, tn), lambda l, j: (l, 0, j)),
        out_shape=jax.ShapeDtypeStruct((depth, 8, n), F32),
        compiler_params=_cparams(("arbitrary", "arbitrary")),
        name="adaln_mod",
    )(cs, w_mod, b_mod.reshape(depth, 1, n))


def _inproj_kernel(x_ref, sh_ref, sc_ref, w_ref, wab_ref, p_ref, pab_ref, h_scr, *, tiles_per_seq, nb):
    i = pl.program_id(0)
    j = pl.program_id(1)

    @pl.when(j == 0)
    def _():
        r = jnp.minimum(i // tiles_per_seq, nb)
        sh = sh_ref[pl.ds(r, 1), :]
        sc = sc_ref[pl.ds(r, 1), :]
        hb = (x_ref[...] * (1.0 + sc) + sh).astype(BF16)
        h_scr[...] = hb
        pab_ref[...] = _dot(hb, wab_ref[...])

    p_ref[...] = _dot(h_scr[...], w_ref[...])


def _inproj_call(xa, mod_l, w_main, w_ab, *, seq, nb):
    t_all, d = xa.shape
    n = w_main.shape[1]
    nab = w_ab.shape[1]
    tm, tn = TM_PROJ, TN_PROJ
    kern = functools.partial(_inproj_kernel, tiles_per_seq=seq // tm, nb=nb)
    return pl.pallas_call(
        kern,
        grid=(t_all // tm, n // tn),
        in_specs=[
            pl.BlockSpec((tm, d), lambda i, j: (i, 0)),
            pl.BlockSpec((8, d), lambda i, j: (0, 0)),
            pl.BlockSpec((8, d), lambda i, j: (0, 1)),
            pl.BlockSpec((d, tn), lambda i, j: (0, j)),
            pl.BlockSpec((d, nab), lambda i, j: (0, 0)),
        ],
        out_specs=[
            pl.BlockSpec((tm, tn), lambda i, j: (i, j)),
            pl.BlockSpec((tm, nab), lambda i, j: (i, 0)),
        ],
        out_shape=[
            jax.ShapeDtypeStruct((t_all, n), F32),
            jax.ShapeDtypeStruct((t_all, nab), F32),
        ],
        scratch_shapes=[pltpu.VMEM((tm, d), BF16)],
        compiler_params=_cparams(("arbitrary", "arbitrary")),
        name="in_proj",
    )(xa, mod_l, mod_l, w_main, w_ab)


def _na_bias_index_tables(nrows):
    rq, rk = NA_QROWS, NA_KROWS
    big = 1 << 20
    cfg = [(0, 0, nrows), (8, 4, big), (nrows - rq, nrows - rk, nrows)]
    dr = np.zeros((3, rq, rk), np.int32)
    rv = np.zeros((3, rq, rk), bool)
    for v, (r0, start, nr) in enumerate(cfg):
        r = r0 + np.arange(rq)[:, None]
        kr = start + np.arange(rk)[None, :]
        rs = np.clip(r - NA_KR // 2, 0, nr - NA_KR)
        rv[v] = (kr >= rs) & (kr < rs + NA_KR)
        dr[v] = np.clip(kr - r + NA_KR - 1, 0, 2 * NA_KR - 2)
    qc = np.arange(GRID_W)[:, None]
    kc = np.arange(GRID_W)[None, :]
    ws = np.clip(qc - NA_KC // 2, 0, GRID_W - NA_KC)
    cv = (kc >= ws) & (kc < ws + NA_KC)
    dc = np.clip(kc - qc + NA_KC - 1, 0, 2 * NA_KC - 2).astype(np.int32)
    return dr, rv, dc, cv


def _na_bias(rpb_l, nrows):
    dr, rv, dc, cv = _na_bias_index_tables(nrows)
    tc = jnp.where(cv[None, None], rpb_l[:, :, dc], NEG_INF)
    blk = tc[:, dr]
    blk = jnp.where(rv[None, :, :, :, None, None], blk, NEG_INF)
    blk = blk.transpose(0, 1, 2, 4, 3, 5)
    return blk.reshape(N_HEADS, 3, NA_QROWS * GRID_W, NA_KROWS * GRID_W)


def _na_kernel(q_ref, k0, k1, k2, k3, v0, v1, v2, v3, kc_ref, vc_ref, bias_ref, o_ref):
    q = (q_ref[...] * (HEAD_DIM ** -0.5)).astype(BF16)
    kb = 4 * GRID_W
    s = []
    for j, kr in enumerate((k0, k1, k2, k3)):
        s.append(_dot_t(q, kr[...].astype(BF16)) + bias_ref[:, j * kb:(j + 1) * kb])
    s.append(_dot_t(q, kc_ref[...].astype(BF16)))
    m = s[0].max(axis=-1, keepdims=True)
    for t in s[1:]:
        m = jnp.maximum(m, t.max(axis=-1, keepdims=True))
    p = [jnp.exp(t - m) for t in s]
    l = p[0].sum(axis=-1, keepdims=True)
    for t in p[1:]:
        l = l + t.sum(axis=-1, keepdims=True)
    vs = (v0, v1, v2, v3, vc_ref)
    o = _dot(p[0].astype(BF16), vs[0][...].astype(BF16))
    for t, vr in zip(p[1:], vs[1:]):
        o = o + _dot(t.astype(BF16), vr[...].astype(BF16))
    o_ref[...] = o / l


def _na_call(p_all, bias, *, nb, seq, ctx_len):
    t_all = p_all.shape[0]
    nrows = seq // GRID_W
    nrb = nrows // NA_QROWS
    tq = NA_QROWS * GRID_W
    tk = 4 * GRID_W
    assert ctx_len == tk and nrows >= NA_KROWS
    kblocks_per_seq = seq // tk
    max_sb = kblocks_per_seq - 4
    ctx_blk0 = nb * seq // tk

    def sb(r):
        return jnp.clip(2 * r - 1, 0, max_sb)

    def kspec(j, colbase):
        return pl.BlockSpec((tk, HEAD_DIM), lambda h, b, r: (b * kblocks_per_seq + sb(r) + j, colbase + h))

    def variant(r):
        return jnp.where(r == 0, 0, jnp.where(r == nrb - 1, 2, 1))

    in_specs = [pl.BlockSpec((tq, HEAD_DIM), lambda h, b, r: (b * nrb + r, h))]
    in_specs += [kspec(j, N_HEADS) for j in range(4)]
    in_specs += [kspec(j, 2 * N_HEADS) for j in range(4)]
    in_specs += [
        pl.BlockSpec((tk, HEAD_DIM), lambda h, b, r: (ctx_blk0 + b, N_HEADS + h)),
        pl.BlockSpec((tk, HEAD_DIM), lambda h, b, r: (ctx_blk0 + b, 2 * N_HEADS + h)),
        pl.BlockSpec((None, None, tq, NA_KROWS * GRID_W), lambda h, b, r: (h, variant(r), 0, 0)),
    ]
    return pl.pallas_call(
        _na_kernel,
        grid=(N_HEADS, nb, nrb),
        in_specs=in_specs,
        out_specs=pl.BlockSpec((tq, HEAD_DIM), lambda h, b, r: (b * nrb + r, h)),
        out_shape=jax.ShapeDtypeStruct((nb * seq, GROUP_W), F32),
        compiler_params=_cparams(("arbitrary", "arbitrary", "arbitrary")),
        name="na_attention",
    )(*([p_all] * 11), bias)


def _ctx_attn_kernel(q_ref, k_ref, v_ref, o_ref):
    q = q_ref[...].astype(BF16)
    s = _dot_t(q, k_ref[...].astype(BF16)) * (HEAD_DIM ** -0.5)
    m = s.max(axis=-1, keepdims=True)
    p = jnp.exp(s - m)
    l = p.sum(axis=-1, keepdims=True)
    o_ref[...] = _dot(p.astype(BF16), v_ref[...].astype(BF16)) / l


def _ctx_attn_call(p_all, *, nb, seq, ctx_len):
    blk0 = nb * seq // ctx_len
    return pl.pallas_call(
        _ctx_attn_kernel,
        grid=(nb, N_HEADS),
        in_specs=[
            pl.BlockSpec((ctx_len, HEAD_DIM), lambda b, h: (blk0 + b, h)),
            pl.BlockSpec((ctx_len, HEAD_DIM), lambda b, h: (blk0 + b, N_HEADS + h)),
            pl.BlockSpec((ctx_len, HEAD_DIM), lambda b, h: (blk0 + b, 2 * N_HEADS + h)),
        ],
        out_specs=pl.BlockSpec((ctx_len, HEAD_DIM), lambda b, h: (b, h)),
        out_shape=jax.ShapeDtypeStruct((nb * ctx_len, GROUP_W), F32),
        compiler_params=_cparams(("arbitrary", "arbitrary")),
        name="ctx_attention",
    )(p_all, p_all, p_all)


def _rope_tables(seq, ctx_len):
    t = np.arange(seq)
    row = (t // GRID_W).astype(np.float32)
    col = (t % GRID_W).astype(np.float32)
    n_freq = HEAD_DIM // 4
    inv = (ROPE_THETA ** (-np.arange(n_freq, dtype=np.float32) / n_freq)).astype(np.float32)
    ang = jnp.stack([jnp.asarray(row)[:, None] * inv, jnp.asarray(col)[:, None] * inv], axis=1)
    cos, sin = jnp.cos(ang), jnp.sin(ang)
    c = jnp.concatenate([cos, cos], axis=-1).reshape(seq, HEAD_DIM)
    s = jnp.concatenate([-sin, sin], axis=-1).reshape(seq, HEAD_DIM)
    c = jnp.concatenate([c, jnp.ones((ctx_len, HEAD_DIM), F32)], axis=0)
    s = jnp.concatenate([s, jnp.zeros((ctx_len, HEAD_DIM), F32)], axis=0)
    return c, s


def _gdn_prep_kernel(cur_ref, prev_ref, next_ref, pab_ref, cw_ref, alog_ref, dtb_ref, rc_ref, rs_ref,
                     q_ref, k_ref, v_ref, g_ref, b_ref, ext_ref, *, n_lat_tiles, tps, nab_half):
    i = pl.program_id(0)
    tm = TM_PREP
    is_ctx = i >= n_lat_tiles
    first = jnp.logical_or(is_ctx, i % tps == 0)
    last = jnp.logical_or(is_ctx, i % tps == tps - 1)
    ext_ref[8:8 + tm, :] = cur_ref[...]
    ext_ref[0:8, :] = jnp.where(first, 0.0, prev_ref[...])
    ext_ref[8 + tm:16 + tm, :] = jnp.where(last, 0.0, next_ref[...])

    lane = lax.broadcasted_iota(jnp.int32, (tm, HEAD_DIM), 1)
    half0 = (lane % (HEAD_DIM // 2)) < (HEAD_DIM // 4)
    rc = rc_ref[...]
    rs = rs_ref[...]
    base = 8 - GDN_CONV // 2
    outs = (q_ref, k_ref, v_ref)
    for part in range(3):
        for h in range(N_HEADS):
            c0 = part * GROUP_W + h * HEAD_DIM
            acc = cw_ref[0:1, c0:c0 + HEAD_DIM] * ext_ref[base:base + tm, c0:c0 + HEAD_DIM]
            for t in range(1, GDN_CONV):
                acc = acc + cw_ref[t:t + 1, c0:c0 + HEAD_DIM] * ext_ref[base + t:base + t + tm, c0:c0 + HEAD_DIM]
            y = _silu(acc)
            if part < 2:
                y = y * lax.rsqrt(jnp.sum(y * y, axis=-1, keepdims=True) + 1e-6)
                partner = jnp.where(half0, pltpu.roll(y, HEAD_DIM - HEAD_DIM // 4, 1),
                                    pltpu.roll(y, HEAD_DIM // 4, 1))
                y = y * rc + partner * rs
                if part == 0:
                    y = y * (HEAD_DIM ** -0.5)
            outs[part][:, h * HEAD_DIM:(h + 1) * HEAD_DIM] = y

    a = pab_ref[:, 0:nab_half] + dtb_ref[...]
    softplus = jnp.maximum(a, 0.0) + jnp.log1p(jnp.exp(-jnp.abs(a)))
    g_ref[...] = -jnp.exp(alog_ref[...]) * softplus
    b_ref[...] = _sigmoid(pab_ref[:, nab_half:2 * nab_half])


def _gdn_prep_call(p_all, pab, conv_w, alog, dtb, rope_c, rope_s, *, nb, seq, ctx_len):
    t_all = p_all.shape[0]
    tm = TM_PREP
    assert ctx_len == tm
    n_tiles = t_all // tm
    n_lat_tiles = nb * seq // tm
    tps = seq // tm
    nab_half = pab.shape[1] // 2
    qkv_w = 3 * GROUP_W
    last8 = t_all // 8 - 1
    kern = functools.partial(_gdn_prep_kernel, n_lat_tiles=n_lat_tiles, tps=tps, nab_half=nab_half)

    def rope_idx(i):
        return jnp.where(i < n_lat_tiles, i % tps, tps)

    return pl.pallas_call(
        kern,
        grid=(n_tiles,),
        in_specs=[
            pl.BlockSpec((tm, qkv_w), lambda i: (i, 1)),
            pl.BlockSpec((8, qkv_w), lambda i: (jnp.maximum(i * (tm // 8) - 1, 0), 1)),
            pl.BlockSpec((8, qkv_w), lambda i: (jnp.minimum((i + 1) * (tm // 8), last8), 1)),
            pl.BlockSpec((tm, 2 * nab_half), lambda i: (i, 0)),
            pl.BlockSpec((8, qkv_w), lambda i: (0, 0)),
            pl.BlockSpec((1, nab_half), lambda i: (0, 0)),
            pl.BlockSpec((1, nab_half), lambda i: (0, 0)),
            pl.BlockSpec((tm, HEAD_DIM), lambda i: (rope_idx(i), 0)),
            pl.BlockSpec((tm, HEAD_DIM), lambda i: (rope_idx(i), 0)),
        ],
        out_specs=[
            pl.BlockSpec((tm, GROUP_W), lambda i: (i, 0)),
            pl.BlockSpec((tm, GROUP_W), lambda i: (i, 0)),
            pl.BlockSpec((tm, GROUP_W), lambda i: (i, 0)),
            pl.BlockSpec((tm, nab_half), lambda i: (i, 0)),
            pl.BlockSpec((tm, nab_half), lambda i: (i, 0)),
        ],
        out_shape=[
            jax.ShapeDtypeStruct((t_all, GROUP_W), F32),
            jax.ShapeDtypeStruct((t_all, GROUP_W), F32),
            jax.ShapeDtypeStruct((t_all, GROUP_W), F32),
            jax.ShapeDtypeStruct((t_all, nab_half), F32),
            jax.ShapeDtypeStruct((t_all, nab_half), F32),
        ],
        scratch_shapes=[pltpu.VMEM((tm + 16, qkv_w), F32)],
        compiler_params=_cparams(("arbitrary",)),
        name="gdn_prep",
    )(p_all, p_all, p_all, pab, conv_w, alog, dtb, rope_c, rope_s)


def _unit_tri_inverse(ns, orders):
    c = GDN_CHUNK
    ds = []
    for n, (row, col) in zip(ns, orders):
        m1 = jnp.logical_and(jnp.logical_and((row >> 1) == (col >> 1), (row & 1) == 1), (col & 1) == 0)
        ds.append((row == col).astype(F32) - jnp.where(m1, n, 0.0))
    k = 2
    while k < c:
        sh = int(np.log2(2 * k))
        dsp, xs = [], []
        for n, d, (row, col) in zip(ns, ds, orders):
            mk = jnp.logical_and((row >> sh) == (col >> sh),
                                 jnp.logical_and((row & (2 * k - 1)) >= k, (col & (2 * k - 1)) < k))
            lh, ll = _split_bf16(jnp.where(mk, n, 0.0))
            dh, dl = _split_bf16(d)
            dsp.append((dh, dl))
            xs.append(_dot3_split(lh, ll, dh, dl))
        ds = [d - _dot3_split(dh, dl, *_split_bf16(x)) for d, (dh, dl), x in zip(ds, dsp, xs)]
        k *= 2
    return ds


def _gdn_scan_kernel(qf, kf, vf, gf, bf, qb, kb, vb, gb, bb, of_ref, ob_ref, st_ref, *, hb):
    s = pl.program_id(2)
    c = GDN_CHUNK

    @pl.when(s == 0)
    def _():
        st_ref[...] = jnp.zeros(st_ref.shape, F32)

    ri = lax.broadcasted_iota(jnp.int32, (c, c), 0)
    ci = lax.broadcasted_iota(jnp.int32, (c, c), 1)
    zpad = jnp.zeros((HEAD_DIM - c, HEAD_DIM), F32)
    ch = []
    for d, (q_ref, k_ref, v_ref, g_ref, be_ref, o_ref) in enumerate(
            ((qf, kf, vf, gf, bf, of_ref), (qb, kb, vb, gb, bb, ob_ref))):
        incl = (ri >= ci) if d == 0 else (ri <= ci)
        gc_all = jnp.dot(incl.astype(F32), g_ref[...], precision=HIGHEST, preferred_element_type=F32)
        gc_t = jnp.transpose(jnp.concatenate([gc_all, zpad], axis=0))
        beta_all = be_ref[...]
        for hh in range(hb):
            ln = d * hb + hh
            sl = slice(hh * HEAD_DIM, (hh + 1) * HEAD_DIM)
            ch.append(dict(
                ln=ln, sl=sl, o_ref=o_ref, incl=incl,
                strict=(ri > ci) if d == 0 else (ri < ci),
                order=(ri, ci) if d == 0 else (ci, ri),
                last=c - 1 if d == 0 else 0,
                gcc=gc_all[:, ln:ln + 1], grow=gc_t[ln:ln + 1, 0:c], beta=beta_all[:, ln:ln + 1],
                q=q_ref[:, sl], k=k_ref[:, sl], v=v_ref[:, sl]))
    for t in ch:
        t["decay"] = jnp.where(t["incl"], jnp.exp(jnp.where(t["incl"], t["gcc"] - t["grow"], 0.0)), 0.0)
        t["kbeta"] = t["k"] * t["beta"]
    for t in ch:
        t["kk"] = _dot_t(t["kbeta"], t["k"])
    for t in ch:
        t["qk"] = _dot_t(t["q"], t["k"]) * t["decay"]
    ns = [jnp.where(t["strict"], t["kk"] * t["decay"], 0.0) for t in ch]
    tinvs = _unit_tri_inverse(ns, [t["order"] for t in ch])
    for t, tinv in zip(ch, tinvs):
        eg = jnp.exp(t["gcc"])
        t["eg"] = eg
        rhs = jnp.concatenate([t["v"] * t["beta"], t["kbeta"] * eg], axis=1)
        t["sol"] = _dot3_split(*_split_bf16(tinv), *_split_bf16(rhs))
    for t in ch:
        t["state"] = st_ref[t["ln"]]
        t["v_new"] = t["sol"][:, :HEAD_DIM] - _dot(t["sol"][:, HEAD_DIM:], t["state"])
    for t in ch:
        t["o_ref"][:, t["sl"]] = _dot(t["q"] * t["eg"], t["state"]) + _dot(t["qk"], t["v_new"])
    for t in ch:
        glast = t["gcc"][t["last"]:t["last"] + 1, :]
        kd_t = jnp.transpose(jnp.concatenate([t["k"] * jnp.exp(glast - t["gcc"]), zpad], axis=0))[:, 0:c]
        st_ref[t["ln"]] = t["state"] * jnp.exp(glast) + _dot(kd_t, t["v_new"])


def _gdn_scan_call(qn, kn, vn, g, beta, *, nb, seq, ctx_len):
    t_all = qn.shape[0]
    hb = GDN_HB
    c = GDN_CHUNK
    ncc = ctx_len // c
    ncl = seq // c
    ctx0 = nb * ncl
    n_hg = N_HEADS // hb

    def row_f(b, s):
        return jnp.where(s < ncc, ctx0 + b * ncc + s, b * ncl + (s - ncc))

    def row_b(b, s):
        return jnp.where(s < ncc, ctx0 + b * ncc + (ncc - 1 - s), b * ncl + (ncl - 1 - (s - ncc)))

    def specs(rowfn):
        big = pl.BlockSpec((c, hb * HEAD_DIM), lambda b, hg, s: (rowfn(b, s), hg))
        small = pl.BlockSpec((c, HEAD_DIM), lambda b, hg, s: (rowfn(b, s), hg))
        return [big, big, big, small, small]

    out_f = pl.BlockSpec((c, hb * HEAD_DIM), lambda b, hg, s: (row_f(b, s), hg))
    out_b = pl.BlockSpec((c, hb * HEAD_DIM), lambda b, hg, s: (row_b(b, s), hg))
    return pl.pallas_call(
        functools.partial(_gdn_scan_kernel, hb=hb),
        grid=(nb, n_hg, ncc + ncl),
        in_specs=specs(row_f) + specs(row_b),
        out_specs=[out_f, out_b],
        out_shape=[jax.ShapeDtypeStruct((t_all, GROUP_W), F32)] * 2,
        scratch_shapes=[pltpu.VMEM((2 * hb, HEAD_DIM, HEAD_DIM), F32)],
        compiler_params=_cparams(("arbitrary", "arbitrary", "arbitrary")),
        name="gdn_scan",
    )(qn, kn, vn, g, beta, qn, kn, vn, g, beta)


def _outproj_kernel(nal_ref, nac_ref, of_ref, ob_ref, z_ref, x_ref, ga_ref, shf_ref, scf_ref, w_ref, ng_ref, lg_ref,
                    lb_ref, x1_ref, h2t_ref, gdn_scr, *, tiles_per_seq, nb):
    i = pl.program_id(0)
    r = jnp.minimum(i // tiles_per_seq, nb)
    na = jnp.where(i < nb * tiles_per_seq, nal_ref[...], nac_ref[...])
    o = of_ref[...] + ob_ref[...]
    ng = ng_ref[...]
    for h in range(N_HEADS):
        sl = slice(h * HEAD_DIM, (h + 1) * HEAD_DIM)
        oh = o[:, sl]
        oh = oh * lax.rsqrt(jnp.mean(oh * oh, axis=-1, keepdims=True) + 1e-6) * ng
        gdn_scr[:, sl] = (oh * _silu(z_ref[:, sl])).astype(BF16)
    mix = _dot(na.astype(BF16), w_ref[0:GROUP_W, :]) + _dot(gdn_scr[...], w_ref[GROUP_W:2 * GROUP_W, :])
    y = DEEPNORM_ALPHA * x_ref[...] + ga_ref[pl.ds(r, 1), :] * mix
    x1 = _layer_norm_rows(y, lg_ref[...], lb_ref[...])
    x1_ref[...] = x1
    h2 = x1 * (1.0 + scf_ref[pl.ds(r, 1), :]) + shf_ref[pl.ds(r, 1), :]
    h2t_ref[...] = jnp.transpose(h2).astype(BF16)


def _outproj_call(na_lat, na_ctx, o_f, o_b, p_all, xa, mod_l, w_out, norm_g, ln_g, ln_b, *, seq, nb, rows):
    t_all, d = rows, xa.shape[1]
    tm = TM_OUT
    n_lat_tiles = nb * seq // tm
    kern = functools.partial(_outproj_kernel, tiles_per_seq=seq // tm, nb=nb)
    row = lambda i: (i, 0)
    const = lambda i: (0, 0)
    return pl.pallas_call(
        kern,
        grid=(t_all // tm,),
        in_specs=[
            pl.BlockSpec((tm, GROUP_W), lambda i: (jnp.minimum(i, n_lat_tiles - 1), 0)),
            pl.BlockSpec((tm, GROUP_W), lambda i: (jnp.maximum(i - n_lat_tiles, 0), 0)),
            pl.BlockSpec((tm, GROUP_W), row),
            pl.BlockSpec((tm, GROUP_W), row),
            pl.BlockSpec((tm, GROUP_W), lambda i: (i, 6)),
            pl.BlockSpec((tm, d), row),
            pl.BlockSpec((8, d), lambda i: (0, 2)),
            pl.BlockSpec((8, d), lambda i: (0, 3)),
            pl.BlockSpec((8, d), lambda i: (0, 4)),
            pl.BlockSpec((d, d), const),
            pl.BlockSpec((1, HEAD_DIM), const),
            pl.BlockSpec((1, d), const),
            pl.BlockSpec((1, d), const),
        ],
        out_specs=[
            pl.BlockSpec((tm, d), row),
            pl.BlockSpec((d, tm), lambda i: (0, i)),
        ],
        out_shape=[
            jax.ShapeDtypeStruct((t_all, d), F32),
            jax.ShapeDtypeStruct((d, t_all), BF16),
        ],
        scratch_shapes=[pltpu.VMEM((tm, GROUP_W), BF16)],
        compiler_params=_cparams(("arbitrary",)),
        name="out_proj_ln",
    )(na_lat, na_ctx, o_f, o_b, p_all, xa, mod_l, mod_l, mod_l, w_out, norm_g, ln_g, ln_b)


def _dup_bf16_words(x):
    u = pltpu.bitcast(x.astype(BF16).astype(F32), jnp.uint32)
    return u | (u >> 16)


def _peer_score_kernel(ht_ref, w_ref, sk_ref, l_ref, ea_ref, rb_ref, eb_ref, s_scr, atop, btop, cand):
    tm = TM_PEER
    qt = _dot(w_ref[...], ht_ref[...])
    half = PEER_QDIM // 2
    s_scr[0] = _dot(sk_ref[0], qt[0:half])
    s_scr[1] = _dot(sk_ref[1], qt[half:2 * half])
    n_half = PEER_TOPK // 2
    for tc in range(tm // 128):
        ls = slice(tc * 128, (tc + 1) * 128)
        s0 = s_scr[0, :, ls]
        s1 = s_scr[1, :, ls]
        sc = s0
        for k in range(PEER_TOPK):
            m = jnp.max(sc, axis=0, keepdims=True)
            atop[k:k + 1, ls] = m
            sc = jnp.where(sc == m, -jnp.inf, sc)
        sc = s1
        rank = jnp.full(s1.shape, float(PEER_TOPK), F32)
        for k in range(PEER_TOPK):
            m = jnp.max(sc, axis=0, keepdims=True)
            btop[k:k + 1, ls] = m
            hit = sc == m
            rank = jnp.minimum(rank, jnp.where(hit, float(k), float(PEER_TOPK)))
            sc = jnp.where(hit, -jnp.inf, sc)
        bt = btop[:, ls]
        for i in range(n_half):
            cand[i * PEER_TOPK:(i + 1) * PEER_TOPK, ls] = atop[i:i + 1, ls] + bt
        cand[n_half * PEER_TOPK:n_half * PEER_TOPK + n_half, ls] = atop[n_half:PEER_TOPK, ls] + bt[0:1, :]
        cv = cand[:, ls]
        m0 = jnp.max(cv, axis=0, keepdims=True)
        z = jnp.zeros_like(m0)
        tau = m0
        for k in range(PEER_TOPK):
            tau = jnp.max(cv, axis=0, keepdims=True)
            z = z + jnp.exp(tau - m0)
            cv = jnp.where(cv == tau, -jnp.inf, cv)
        cnt = jnp.zeros(s0.shape, F32)
        for k in range(PEER_TOPK):
            cnt = cnt + jnp.where(s0 + bt[k:k + 1, :] >= tau, 1.0, 0.0)
        l_ref[0, :, ls] = _dup_bf16_words(cnt)
        ea_ref[0, :, ls] = _dup_bf16_words(jnp.exp(s0 - atop[0:1, ls]) * (1.0 / z))
        rb_ref[0, :, ls] = rank.astype(BF16)
        eb_ref[0, :, ls] = jnp.exp(s1 - bt[0:1, :]).astype(BF16)


def _peer_score_call(h2t, wq_t, subkeys):
    d, t_all = h2t.shape
    tm = TM_PEER
    ncand = PEER_TOPK * PEER_TOPK // 2 + PEER_TOPK // 2
    tok = lambda i, h: (h, 0, i)
    return pl.pallas_call(
        _peer_score_kernel,
        grid=(t_all // tm, PEER_HEADS),
        in_specs=[
            pl.BlockSpec((d, tm), lambda i, h: (0, i)),
            pl.BlockSpec((PEER_QDIM, d), lambda i, h: (h, 0)),
            pl.BlockSpec((2, PEER_NKEYS, PEER_QDIM // 2), lambda i, h: (0, 0, 0)),
        ],
        out_specs=[pl.BlockSpec((1, PEER_NKEYS, tm), tok)] * 4,
        out_shape=[
            jax.ShapeDtypeStruct((PEER_HEADS, PEER_NKEYS, t_all), jnp.uint32),
            jax.ShapeDtypeStruct((PEER_HEADS, PEER_NKEYS, t_all), jnp.uint32),
            jax.ShapeDtypeStruct((PEER_HEADS, PEER_NKEYS, t_all), BF16),
            jax.ShapeDtypeStruct((PEER_HEADS, PEER_NKEYS, t_all), BF16),
        ],
        scratch_shapes=[
            pltpu.VMEM((2, PEER_NKEYS, tm), F32),
            pltpu.VMEM((PEER_TOPK, tm), F32),
            pltpu.VMEM((PEER_TOPK, tm), F32),
            pltpu.VMEM((ncand, tm), F32),
        ],
        compiler_params=_cparams(("arbitrary", "arbitrary")),
        name="peer_scores",
    )(h2t, wq_t, subkeys)


def _peer_dense_kernel(h_ref, u_ref, vt_ref, l_ref, ea_ref, rb_ref, eb_ref, o_ref, act0, act1, w0, w1):
    m = pl.program_id(1)
    n_tiles = pl.num_programs(1) - 2
    tm = TM_PEER
    nk = PEER_NKEYS
    rows_per_step = TE_PEER // nk
    jq_rows = 32
    n_jq = nk // jq_rows

    @pl.when(m == 0)
    def _():
        o_ref[...] = jnp.zeros(o_ref.shape, F32)
        for r in (act0, act1):
            r[...] = jnp.zeros(r.shape, F32)
        for r in (w0, w1):
            r[...] = jnp.zeros(r.shape, BF16)

    valid_b = jnp.logical_and(m >= 1, m <= n_tiles)
    sqrt_half = np.float32(np.sqrt(0.5))

    def stages(act_a, w_c, act_b, w_b):
        n_tc = tm // 128

        def stage_a(n, r):
            cs = slice(n * 256, (n + 1) * 256)
            rs = slice(r * 128, (r + 1) * 128)
            act_a[rs, cs] = _dot(u_ref[rs, :], h_ref[:, cs])

        def stage_c(q):
            rs = slice(q * 128, (q + 1) * 128)
            o_ref[rs, :] += _dot(vt_ref[rs, :], w_c[...])

        def stage_b(tc, jq):
            ls = slice(tc * 128, (tc + 1) * 128)
            js = slice(jq * (jq_rows // 16), (jq + 1) * (jq_rows // 16))
            g = [jnp.zeros((jq_rows // 16, 16, 128), BF16) for _ in range(rows_per_step)]
            for h in range(PEER_HEADS):
                rk = rb_ref[h, js, :, ls]
                ee = eb_ref[h, js, :, ls]
                for ii in range(rows_per_step):
                    l_b = pltpu.bitcast(jnp.broadcast_to(l_ref[h, ii:ii + 1, ls], (8, 128)), BF16)
                    e_b = pltpu.bitcast(jnp.broadcast_to(ea_ref[h, ii:ii + 1, ls], (8, 128)), BF16)
                    g[ii] = g[ii] + jnp.where(rk < l_b, ee * e_b, jnp.zeros((), BF16))
            for ii in range(rows_per_step):
                r0 = ii * nk + jq * jq_rows
                xa = act_b[r0:r0 + jq_rows, ls]
                ge = (0.5 * xa * (1.0 + lax.erf(xa * sqrt_half))).astype(BF16)
                wv = jnp.where(valid_b, ge * g[ii].reshape(jq_rows, 128), jnp.zeros((), BF16))
                w_b[r0:r0 + jq_rows, ls] = wv

        mxu = []
        for q in range(8):
            mxu += [functools.partial(stage_a, q // 4, q % 4), functools.partial(stage_c, 2 * q),
                    functools.partial(stage_c, 2 * q + 1)]
        blocks = [(tc, jq) for tc in range(n_tc) for jq in range(n_jq)]
        mi = 0
        for bi, (tc, jq) in enumerate(blocks):
            while mi < len(mxu) and mi * len(blocks) <= bi * len(mxu):
                mxu[mi]()
                mi += 1
            stage_b(tc, jq)
        while mi < len(mxu):
            mxu[mi]()
            mi += 1

    @pl.when(m % 2 == 0)
    def _():
        stages(act0, w0, act1, w1)

    @pl.when(m % 2 == 1)
    def _():
        stages(act1, w1, act0, w0)


def _peer_dense_call(hb, u_bf, vt_bf, l_pk, ea_pk, rb, eb):
    d, t_all = hb.shape
    ne = u_bf.shape[0]
    tm, te = TM_PEER, TE_PEER
    n_tiles = ne // te
    rows = te // PEER_NKEYS
    row_tile = lambda i, m: (0, jnp.clip(m - 1, 0, n_tiles - 1), 0, i)
    all_j = lambda i, m: (0, 0, 0, i)
    l4 = l_pk.reshape(PEER_HEADS, n_tiles, rows, t_all)
    ea4 = ea_pk.reshape(PEER_HEADS, n_tiles, rows, t_all)
    rb4 = rb.reshape(PEER_HEADS, PEER_NKEYS // 16, 16, t_all)
    eb4 = eb.reshape(PEER_HEADS, PEER_NKEYS // 16, 16, t_all)
    return pl.pallas_call(
        _peer_dense_kernel,
        grid=(t_all // tm, n_tiles + 2),
        in_specs=[
            pl.BlockSpec((d, tm), lambda i, m: (0, i)),
            pl.BlockSpec((te, d), lambda i, m: (jnp.minimum(m, n_tiles - 1), 0)),
            pl.BlockSpec((d, te), lambda i, m: (0, jnp.maximum(m - 2, 0))),
            pl.BlockSpec((PEER_HEADS, None, rows, tm), row_tile),
            pl.BlockSpec((PEER_HEADS, None, rows, tm), row_tile),
            pl.BlockSpec((PEER_HEADS, PEER_NKEYS // 16, 16, tm), all_j),
            pl.BlockSpec((PEER_HEADS, PEER_NKEYS // 16, 16, tm), all_j),
        ],
        out_specs=pl.BlockSpec((d, tm), lambda i, m: (0, i)),
        out_shape=jax.ShapeDtypeStruct((d, t_all), F32),
        scratch_shapes=[
            pltpu.VMEM((te, tm), F32),
            pltpu.VMEM((te, tm), F32),
            pltpu.VMEM((te, tm), BF16),
            pltpu.VMEM((te, tm), BF16),
        ],
        compiler_params=_cparams(("arbitrary", "arbitrary")),
        name="peer_dense",
    )(hb, u_bf, vt_bf, l4, ea4, rb4, eb4)


def _peer_out_kernel(ft_ref, x1_ref, gf_ref, lg_ref, lb_ref, o_ref, *, tiles_per_seq, nb):
    i = pl.program_id(0)
    r = jnp.minimum(i // tiles_per_seq, nb)
    y = DEEPNORM_ALPHA * x1_ref[...] + gf_ref[pl.ds(r, 1), :] * jnp.transpose(ft_ref[...])
    o_ref[...] = _layer_norm_rows(y, lg_ref[...], lb_ref[...])


def _peer_out_call(ffn_t, x1, mod_l, ln_g, ln_b, *, seq, nb):
    t_all, d = x1.shape
    tm = TM_OUT
    kern = functools.partial(_peer_out_kernel, tiles_per_seq=seq // tm, nb=nb)
    return pl.pallas_call(
        kern,
        grid=(t_all // tm,),
        in_specs=[
            pl.BlockSpec((d, tm), lambda i: (0, i)),
            pl.BlockSpec((tm, d), lambda i: (i, 0)),
            pl.BlockSpec((8, d), lambda i: (0, 5)),
            pl.BlockSpec((1, d), lambda i: (0, 0)),
            pl.BlockSpec((1, d), lambda i: (0, 0)),
        ],
        out_specs=pl.BlockSpec((tm, d), lambda i: (i, 0)),
        out_shape=jax.ShapeDtypeStruct((t_all, d), F32),
        compiler_params=_cparams(("arbitrary",)),
        name="peer_out_ln",
    )(ffn_t, x1, mod_l, ln_g, ln_b)


def _ab_columns(w_tail):
    depth, d, _ = w_tail.shape
    hb = GDN_HB
    n_hg = N_HEADS // hb
    wa = w_tail[:, :, :2 * N_HEADS].reshape(depth, d, 2, n_hg, hb)
    wb = w_tail[:, :, 2 * N_HEADS:].reshape(depth, d, 2, n_hg, hb)

    def lay(w):
        w = w.transpose(0, 1, 3, 2, 4).reshape(depth, d, n_hg, 2 * hb)
        w = jnp.pad(w, ((0, 0), (0, 0), (0, 0), (0, HEAD_DIM - 2 * hb)))
        return w.reshape(depth, d, n_hg * HEAD_DIM)

    return jnp.concatenate([lay(wa), lay(wb)], axis=-1)


def _head_param_lanes(p):
    depth = p.shape[0]
    hb = GDN_HB
    n_hg = N_HEADS // hb
    p = p.reshape(depth, 2, n_hg, hb).transpose(0, 2, 1, 3).reshape(depth, n_hg, 2 * hb)
    p = jnp.pad(p, ((0, 0), (0, 0), (0, HEAD_DIM - 2 * hb)))
    return p.reshape(depth, 1, n_hg * HEAD_DIM)


def kernel(x, c, ctx, c_ctx, w_mod, b_mod, w_in, w_out, na_rpb, gdn_conv, gdn_a_log, gdn_dt_bias, gdn_norm_g,
           peer_wq, peer_subkeys, peer_u, peer_v, ln_g, ln_b):
    nb, seq, d = x.shape
    ctx_len = ctx.shape[1]
    depth = w_mod.shape[0]
    assert d == D_MODEL and nb + 1 <= 8
    assert seq % (NA_QROWS * GRID_W) == 0 and seq % TM_PROJ == 0
    t_lat = nb * seq
    dims = dict(nb=nb, seq=seq, ctx_len=ctx_len)

    xa = jnp.concatenate([x.reshape(t_lat, d), ctx.reshape(nb * ctx_len, d)], axis=0)
    cs = jnp.concatenate([c, c_ctx[None, :], jnp.zeros((8 - nb - 1, d), F32)], axis=0)
    mod = _mod_call(cs, w_mod, b_mod)

    w_main = w_in[:, :, :P_MAIN_W].astype(BF16)
    w_ab = _ab_columns(w_in[:, :, P_MAIN_W:]).astype(BF16)
    w_out_bf = w_out.astype(BF16)
    alog = _head_param_lanes(gdn_a_log)
    dtb = _head_param_lanes(gdn_dt_bias)
    conv_w = jnp.pad(gdn_conv, ((0, 0), (0, 8 - GDN_CONV), (0, 0)))
    rope_c, rope_s = _rope_tables(seq, ctx_len)
    wq_t = peer_wq.astype(BF16).transpose(0, 2, 1)
    u_bf = peer_u.astype(BF16)
    vt_bf = peer_v.astype(BF16).transpose(0, 2, 1)

    for l in range(depth):
        p_all, pab = _inproj_call(xa, mod[l], w_main[l], w_ab[l], seq=seq, nb=nb)
        bias = _na_bias(na_rpb[l], seq // GRID_W)
        na_lat = _na_call(p_all, bias, **dims)
        na_ctx = _ctx_attn_call(p_all, **dims)
        qn, kn, vn, g, beta = _gdn_prep_call(p_all, pab, conv_w[l], alog[l], dtb[l], rope_c, rope_s, **dims)
        o_f, o_b = _gdn_scan_call(qn, kn, vn, g, beta, **dims)
        x1, h2t = _outproj_call(na_lat, na_ctx, o_f, o_b, p_all, xa, mod[l], w_out_bf[l], gdn_norm_g[l][None, :],
                                ln_g[l, 0][None, :], ln_b[l, 0][None, :], seq=seq, nb=nb,
                                rows=xa.shape[0] if l < depth - 1 else t_lat)
        l_pk, ea_pk, rb, eb = _peer_score_call(h2t, wq_t[l], peer_subkeys[l])
        ffn_t = _peer_dense_call(h2t, u_bf[l], vt_bf[l], l_pk, ea_pk, rb, eb)
        xa = _peer_out_call(ffn_t, x1, mod[l], ln_g[l, 1][None, :], ln_b[l, 1][None, :], seq=seq, nb=nb)
    return xa.reshape(nb, seq, d)
```

```python
import functools

import numpy as np
import jax
import jax.numpy as jnp
from jax import lax
from jax.experimental import pallas as pl
from jax.experimental.pallas import tpu as pltpu

F32 = jnp.float32
BF16 = jnp.bfloat16
HIGHEST = lax.Precision.HIGHEST

D_MODEL = 2048
HEAD_DIM = 128
N_HEADS = 8
GROUP_W = N_HEADS * HEAD_DIM
GRID_W = 64
NA_KR = 8
NA_KC = 16
NA_QROWS = 8
NA_KROWS = 16
NA_SUB = 2
GDN_CHUNK = 64
GDN_CONV = 5
GDN_HB = 8
ROPE_THETA = 10000.0
PEER_HEADS = 8
PEER_TOPK = 16
PEER_NKEYS = 128
PEER_QDIM = 256
N_MOD = 6
LN_EPS = 1e-6
NEG_INF = -1e30
DEPTH_FOR_DEEPNORM = 4
DEEPNORM_ALPHA = (2 * DEPTH_FOR_DEEPNORM) ** 0.25
P_MAIN_W = 7 * GROUP_W
VMEM_LIMIT = 56 * 1024 * 1024

TM_PROJ = 512
TN_PROJ = 1024
TM_OUT = 256
TM_PREP = 256
TM_PEER = 512
TE_PEER = 512


def _cparams(sem):
    return pltpu.CompilerParams(dimension_semantics=sem, vmem_limit_bytes=VMEM_LIMIT)


def _sigmoid(x):
    return 1.0 / (1.0 + jnp.exp(-x))


def _silu(x):
    return x * _sigmoid(x)


def _dot(a, b):
    return jnp.dot(a, b, preferred_element_type=F32)


def _dot_t(a, b):
    return lax.dot_general(a, b, (((1,), (1,)), ((), ())), preferred_element_type=F32)


def _split_bf16(a):
    hi = a.astype(BF16)
    lo = (a - hi.astype(F32)).astype(BF16)
    return hi, lo


def _dot3_split(ah, al, bh, bl):
    return _dot(ah, bh) + (_dot(ah, bl) + _dot(al, bh))


def _layer_norm_rows(y, g, b):
    mu = jnp.mean(y, axis=-1, keepdims=True)
    yc = y - mu
    var = jnp.mean(yc * yc, axis=-1, keepdims=True)
    return yc * lax.rsqrt(var + LN_EPS) * g + b


def _mod_kernel(c_ref, w_ref, b_ref, o_ref):
    s = _silu(c_ref[...])
    o_ref[0] = jnp.dot(s, w_ref[0], precision=HIGHEST, preferred_element_type=F32) + b_ref[0]


def _mod_call(cs, w_mod, b_mod):
    depth, d, n = w_mod.shape
    tn = 1024
    return pl.pallas_call(
        _mod_kernel,
        grid=(depth, n // tn),
        in_specs=[
            pl.BlockSpec((8, d), lambda l, j: (0, 0)),
            pl.BlockSpec((1, d, tn), lambda l, j: (l, 0, j)),
            pl.BlockSpec((1, 1, tn), lambda l, j: (l, 0, j)),
        ],
        out_specs=pl.BlockSpec((1, 8, tn), lambda l, j: (l, 0, j)),
        out_shape=jax.ShapeDtypeStruct((depth, 8, n), F32),
        compiler_params=_cparams(("arbitrary", "arbitrary")),
        name="adaln_mod",
    )(cs, w_mod, b_mod.reshape(depth, 1, n))


def _inproj_kernel(x_ref, sh_ref, sc_ref, w_ref, wab_ref, p_ref, pab_ref, h_scr, *, tiles_per_seq, nb):
    i = pl.program_id(0)
    j = pl.program_id(1)

    @pl.when(j == 0)
    def _():
        r = jnp.minimum(i // tiles_per_seq, nb)
        sh = sh_ref[pl.ds(r, 1), :]
        sc = sc_ref[pl.ds(r, 1), :]
        hb = (x_ref[...] * (1.0 + sc) + sh).astype(BF16)
        h_scr[...] = hb
        pab_ref[...] = _dot(hb, wab_ref[...])

    p_ref[...] = _dot(h_scr[...], w_ref[...])


def _inproj_call(xa, mod_l, w_main, w_ab, *, seq, nb):
    t_all, d = xa.shape
    n = w_main.shape[1]
    nab = w_ab.shape[1]
    tm, tn = TM_PROJ, TN_PROJ
    kern = functools.partial(_inproj_kernel, tiles_per_seq=seq // tm, nb=nb)
    return pl.pallas_call(
        kern,
        grid=(t_all // tm, n // tn),
        in_specs=[
            pl.BlockSpec((tm, d), lambda i, j: (i, 0)),
            pl.BlockSpec((8, d), lambda i, j: (0, 0)),
            pl.BlockSpec((8, d), lambda i, j: (0, 1)),
            pl.BlockSpec((d, tn), lambda i, j: (0, j)),
            pl.BlockSpec((d, nab), lambda i, j: (0, 0)),
        ],
        out_specs=[
            pl.BlockSpec((tm, tn), lambda i, j: (i, j)),
            pl.BlockSpec((tm, nab), lambda i, j: (i, 0)),
        ],
        out_shape=[
            jax.ShapeDtypeStruct((t_all, n), F32),
            jax.ShapeDtypeStruct((t_all, nab), F32),
        ],
        scratch_shapes=[pltpu.VMEM((tm, d), BF16)],
        compiler_params=_cparams(("arbitrary", "arbitrary")),
        name="in_proj",
    )(xa, mod_l, mod_l, w_main, w_ab)


def _na_bias_index_tables(nrows):
    rq, rk = NA_QROWS, NA_KROWS
    big = 1 << 20
    cfg = [(0, 0, nrows), (8, 4, big), (nrows - rq, nrows - rk, nrows)]
    dr = np.zeros((3, rq, rk), np.int32)
    rv = np.zeros((3, rq, rk), bool)
    for v, (r0, start, nr) in enumerate(cfg):
        r = r0 + np.arange(rq)[:, None]
        kr = start + np.arange(rk)[None, :]
        rs = np.clip(r - NA_KR // 2, 0, nr - NA_KR)
        rv[v] = (kr >= rs) & (kr < rs + NA_KR)
        dr[v] = np.clip(kr - r + NA_KR - 1, 0, 2 * NA_KR - 2)
    qc = np.arange(GRID_W)[:, None]
    kc = np.arange(GRID_W)[None, :]
    ws = np.clip(qc - NA_KC // 2, 0, GRID_W - NA_KC)
    cv = (kc >= ws) & (kc < ws + NA_KC)
    dc = np.clip(kc - qc + NA_KC - 1, 0, 2 * NA_KC - 2).astype(np.int32)
    return dr, rv, dc, cv


def _na_bias(rpb_l, nrows):
    dr, rv, dc, cv = _na_bias_index_tables(nrows)
    tc = jnp.where(cv[None, None], rpb_l[:, :, dc], NEG_INF)
    blk = tc[:, dr]
    blk = jnp.where(rv[None, :, :, :, None, None], blk, NEG_INF)
    blk = blk.transpose(0, 1, 2, 4, 3, 5)
    return blk.reshape(N_HEADS, 3, NA_QROWS * GRID_W, NA_KROWS * GRID_W)


def _na_kernel(q_ref, k0, k1, k2, k3, v0, v1, v2, v3, kc_ref, vc_ref, bias_ref, o_ref):
    kb = 4 * GRID_W
    ks = [kr[...].astype(BF16) for kr in (k0, k1, k2, k3, kc_ref)]
    vs = [vr[...].astype(BF16) for vr in (v0, v1, v2, v3, vc_ref)]
    rows = q_ref.shape[0] // NA_SUB
    for sub in range(NA_SUB):
        rs = slice(sub * rows, (sub + 1) * rows)
        q = (q_ref[rs, :] * (HEAD_DIM ** -0.5)).astype(BF16)
        s = [_dot_t(q, ks[j]) + bias_ref[rs, j * kb:(j + 1) * kb] for j in range(4)]
        s.append(_dot_t(q, ks[4]))
        m = s[0].max(axis=-1, keepdims=True)
        for t in s[1:]:
            m = jnp.maximum(m, t.max(axis=-1, keepdims=True))
        p = [jnp.exp(t - m) for t in s]
        l = p[0].sum(axis=-1, keepdims=True)
        for t in p[1:]:
            l = l + t.sum(axis=-1, keepdims=True)
        o = _dot(p[0].astype(BF16), vs[0])
        for t, vv in zip(p[1:], vs[1:]):
            o = o + _dot(t.astype(BF16), vv)
        o_ref[rs, :] = o / l


def _na_call(p_all, bias, *, nb, seq, ctx_len):
    t_all = p_all.shape[0]
    nrows = seq // GRID_W
    nrb = nrows // NA_QROWS
    tq = NA_QROWS * GRID_W
    tk = 4 * GRID_W
    assert ctx_len == tk and nrows >= NA_KROWS
    kblocks_per_seq = seq // tk
    max_sb = kblocks_per_seq - 4
    ctx_blk0 = nb * seq // tk

    def sb(r):
        return jnp.clip(2 * r - 1, 0, max_sb)

    def kspec(j, colbase):
        return pl.BlockSpec((tk, HEAD_DIM), lambda h, b, r: (b * kblocks_per_seq + sb(r) + j, colbase + h))

    def variant(r):
        return jnp.where(r == 0, 0, jnp.where(r == nrb - 1, 2, 1))

    in_specs = [pl.BlockSpec((tq, HEAD_DIM), lambda h, b, r: (b * nrb + r, h))]
    in_specs += [kspec(j, N_HEADS) for j in range(4)]
    in_specs += [kspec(j, 2 * N_HEADS) for j in range(4)]
    in_specs += [
        pl.BlockSpec((tk, HEAD_DIM), lambda h, b, r: (ctx_blk0 + b, N_HEADS + h)),
        pl.BlockSpec((tk, HEAD_DIM), lambda h, b, r: (ctx_blk0 + b, 2 * N_HEADS + h)),
        pl.BlockSpec((None, None, tq, NA_KROWS * GRID_W), lambda h, b, r: (h, variant(r), 0, 0)),
    ]
    return pl.pallas_call(
        _na_kernel,
        grid=(N_HEADS, nb, nrb),
        in_specs=in_specs,
        out_specs=pl.BlockSpec((tq, HEAD_DIM), lambda h, b, r: (b * nrb + r, h)),
        out_shape=jax.ShapeDtypeStruct((nb * seq, GROUP_W), F32),
        compiler_params=_cparams(("arbitrary", "arbitrary", "arbitrary")),
        name="na_attention",
    )(*([p_all] * 11), bias)


def _ctx_attn_kernel(q_ref, k_ref, v_ref, o_ref):
    q = q_ref[...].astype(BF16)
    s = _dot_t(q, k_ref[...].astype(BF16)) * (HEAD_DIM ** -0.5)
    m = s.max(axis=-1, keepdims=True)
    p = jnp.exp(s - m)
    l = p.sum(axis=-1, keepdims=True)
    o_ref[...] = _dot(p.astype(BF16), v_ref[...].astype(BF16)) / l


def _ctx_attn_call(p_all, *, nb, seq, ctx_len):
    blk0 = nb * seq // ctx_len
    return pl.pallas_call(
        _ctx_attn_kernel,
        grid=(nb, N_HEADS),
        in_specs=[
            pl.BlockSpec((ctx_len, HEAD_DIM), lambda b, h: (blk0 + b, h)),
            pl.BlockSpec((ctx_len, HEAD_DIM), lambda b, h: (blk0 + b, N_HEADS + h)),
            pl.BlockSpec((ctx_len, HEAD_DIM), lambda b, h: (blk0 + b, 2 * N_HEADS + h)),
        ],
        out_specs=pl.BlockSpec((ctx_len, HEAD_DIM), lambda b, h: (b, h)),
        out_shape=jax.ShapeDtypeStruct((nb * ctx_len, GROUP_W), F32),
        compiler_params=_cparams(("arbitrary", "arbitrary")),
        name="ctx_attention",
    )(p_all, p_all, p_all)


def _rope_tables(seq, ctx_len):
    t = np.arange(seq)
    row = (t // GRID_W).astype(np.float32)
    col = (t % GRID_W).astype(np.float32)
    n_freq = HEAD_DIM // 4
    inv = (ROPE_THETA ** (-np.arange(n_freq, dtype=np.float32) / n_freq)).astype(np.float32)
    ang = jnp.stack([jnp.asarray(row)[:, None] * inv, jnp.asarray(col)[:, None] * inv], axis=1)
    cos, sin = jnp.cos(ang), jnp.sin(ang)
    c = jnp.concatenate([cos, cos], axis=-1).reshape(seq, HEAD_DIM)
    s = jnp.concatenate([-sin, sin], axis=-1).reshape(seq, HEAD_DIM)
    c = jnp.concatenate([c, jnp.ones((ctx_len, HEAD_DIM), F32)], axis=0)
    s = jnp.concatenate([s, jnp.zeros((ctx_len, HEAD_DIM), F32)], axis=0)
    return c, s


def _gdn_prep_kernel(cur_ref, prev_ref, next_ref, pab_ref, cw_ref, alog_ref, dtb_ref, rc_ref, rs_ref,
                     q_ref, k_ref, v_ref, g_ref, b_ref, ext_ref, *, n_lat_tiles, tps, nab_half):
    i = pl.program_id(0)
    tm = TM_PREP
    is_ctx = i >= n_lat_tiles
    first = jnp.logical_or(is_ctx, i % tps == 0)
    last = jnp.logical_or(is_ctx, i % tps == tps - 1)
    ext_ref[8:8 + tm, :] = cur_ref[...]
    ext_ref[0:8, :] = jnp.where(first, 0.0, prev_ref[...])
    ext_ref[8 + tm:16 + tm, :] = jnp.where(last, 0.0, next_ref[...])

    lane = lax.broadcasted_iota(jnp.int32, (tm, HEAD_DIM), 1)
    half0 = (lane % (HEAD_DIM // 2)) < (HEAD_DIM // 4)
    rc = rc_ref[...]
    rs = rs_ref[...]
    base = 8 - GDN_CONV // 2
    outs = (q_ref, k_ref, v_ref)
    for part in range(3):
        for h in range(N_HEADS):
            c0 = part * GROUP_W + h * HEAD_DIM
            acc = cw_ref[0:1, c0:c0 + HEAD_DIM] * ext_ref[base:base + tm, c0:c0 + HEAD_DIM]
            for t in range(1, GDN_CONV):
                acc = acc + cw_ref[t:t + 1, c0:c0 + HEAD_DIM] * ext_ref[base + t:base + t + tm, c0:c0 + HEAD_DIM]
            y = _silu(acc)
            if part < 2:
                y = y * lax.rsqrt(jnp.sum(y * y, axis=-1, keepdims=True) + 1e-6)
                partner = jnp.where(half0, pltpu.roll(y, HEAD_DIM - HEAD_DIM // 4, 1),
                                    pltpu.roll(y, HEAD_DIM // 4, 1))
                y = y * rc + partner * rs
                if part == 0:
                    y = y * (HEAD_DIM ** -0.5)
            outs[part][:, h * HEAD_DIM:(h + 1) * HEAD_DIM] = y

    a = pab_ref[:, 0:nab_half] + dtb_ref[...]
    softplus = jnp.maximum(a, 0.0) + jnp.log1p(jnp.exp(-jnp.abs(a)))
    g_ref[...] = -jnp.exp(alog_ref[...]) * softplus
    b_ref[...] = _sigmoid(pab_ref[:, nab_half:2 * nab_half])


def _gdn_prep_call(p_all, pab, conv_w, alog, dtb, rope_c, rope_s, *, nb, seq, ctx_len):
    t_all = p_all.shape[0]
    tm = TM_PREP
    assert ctx_len == tm
    n_tiles = t_all // tm
    n_lat_tiles = nb * seq // tm
    tps = seq // tm
    nab_half = pab.shape[1] // 2
    qkv_w = 3 * GROUP_W
    last8 = t_all // 8 - 1
    kern = functools.partial(_gdn_prep_kernel, n_lat_tiles=n_lat_tiles, tps=tps, nab_half=nab_half)

    def rope_idx(i):
        return jnp.where(i < n_lat_tiles, i % tps, tps)

    return pl.pallas_call(
        kern,
        grid=(n_tiles,),
        in_specs=[
            pl.BlockSpec((tm, qkv_w), lambda i: (i, 1)),
            pl.BlockSpec((8, qkv_w), lambda i: (jnp.maximum(i * (tm // 8) - 1, 0), 1)),
            pl.BlockSpec((8, qkv_w), lambda i: (jnp.minimum((i + 1) * (tm // 8), last8), 1)),
            pl.BlockSpec((tm, 2 * nab_half), lambda i: (i, 0)),
            pl.BlockSpec((8, qkv_w), lambda i: (0, 0)),
            pl.BlockSpec((1, nab_half), lambda i: (0, 0)),
            pl.BlockSpec((1, nab_half), lambda i: (0, 0)),
            pl.BlockSpec((tm, HEAD_DIM), lambda i: (rope_idx(i), 0)),
            pl.BlockSpec((tm, HEAD_DIM), lambda i: (rope_idx(i), 0)),
        ],
        out_specs=[
            pl.BlockSpec((tm, GROUP_W), lambda i: (i, 0)),
            pl.BlockSpec((tm, GROUP_W), lambda i: (i, 0)),
            pl.BlockSpec((tm, GROUP_W), lambda i: (i, 0)),
            pl.BlockSpec((tm, nab_half), lambda i: (i, 0)),
            pl.BlockSpec((tm, nab_half), lambda i: (i, 0)),
        ],
        out_shape=[
            jax.ShapeDtypeStruct((t_all, GROUP_W), F32),
            jax.ShapeDtypeStruct((t_all, GROUP_W), F32),
            jax.ShapeDtypeStruct((t_all, GROUP_W), F32),
            jax.ShapeDtypeStruct((t_all, nab_half), F32),
            jax.ShapeDtypeStruct((t_all, nab_half), F32),
        ],
        scratch_shapes=[pltpu.VMEM((tm + 16, qkv_w), F32)],
        compiler_params=_cparams(("arbitrary",)),
        name="gdn_prep",
    )(p_all, p_all, p_all, pab, conv_w, alog, dtb, rope_c, rope_s)


def _unit_tri_inverse(ns, orders):
    c = GDN_CHUNK
    ds = []
    for n, (row, col) in zip(ns, orders):
        m1 = jnp.logical_and(jnp.logical_and((row >> 1) == (col >> 1), (row & 1) == 1), (col & 1) == 0)
        ds.append((row == col).astype(F32) - jnp.where(m1, n, 0.0))
    k = 2
    while k < c:
        sh = int(np.log2(2 * k))
        dsp, xs = [], []
        for n, d, (row, col) in zip(ns, ds, orders):
            mk = jnp.logical_and((row >> sh) == (col >> sh),
                                 jnp.logical_and((row & (2 * k - 1)) >= k, (col & (2 * k - 1)) < k))
            lh, ll = _split_bf16(jnp.where(mk, n, 0.0))
            dh, dl = _split_bf16(d)
            dsp.append((dh, dl))
            xs.append(_dot3_split(lh, ll, dh, dl))
        ds = [d - _dot3_split(dh, dl, *_split_bf16(x)) for d, (dh, dl), x in zip(ds, dsp, xs)]
        k *= 2
    return ds


def _gdn_scan_kernel(qf, kf, vf, gf, bf, qb, kb, vb, gb, bb, of_ref, ob_ref, st_ref, *, hb):
    s = pl.program_id(2)
    c = GDN_CHUNK

    @pl.when(s == 0)
    def _():
        st_ref[...] = jnp.zeros(st_ref.shape, F32)

    ri = lax.broadcasted_iota(jnp.int32, (c, c), 0)
    ci = lax.broadcasted_iota(jnp.int32, (c, c), 1)
    zpad = jnp.zeros((HEAD_DIM - c, HEAD_DIM), F32)
    ch = []
    for d, (q_ref, k_ref, v_ref, g_ref, be_ref, o_ref) in enumerate(
            ((qf, kf, vf, gf, bf, of_ref), (qb, kb, vb, gb, bb, ob_ref))):
        incl = (ri >= ci) if d == 0 else (ri <= ci)
        gc_all = jnp.dot(incl.astype(F32), g_ref[...], precision=HIGHEST, preferred_element_type=F32)
        gc_t = jnp.transpose(jnp.concatenate([gc_all, zpad], axis=0))
        beta_all = be_ref[...]
        for hh in range(hb):
            ln = d * hb + hh
            sl = slice(hh * HEAD_DIM, (hh + 1) * HEAD_DIM)
            ch.append(dict(
                ln=ln, sl=sl, o_ref=o_ref, incl=incl,
                strict=(ri > ci) if d == 0 else (ri < ci),
                order=(ri, ci) if d == 0 else (ci, ri),
                last=c - 1 if d == 0 else 0,
                gcc=gc_all[:, ln:ln + 1], grow=gc_t[ln:ln + 1, 0:c], beta=beta_all[:, ln:ln + 1],
                q=q_ref[:, sl], k=k_ref[:, sl], v=v_ref[:, sl]))
    for t in ch:
        t["decay"] = jnp.where(t["incl"], jnp.exp(jnp.where(t["incl"], t["gcc"] - t["grow"], 0.0)), 0.0)
        t["kbeta"] = t["k"] * t["beta"]
    for t in ch:
        t["kk"] = _dot_t(t["kbeta"], t["k"])
    for t in ch:
        t["qk"] = _dot_t(t["q"], t["k"]) * t["decay"]
    ns = [jnp.where(t["strict"], t["kk"] * t["decay"], 0.0) for t in ch]
    tinvs = _unit_tri_inverse(ns, [t["order"] for t in ch])
    for t, tinv in zip(ch, tinvs):
        eg = jnp.exp(t["gcc"])
        t["eg"] = eg
        rhs = jnp.concatenate([t["v"] * t["beta"], t["kbeta"] * eg], axis=1)
        t["sol"] = _dot3_split(*_split_bf16(tinv), *_split_bf16(rhs))
    for t in ch:
        t["state"] = st_ref[t["ln"]]
        t["v_new"] = t["sol"][:, :HEAD_DIM] - _dot(t["sol"][:, HEAD_DIM:], t["state"])
    for t in ch:
        t["o_ref"][:, t["sl"]] = _dot(t["q"] * t["eg"], t["state"]) + _dot(t["qk"], t["v_new"])
    for t in ch:
        glast = t["gcc"][t["last"]:t["last"] + 1, :]
        kd_t = jnp.transpose(jnp.concatenate([t["k"] * jnp.exp(glast - t["gcc"]), zpad], axis=0))[:, 0:c]
        st_ref[t["ln"]] = t["state"] * jnp.exp(glast) + _dot(kd_t, t["v_new"])


def _gdn_scan_call(qn, kn, vn, g, beta, *, nb, seq, ctx_len):
    t_all = qn.shape[0]
    hb = GDN_HB
    c = GDN_CHUNK
    ncc = ctx_len // c
    ncl = seq // c
    ctx0 = nb * ncl
    n_hg = N_HEADS // hb

    def row_f(b, s):
        return jnp.where(s < ncc, ctx0 + b * ncc + s, b * ncl + (s - ncc))

    def row_b(b, s):
        return jnp.where(s < ncc, ctx0 + b * ncc + (ncc - 1 - s), b * ncl + (ncl - 1 - (s - ncc)))

    def specs(rowfn):
        big = pl.BlockSpec((c, hb * HEAD_DIM), lambda b, hg, s: (rowfn(b, s), hg))
        small = pl.BlockSpec((c, HEAD_DIM), lambda b, hg, s: (rowfn(b, s), hg))
        return [big, big, big, small, small]

    out_f = pl.BlockSpec((c, hb * HEAD_DIM), lambda b, hg, s: (row_f(b, s), hg))
    out_b = pl.BlockSpec((c, hb * HEAD_DIM), lambda b, hg, s: (row_b(b, s), hg))
    return pl.pallas_call(
        functools.partial(_gdn_scan_kernel, hb=hb),
        grid=(nb, n_hg, ncc + ncl),
        in_specs=specs(row_f) + specs(row_b),
        out_specs=[out_f, out_b],
        out_shape=[jax.ShapeDtypeStruct((t_all, GROUP_W), F32)] * 2,
        scratch_shapes=[pltpu.VMEM((2 * hb, HEAD_DIM, HEAD_DIM), F32)],
        compiler_params=_cparams(("arbitrary", "arbitrary", "arbitrary")),
        name="gdn_scan",
    )(qn, kn, vn, g, beta, qn, kn, vn, g, beta)


def _outproj_kernel(nal_ref, nac_ref, of_ref, ob_ref, z_ref, x_ref, ga_ref, shf_ref, scf_ref, w_ref, ng_ref, lg_ref,
                    lb_ref, x1_ref, h2t_ref, gdn_scr, *, tiles_per_seq, nb):
    i = pl.program_id(0)
    r = jnp.minimum(i // tiles_per_seq, nb)
    na = jnp.where(i < nb * tiles_per_seq, nal_ref[...], nac_ref[...])
    o = of_ref[...] + ob_ref[...]
    ng = ng_ref[...]
    for h in range(N_HEADS):
        sl = slice(h * HEAD_DIM, (h + 1) * HEAD_DIM)
        oh = o[:, sl]
        oh = oh * lax.rsqrt(jnp.mean(oh * oh, axis=-1, keepdims=True) + 1e-6) * ng
        gdn_scr[:, sl] = (oh * _silu(z_ref[:, sl])).astype(BF16)
    mix = _dot(na.astype(BF16), w_ref[0:GROUP_W, :]) + _dot(gdn_scr[...], w_ref[GROUP_W:2 * GROUP_W, :])
    y = DEEPNORM_ALPHA * x_ref[...] + ga_ref[pl.ds(r, 1), :] * mix
    x1 = _layer_norm_rows(y, lg_ref[...], lb_ref[...])
    x1_ref[...] = x1
    h2 = x1 * (1.0 + scf_ref[pl.ds(r, 1), :]) + shf_ref[pl.ds(r, 1), :]
    h2t_ref[...] = jnp.transpose(h2).astype(BF16)


def _outproj_call(na_lat, na_ctx, o_f, o_b, p_all, xa, mod_l, w_out, norm_g, ln_g, ln_b, *, seq, nb, rows):
    t_all, d = rows, xa.shape[1]
    tm = TM_OUT
    n_lat_tiles = nb * seq // tm
    kern = functools.partial(_outproj_kernel, tiles_per_seq=seq // tm, nb=nb)
    row = lambda i: (i, 0)
    const = lambda i: (0, 0)
    return pl.pallas_call(
        kern,
        grid=(t_all // tm,),
        in_specs=[
            pl.BlockSpec((tm, GROUP_W), lambda i: (jnp.minimum(i, n_lat_tiles - 1), 0)),
            pl.BlockSpec((tm, GROUP_W), lambda i: (jnp.maximum(i - n_lat_tiles, 0), 0)),
            pl.BlockSpec((tm, GROUP_W), row),
            pl.BlockSpec((tm, GROUP_W), row),
            pl.BlockSpec((tm, GROUP_W), lambda i: (i, 6)),
            pl.BlockSpec((tm, d), row),
            pl.BlockSpec((8, d), lambda i: (0, 2)),
            pl.BlockSpec((8, d), lambda i: (0, 3)),
            pl.BlockSpec((8, d), lambda i: (0, 4)),
            pl.BlockSpec((d, d), const),
            pl.BlockSpec((1, HEAD_DIM), const),
            pl.BlockSpec((1, d), const),
            pl.BlockSpec((1, d), const),
        ],
        out_specs=[
            pl.BlockSpec((tm, d), row),
            pl.BlockSpec((d, tm), lambda i: (0, i)),
        ],
        out_shape=[
            jax.ShapeDtypeStruct((t_all, d), F32),
            jax.ShapeDtypeStruct((d, t_all), BF16),
        ],
        scratch_shapes=[pltpu.VMEM((tm, GROUP_W), BF16)],
        compiler_params=_cparams(("arbitrary",)),
        name="out_proj_ln",
    )(na_lat, na_ctx, o_f, o_b, p_all, xa, mod_l, mod_l, mod_l, w_out, norm_g, ln_g, ln_b)


def _dup_bf16_words(x):
    u = pltpu.bitcast(x.astype(BF16).astype(F32), jnp.uint32)
    return u | (u >> 16)


def _peer_score_kernel(ht_ref, w_ref, sk_ref, l_ref, ea_ref, rb_ref, eb_ref, s_scr, atop, btop, cand):
    tm = TM_PEER
    qt = _dot(w_ref[...], ht_ref[...])
    half = PEER_QDIM // 2
    s_scr[0] = _dot(sk_ref[0], qt[0:half])
    s_scr[1] = _dot(sk_ref[1], qt[half:2 * half])
    n_half = PEER_TOPK // 2
    for tc in range(tm // 128):
        ls = slice(tc * 128, (tc + 1) * 128)
        s0 = s_scr[0, :, ls]
        s1 = s_scr[1, :, ls]
        sc = s0
        for k in range(PEER_TOPK):
            m = jnp.max(sc, axis=0, keepdims=True)
            atop[k:k + 1, ls] = m
            sc = jnp.where(sc == m, -jnp.inf, sc)
        sc = s1
        rank = jnp.full(s1.shape, float(PEER_TOPK), F32)
        for k in range(PEER_TOPK):
            m = jnp.max(sc, axis=0, keepdims=True)
            btop[k:k + 1, ls] = m
            hit = sc == m
            rank = jnp.minimum(rank, jnp.where(hit, float(k), float(PEER_TOPK)))
            sc = jnp.where(hit, -jnp.inf, sc)
        bt = btop[:, ls]
        cand[0:PEER_TOPK, ls] = atop[0:1, ls] + bt
        for i in range(1, n_half):
            r0 = PEER_TOPK + (i - 1) * n_half
            cand[r0:r0 + n_half, ls] = atop[i:i + 1, ls] + bt[0:n_half, :]
        r0 = PEER_TOPK + (n_half - 1) * n_half
        cand[r0:r0 + n_half, ls] = atop[n_half:PEER_TOPK, ls] + bt[0:1, :]
        cv = cand[:, ls]
        m0 = jnp.max(cv, axis=0, keepdims=True)
        z = jnp.zeros_like(m0)
        tau = m0
        for k in range(PEER_TOPK):
            tau = jnp.max(cv, axis=0, keepdims=True)
            z = z + jnp.exp(tau - m0)
            cv = jnp.where(cv == tau, -jnp.inf, cv)
        row = lambda k: bt[k:k + 1, :]
        ge = lambda v: (s0 + v) >= tau
        c8 = ge(row(7))
        c4 = ge(jnp.where(c8, row(11), row(3)))
        c2 = ge(jnp.where(c8, jnp.where(c4, row(13), row(9)), jnp.where(c4, row(5), row(1))))
        c1 = ge(jnp.where(c8,
                          jnp.where(c4, jnp.where(c2, row(14), row(12)), jnp.where(c2, row(10), row(8))),
                          jnp.where(c4, jnp.where(c2, row(6), row(4)), jnp.where(c2, row(2), row(0)))))
        cnt = (jnp.where(c8, 8.0, 0.0) + jnp.where(c4, 4.0, 0.0) + jnp.where(c2, 2.0, 0.0)
               + jnp.where(c1, 1.0, 0.0) + jnp.where(ge(row(15)), 1.0, 0.0))
        l_ref[0, :, ls] = _dup_bf16_words(cnt)
        ea_ref[0, :, ls] = _dup_bf16_words(jnp.exp(s0 - atop[0:1, ls]) * (1.0 / z))
        rb_ref[0, :, ls] = rank.astype(BF16)
        eb_ref[0, :, ls] = jnp.exp(s1 - bt[0:1, :]).astype(BF16)


def _peer_score_call(h2t, wq_t, subkeys):
    d, t_all = h2t.shape
    tm = TM_PEER
    ncand = PEER_TOPK + (PEER_TOPK // 2) ** 2
    tok = lambda i, h: (h, 0, i)
    return pl.pallas_call(
        _peer_score_kernel,
        grid=(t_all // tm, PEER_HEADS),
        in_specs=[
            pl.BlockSpec((d, tm), lambda i, h: (0, i)),
            pl.BlockSpec((PEER_QDIM, d), lambda i, h: (h, 0)),
            pl.BlockSpec((2, PEER_NKEYS, PEER_QDIM // 2), lambda i, h: (0, 0, 0)),
        ],
        out_specs=[pl.BlockSpec((1, PEER_NKEYS, tm), tok)] * 4,
        out_shape=[
            jax.ShapeDtypeStruct((PEER_HEADS, PEER_NKEYS, t_all), jnp.uint32),
            jax.ShapeDtypeStruct((PEER_HEADS, PEER_NKEYS, t_all), jnp.uint32),
            jax.ShapeDtypeStruct((PEER_HEADS, PEER_NKEYS, t_all), BF16),
            jax.ShapeDtypeStruct((PEER_HEADS, PEER_NKEYS, t_all), BF16),
        ],
        scratch_shapes=[
            pltpu.VMEM((2, PEER_NKEYS, tm), F32),
            pltpu.VMEM((PEER_TOPK, tm), F32),
            pltpu.VMEM((PEER_TOPK, tm), F32),
            pltpu.VMEM((ncand, tm), F32),
        ],
        compiler_params=_cparams(("arbitrary", "arbitrary")),
        name="peer_scores",
    )(h2t, wq_t, subkeys)


def _peer_dense_kernel(h_ref, u_ref, vt_ref, l_ref, ea_ref, rb_ref, eb_ref, o_ref, act0, act1, w0, w1):
    m = pl.program_id(1)
    n_tiles = pl.num_programs(1) - 2
    tm = TM_PEER
    nk = PEER_NKEYS
    rows_per_step = TE_PEER // nk
    jq_rows = 32
    n_jq = nk // jq_rows

    @pl.when(m == 0)
    def _():
        o_ref[...] = jnp.zeros(o_ref.shape, F32)
        for r in (act0, act1):
            r[...] = jnp.zeros(r.shape, F32)
        for r in (w0, w1):
            r[...] = jnp.zeros(r.shape, BF16)

    valid_b = jnp.logical_and(m >= 1, m <= n_tiles)
    sqrt_half = np.float32(np.sqrt(0.5))

    def stages(act_a, w_c, act_b, w_b):
        n_tc = tm // 128

        def stage_a(n, r):
            cs = slice(n * 256, (n + 1) * 256)
            rs = slice(r * 128, (r + 1) * 128)
            act_a[rs, cs] = _dot(u_ref[rs, :], h_ref[:, cs])

        def stage_c(q):
            rs = slice(q * 128, (q + 1) * 128)
            o_ref[rs, :] += _dot(vt_ref[rs, :], w_c[...])

        def stage_b(tc, jq):
            ls = slice(tc * 128, (tc + 1) * 128)
            js = slice(jq * (jq_rows // 16), (jq + 1) * (jq_rows // 16))
            g = [jnp.zeros((jq_rows // 16, 16, 128), BF16) for _ in range(rows_per_step)]
            for h in range(PEER_HEADS):
                rk = rb_ref[h, js, :, ls]
                ee = eb_ref[h, js, :, ls]
                for ii in range(rows_per_step):
                    l_b = pltpu.bitcast(jnp.broadcast_to(l_ref[h, ii:ii + 1, ls], (8, 128)), BF16)
                    e_b = pltpu.bitcast(jnp.broadcast_to(ea_ref[h, ii:ii + 1, ls], (8, 128)), BF16)
                    g[ii] = g[ii] + jnp.where(rk < l_b, ee * e_b, jnp.zeros((), BF16))
            for ii in range(rows_per_step):
                r0 = ii * nk + jq * jq_rows
                xa = act_b[r0:r0 + jq_rows, ls]
                ge = (0.5 * xa * (1.0 + lax.erf(xa * sqrt_half))).astype(BF16)
                wv = jnp.where(valid_b, ge * g[ii].reshape(jq_rows, 128), jnp.zeros((), BF16))
                w_b[r0:r0 + jq_rows, ls] = wv

        mxu = []
        for q in range(8):
            mxu += [functools.partial(stage_a, q // 4, q % 4), functools.partial(stage_c, 2 * q),
                    functools.partial(stage_c, 2 * q + 1)]
        blocks = [(tc, jq) for tc in range(n_tc) for jq in range(n_jq)]
        mi = 0
        for bi, (tc, jq) in enumerate(blocks):
            while mi < len(mxu) and mi * len(blocks) <= bi * len(mxu):
                mxu[mi]()
                mi += 1
            stage_b(tc, jq)
        while mi < len(mxu):
            mxu[mi]()
            mi += 1

    @pl.when(m % 2 == 0)
    def _():
        stages(act0, w0, act1, w1)

    @pl.when(m % 2 == 1)
    def _():
        stages(act1, w1, act0, w0)


def _peer_dense_call(hb, u_bf, vt_bf, l_pk, ea_pk, rb, eb):
    d, t_all = hb.shape
    ne = u_bf.shape[0]
    tm, te = TM_PEER, TE_PEER
    n_tiles = ne // te
    rows = te // PEER_NKEYS
    row_tile = lambda i, m: (0, jnp.clip(m - 1, 0, n_tiles - 1), 0, i)
    all_j = lambda i, m: (0, 0, 0, i)
    l4 = l_pk.reshape(PEER_HEADS, n_tiles, rows, t_all)
    ea4 = ea_pk.reshape(PEER_HEADS, n_tiles, rows, t_all)
    rb4 = rb.reshape(PEER_HEADS, PEER_NKEYS // 16, 16, t_all)
    eb4 = eb.reshape(PEER_HEADS, PEER_NKEYS // 16, 16, t_all)
    return pl.pallas_call(
        _peer_dense_kernel,
        grid=(t_all // tm, n_tiles + 2),
        in_specs=[
            pl.BlockSpec((d, tm), lambda i, m: (0, i)),
            pl.BlockSpec((te, d), lambda i, m: (jnp.minimum(m, n_tiles - 1), 0)),
            pl.BlockSpec((d, te), lambda i, m: (0, jnp.maximum(m - 2, 0))),
            pl.BlockSpec((PEER_HEADS, None, rows, tm), row_tile),
            pl.BlockSpec((PEER_HEADS, None, rows, tm), row_tile),
            pl.BlockSpec((PEER_HEADS, PEER_NKEYS // 16, 16, tm), all_j),
            pl.BlockSpec((PEER_HEADS, PEER_NKEYS // 16, 16, tm), all_j),
        ],
        out_specs=pl.BlockSpec((d, tm), lambda i, m: (0, i)),
        out_shape=jax.ShapeDtypeStruct((d, t_all), F32),
        scratch_shapes=[
            pltpu.VMEM((te, tm), F32),
            pltpu.VMEM((te, tm), F32),
            pltpu.VMEM((te, tm), BF16),
            pltpu.VMEM((te, tm), BF16),
        ],
        compiler_params=_cparams(("arbitrary", "arbitrary")),
        name="peer_dense",
    )(hb, u_bf, vt_bf, l4, ea4, rb4, eb4)


def _peer_out_kernel(ft_ref, x1_ref, gf_ref, lg_ref, lb_ref, o_ref, *, tiles_per_seq, nb):
    i = pl.program_id(0)
    r = jnp.minimum(i // tiles_per_seq, nb)
    y = DEEPNORM_ALPHA * x1_ref[...] + gf_ref[pl.ds(r, 1), :] * jnp.transpose(ft_ref[...])
    o_ref[...] = _layer_norm_rows(y, lg_ref[...], lb_ref[...])


def _peer_out_call(ffn_t, x1, mod_l, ln_g, ln_b, *, seq, nb):
    t_all, d = x1.shape
    tm = TM_OUT
    kern = functools.partial(_peer_out_kernel, tiles_per_seq=seq // tm, nb=nb)
    return pl.pallas_call(
        kern,
        grid=(t_all // tm,),
        in_specs=[
            pl.BlockSpec((d, tm), lambda i: (0, i)),
            pl.BlockSpec((tm, d), lambda i: (i, 0)),
            pl.BlockSpec((8, d), lambda i: (0, 5)),
            pl.BlockSpec((1, d), lambda i: (0, 0)),
            pl.BlockSpec((1, d), lambda i: (0, 0)),
        ],
        out_specs=pl.BlockSpec((tm, d), lambda i: (i, 0)),
        out_shape=jax.ShapeDtypeStruct((t_all, d), F32),
        compiler_params=_cparams(("arbitrary",)),
        name="peer_out_ln",
    )(ffn_t, x1, mod_l, ln_g, ln_b)


def _ab_columns(w_tail):
    depth, d, _ = w_tail.shape
    hb = GDN_HB
    n_hg = N_HEADS // hb
    wa = w_tail[:, :, :2 * N_HEADS].reshape(depth, d, 2, n_hg, hb)
    wb = w_tail[:, :, 2 * N_HEADS:].reshape(depth, d, 2, n_hg, hb)

    def lay(w):
        w = w.transpose(0, 1, 3, 2, 4).reshape(depth, d, n_hg, 2 * hb)
        w = jnp.pad(w, ((0, 0), (0, 0), (0, 0), (0, HEAD_DIM - 2 * hb)))
        return w.reshape(depth, d, n_hg * HEAD_DIM)

    return jnp.concatenate([lay(wa), lay(wb)], axis=-1)


def _head_param_lanes(p):
    depth = p.shape[0]
    hb = GDN_HB
    n_hg = N_HEADS // hb
    p = p.reshape(depth, 2, n_hg, hb).transpose(0, 2, 1, 3).reshape(depth, n_hg, 2 * hb)
    p = jnp.pad(p, ((0, 0), (0, 0), (0, HEAD_DIM - 2 * hb)))
    return p.reshape(depth, 1, n_hg * HEAD_DIM)


def kernel(x, c, ctx, c_ctx, w_mod, b_mod, w_in, w_out, na_rpb, gdn_conv, gdn_a_log, gdn_dt_bias, gdn_norm_g,
           peer_wq, peer_subkeys, peer_u, peer_v, ln_g, ln_b):
    nb, seq, d = x.shape
    ctx_len = ctx.shape[1]
    depth = w_mod.shape[0]
    assert d == D_MODEL and nb + 1 <= 8
    assert seq % (NA_QROWS * GRID_W) == 0 and seq % TM_PROJ == 0
    t_lat = nb * seq
    dims = dict(nb=nb, seq=seq, ctx_len=ctx_len)

    xa = jnp.concatenate([x.reshape(t_lat, d), ctx.reshape(nb * ctx_len, d)], axis=0)
    cs = jnp.concatenate([c, c_ctx[None, :], jnp.zeros((8 - nb - 1, d), F32)], axis=0)
    mod = _mod_call(cs, w_mod, b_mod)

    w_main = w_in[:, :, :P_MAIN_W].astype(BF16)
    w_ab = _ab_columns(w_in[:, :, P_MAIN_W:]).astype(BF16)
    w_out_bf = w_out.astype(BF16)
    alog = _head_param_lanes(gdn_a_log)
    dtb = _head_param_lanes(gdn_dt_bias)
    conv_w = jnp.pad(gdn_conv, ((0, 0), (0, 8 - GDN_CONV), (0, 0)))
    rope_c, rope_s = _rope_tables(seq, ctx_len)
    wq_t = peer_wq.astype(BF16).transpose(0, 2, 1)
    u_bf = peer_u.astype(BF16)
    vt_bf = peer_v.astype(BF16).transpose(0, 2, 1)

    for l in range(depth):
        p_all, pab = _inproj_call(xa, mod[l], w_main[l], w_ab[l], seq=seq, nb=nb)
        bias = _na_bias(na_rpb[l], seq // GRID_W)
        na_lat = _na_call(p_all, bias, **dims)
        na_ctx = _ctx_attn_call(p_all, **dims)
        qn, kn, vn, g, beta = _gdn_prep_call(p_all, pab, conv_w[l], alog[l], dtb[l], rope_c, rope_s, **dims)
        o_f, o_b = _gdn_scan_call(qn, kn, vn, g, beta, **dims)
        x1, h2t = _outproj_call(na_lat, na_ctx, o_f, o_b, p_all, xa, mod[l], w_out_bf[l], gdn_norm_g[l][None, :],
                                ln_g[l, 0][None, :], ln_b[l, 0][None, :], seq=seq, nb=nb,
                                rows=xa.shape[0] if l < depth - 1 else t_lat)
        l_pk, ea_pk, rb, eb = _peer_score_call(h2t, wq_t[l], peer_subkeys[l])
        ffn_t = _peer_dense_call(h2t, u_bf[l], vt_bf[l], l_pk, ea_pk, rb, eb)
        xa = _peer_out_call(ffn_t, x1, mod[l], ln_g[l, 1][None, :], ln_b[l, 1][None, :], seq=seq, nb=nb)
    return xa.reshape(nb, seq, d)
```

```python
import functools

import numpy as np
import jax
import jax.numpy as jnp
from jax import lax
from jax.experimental import pallas as pl
from jax.experimental.pallas import tpu as pltpu

F32 = jnp.float32
BF16 = jnp.bfloat16
HIGHEST = lax.Precision.HIGHEST

D_MODEL = 2048
HEAD_DIM = 128
N_HEADS = 8
GROUP_W = N_HEADS * HEAD_DIM
GRID_W = 64
NA_KR = 8
NA_KC = 16
NA_QROWS = 8
NA_KROWS = 16
NA_SUB = 2
GDN_CHUNK = 64
GDN_CONV = 5
GDN_HB = 8
ROPE_THETA = 10000.0
PEER_HEADS = 8
PEER_TOPK = 16
PEER_NKEYS = 128
PEER_QDIM = 256
N_MOD = 6
LN_EPS = 1e-6
NEG_INF = -1e30
DEPTH_FOR_DEEPNORM = 4
DEEPNORM_ALPHA = (2 * DEPTH_FOR_DEEPNORM) ** 0.25
P_MAIN_W = 7 * GROUP_W
VMEM_LIMIT = 56 * 1024 * 1024

TM_PROJ = 512
TN_PROJ = 1792
TM_OUT = 256
TM_PREP = 256
TM_PEER = 512
TE_PEER = 512


def _cparams(sem):
    return pltpu.CompilerParams(dimension_semantics=sem, vmem_limit_bytes=VMEM_LIMIT)


def _sigmoid(x):
    return 1.0 / (1.0 + jnp.exp(-x))


def _silu(x):
    return x * _sigmoid(x)


def _dot(a, b):
    return jnp.dot(a, b, preferred_element_type=F32)


def _dot_t(a, b):
    return lax.dot_general(a, b, (((1,), (1,)), ((), ())), preferred_element_type=F32)


def _split_bf16(a):
    hi = a.astype(BF16)
    lo = (a - hi.astype(F32)).astype(BF16)
    return hi, lo


def _dot3_split(ah, al, bh, bl):
    return _dot(ah, bh) + (_dot(ah, bl) + _dot(al, bh))


def _layer_norm_rows(y, g, b):
    mu = jnp.mean(y, axis=-1, keepdims=True)
    yc = y - mu
    var = jnp.mean(yc * yc, axis=-1, keepdims=True)
    return yc * lax.rsqrt(var + LN_EPS) * g + b


def _mod_kernel(c_ref, w_ref, b_ref, o_ref):
    s = _silu(c_ref[...])
    o_ref[0] = jnp.dot(s, w_ref[0], precision=HIGHEST, preferred_element_type=F32) + b_ref[0]


def _mod_call(cs, w_mod, b_mod):
    depth, d, n = w_mod.shape
    tn = 1024
    return pl.pallas_call(
        _mod_kernel,
        grid=(depth, n // tn),
        in_specs=[
            pl.BlockSpec((8, d), lambda l, j: (0, 0)),
            pl.BlockSpec((1, d, tn), lambda l, j: (l, 0, j)),
            pl.BlockSpec((1, 1, tn), lambda l, j: (l, 0, j)),
        ],
        out_specs=pl.BlockSpec((1, 8, tn), lambda l, j: (l, 0, j)),
        out_shape=jax.ShapeDtypeStruct((depth, 8, n), F32),
        compiler_params=_cparams(("arbitrary", "arbitrary")),
        name="adaln_mod",
    )(cs, w_mod, b_mod.reshape(depth, 1, n))


def _inproj_kernel(x_ref, sh_ref, sc_ref, w_ref, wab_ref, p_ref, pab_ref, h_scr, *, tiles_per_seq, nb):
    i = pl.program_id(0)
    j = pl.program_id(1)

    @pl.when(j == 0)
    def _():
        r = jnp.minimum(i // tiles_per_seq, nb)
        sh = sh_ref[pl.ds(r, 1), :]
        sc = sc_ref[pl.ds(r, 1), :]
        hb = (x_ref[...] * (1.0 + sc) + sh).astype(BF16)
        h_scr[...] = hb
        pab_ref[...] = _dot(hb, wab_ref[...])

    p_ref[...] = _dot(h_scr[...], w_ref[...])


def _inproj_call(xa, mod_l, w_main, w_ab, *, seq, nb):
    t_all, d = xa.shape
    n = w_main.shape[1]
    nab = w_ab.shape[1]
    tm, tn = TM_PROJ, TN_PROJ
    kern = functools.partial(_inproj_kernel, tiles_per_seq=seq // tm, nb=nb)
    return pl.pallas_call(
        kern,
        grid=(t_all // tm, n // tn),
        in_specs=[
            pl.BlockSpec((tm, d), lambda i, j: (i, 0)),
            pl.BlockSpec((8, d), lambda i, j: (0, 0)),
            pl.BlockSpec((8, d), lambda i, j: (0, 1)),
            pl.BlockSpec((d, tn), lambda i, j: (0, j)),
            pl.BlockSpec((d, nab), lambda i, j: (0, 0)),
        ],
        out_specs=[
            pl.BlockSpec((tm, tn), lambda i, j: (i, j)),
            pl.BlockSpec((tm, nab), lambda i, j: (i, 0)),
        ],
        out_shape=[
            jax.ShapeDtypeStruct((t_all, n), F32),
            jax.ShapeDtypeStruct((t_all, nab), F32),
        ],
        scratch_shapes=[pltpu.VMEM((tm, d), BF16)],
        compiler_params=_cparams(("arbitrary", "arbitrary")),
        name="in_proj",
    )(xa, mod_l, mod_l, w_main, w_ab)


def _na_bias_index_tables(nrows):
    rq, rk = NA_QROWS, NA_KROWS
    big = 1 << 20
    cfg = [(0, 0, nrows), (8, 4, big), (nrows - rq, nrows - rk, nrows)]
    dr = np.zeros((3, rq, rk), np.int32)
    rv = np.zeros((3, rq, rk), bool)
    for v, (r0, start, nr) in enumerate(cfg):
        r = r0 + np.arange(rq)[:, None]
        kr = start + np.arange(rk)[None, :]
        rs = np.clip(r - NA_KR // 2, 0, nr - NA_KR)
        rv[v] = (kr >= rs) & (kr < rs + NA_KR)
        dr[v] = np.clip(kr - r + NA_KR - 1, 0, 2 * NA_KR - 2)
    qc = np.arange(GRID_W)[:, None]
    kc = np.arange(GRID_W)[None, :]
    ws = np.clip(qc - NA_KC // 2, 0, GRID_W - NA_KC)
    cv = (kc >= ws) & (kc < ws + NA_KC)
    dc = np.clip(kc - qc + NA_KC - 1, 0, 2 * NA_KC - 2).astype(np.int32)
    return dr, rv, dc, cv


def _na_bias(rpb_l, nrows):
    dr, rv, dc, cv = _na_bias_index_tables(nrows)
    tc = jnp.where(cv[None, None], rpb_l[:, :, dc], NEG_INF)
    n_dr = tc.shape[1]
    tc = jnp.concatenate([tc, jnp.full((N_HEADS, 1, GRID_W, GRID_W), NEG_INF, F32)], axis=1)
    blk = tc[:, np.where(rv, dr, n_dr)]
    blk = blk.transpose(0, 1, 2, 4, 3, 5)
    return blk.reshape(N_HEADS, 3, NA_QROWS * GRID_W, NA_KROWS * GRID_W)


def _na_kernel(q_ref, k0, k1, k2, k3, v0, v1, v2, v3, kc_ref, vc_ref, bias_ref, o_ref):
    kb = 4 * GRID_W
    ks = [kr[...].astype(BF16) for kr in (k0, k1, k2, k3, kc_ref)]
    vs = [vr[...].astype(BF16) for vr in (v0, v1, v2, v3, vc_ref)]
    rows = q_ref.shape[0] // NA_SUB
    for sub in range(NA_SUB):
        rs = slice(sub * rows, (sub + 1) * rows)
        q = (q_ref[rs, :] * (HEAD_DIM ** -0.5)).astype(BF16)
        s = [_dot_t(q, ks[j]) + bias_ref[rs, j * kb:(j + 1) * kb] for j in range(4)]
        s.append(_dot_t(q, ks[4]))
        m = s[0].max(axis=-1, keepdims=True)
        for t in s[1:]:
            m = jnp.maximum(m, t.max(axis=-1, keepdims=True))
        p = [jnp.exp(t - m) for t in s]
        l = p[0].sum(axis=-1, keepdims=True)
        for t in p[1:]:
            l = l + t.sum(axis=-1, keepdims=True)
        o = _dot(p[0].astype(BF16), vs[0])
        for t, vv in zip(p[1:], vs[1:]):
            o = o + _dot(t.astype(BF16), vv)
        o_ref[rs, :] = o / l


def _na_call(p_all, bias, *, nb, seq, ctx_len):
    t_all = p_all.shape[0]
    nrows = seq // GRID_W
    nrb = nrows // NA_QROWS
    tq = NA_QROWS * GRID_W
    tk = 4 * GRID_W
    assert ctx_len == tk and nrows >= NA_KROWS
    kblocks_per_seq = seq // tk
    max_sb = kblocks_per_seq - 4
    ctx_blk0 = nb * seq // tk

    def sb(r):
        return jnp.clip(2 * r - 1, 0, max_sb)

    def kspec(j, colbase):
        return pl.BlockSpec((tk, HEAD_DIM), lambda h, b, r: (b * kblocks_per_seq + sb(r) + j, colbase + h))

    def variant(r):
        return jnp.where(r == 0, 0, jnp.where(r == nrb - 1, 2, 1))

    in_specs = [pl.BlockSpec((tq, HEAD_DIM), lambda h, b, r: (b * nrb + r, h))]
    in_specs += [kspec(j, N_HEADS) for j in range(4)]
    in_specs += [kspec(j, 2 * N_HEADS) for j in range(4)]
    in_specs += [
        pl.BlockSpec((tk, HEAD_DIM), lambda h, b, r: (ctx_blk0 + b, N_HEADS + h)),
        pl.BlockSpec((tk, HEAD_DIM), lambda h, b, r: (ctx_blk0 + b, 2 * N_HEADS + h)),
        pl.BlockSpec((None, None, tq, NA_KROWS * GRID_W), lambda h, b, r: (h, variant(r), 0, 0)),
    ]
    return pl.pallas_call(
        _na_kernel,
        grid=(N_HEADS, nb, nrb),
        in_specs=in_specs,
        out_specs=pl.BlockSpec((tq, HEAD_DIM), lambda h, b, r: (b * nrb + r, h)),
        out_shape=jax.ShapeDtypeStruct((nb * seq, GROUP_W), F32),
        compiler_params=_cparams(("arbitrary", "arbitrary", "arbitrary")),
        name="na_attention",
    )(*([p_all] * 11), bias)


def _ctx_attn_kernel(q_ref, k_ref, v_ref, o_ref):
    q = q_ref[...].astype(BF16)
    s = _dot_t(q, k_ref[...].astype(BF16)) * (HEAD_DIM ** -0.5)
    m = s.max(axis=-1, keepdims=True)
    p = jnp.exp(s - m)
    l = p.sum(axis=-1, keepdims=True)
    o_ref[...] = _dot(p.astype(BF16), v_ref[...].astype(BF16)) / l


def _ctx_attn_call(p_all, *, nb, seq, ctx_len):
    blk0 = nb * seq // ctx_len
    return pl.pallas_call(
        _ctx_attn_kernel,
        grid=(nb, N_HEADS),
        in_specs=[
            pl.BlockSpec((ctx_len, HEAD_DIM), lambda b, h: (blk0 + b, h)),
            pl.BlockSpec((ctx_len, HEAD_DIM), lambda b, h: (blk0 + b, N_HEADS + h)),
            pl.BlockSpec((ctx_len, HEAD_DIM), lambda b, h: (blk0 + b, 2 * N_HEADS + h)),
        ],
        out_specs=pl.BlockSpec((ctx_len, HEAD_DIM), lambda b, h: (b, h)),
        out_shape=jax.ShapeDtypeStruct((nb * ctx_len, GROUP_W), F32),
        compiler_params=_cparams(("arbitrary", "arbitrary")),
        name="ctx_attention",
    )(p_all, p_all, p_all)


def _rope_tables(seq, ctx_len):
    t = np.arange(seq)
    row = (t // GRID_W).astype(np.float32)
    col = (t % GRID_W).astype(np.float32)
    n_freq = HEAD_DIM // 4
    inv = (ROPE_THETA ** (-np.arange(n_freq, dtype=np.float32) / n_freq)).astype(np.float32)
    ang = jnp.stack([jnp.asarray(row)[:, None] * inv, jnp.asarray(col)[:, None] * inv], axis=1)
    cos, sin = jnp.cos(ang), jnp.sin(ang)
    c = jnp.concatenate([cos, cos], axis=-1).reshape(seq, HEAD_DIM)
    s = jnp.concatenate([-sin, sin], axis=-1).reshape(seq, HEAD_DIM)
    c = jnp.concatenate([c, jnp.ones((ctx_len, HEAD_DIM), F32)], axis=0)
    s = jnp.concatenate([s, jnp.zeros((ctx_len, HEAD_DIM), F32)], axis=0)
    return c, s


def _gdn_prep_kernel(cur_ref, prev_ref, next_ref, pab_ref, cw_ref, alog_ref, dtb_ref, rc_ref, rs_ref,
                     q_ref, k_ref, v_ref, g_ref, b_ref, ext_ref, *, n_lat_tiles, tps, nab_half):
    i = pl.program_id(0)
    tm = TM_PREP
    is_ctx = i >= n_lat_tiles
    first = jnp.logical_or(is_ctx, i % tps == 0)
    last = jnp.logical_or(is_ctx, i % tps == tps - 1)
    ext_ref[8:8 + tm, :] = cur_ref[...]
    ext_ref[0:8, :] = jnp.where(first, 0.0, prev_ref[...])
    ext_ref[8 + tm:16 + tm, :] = jnp.where(last, 0.0, next_ref[...])

    lane = lax.broadcasted_iota(jnp.int32, (tm, HEAD_DIM), 1)
    half0 = (lane % (HEAD_DIM // 2)) < (HEAD_DIM // 4)
    rc = rc_ref[...]
    rs = rs_ref[...]
    base = 8 - GDN_CONV // 2
    outs = (q_ref, k_ref, v_ref)
    for part in range(3):
        for h in range(N_HEADS):
            c0 = part * GROUP_W + h * HEAD_DIM
            acc = cw_ref[0:1, c0:c0 + HEAD_DIM] * ext_ref[base:base + tm, c0:c0 + HEAD_DIM]
            for t in range(1, GDN_CONV):
                acc = acc + cw_ref[t:t + 1, c0:c0 + HEAD_DIM] * ext_ref[base + t:base + t + tm, c0:c0 + HEAD_DIM]
            y = _silu(acc)
            if part < 2:
                y = y * lax.rsqrt(jnp.sum(y * y, axis=-1, keepdims=True) + 1e-6)
                partner = jnp.where(half0, pltpu.roll(y, HEAD_DIM - HEAD_DIM // 4, 1),
                                    pltpu.roll(y, HEAD_DIM // 4, 1))
                y = y * rc + partner * rs
                if part == 0:
                    y = y * (HEAD_DIM ** -0.5)
            outs[part][:, h * HEAD_DIM:(h + 1) * HEAD_DIM] = y

    a = pab_ref[:, 0:nab_half] + dtb_ref[...]
    softplus = jnp.maximum(a, 0.0) + jnp.log1p(jnp.exp(-jnp.abs(a)))
    g_ref[...] = -jnp.exp(alog_ref[...]) * softplus
    b_ref[...] = _sigmoid(pab_ref[:, nab_half:2 * nab_half])


def _gdn_prep_call(p_all, pab, conv_w, alog, dtb, rope_c, rope_s, *, nb, seq, ctx_len):
    t_all = p_all.shape[0]
    tm = TM_PREP
    assert ctx_len == tm
    n_tiles = t_all // tm
    n_lat_tiles = nb * seq // tm
    tps = seq // tm
    nab_half = pab.shape[1] // 2
    qkv_w = 3 * GROUP_W
    last8 = t_all // 8 - 1
    kern = functools.partial(_gdn_prep_kernel, n_lat_tiles=n_lat_tiles, tps=tps, nab_half=nab_half)

    def rope_idx(i):
        return jnp.where(i < n_lat_tiles, i % tps, tps)

    return pl.pallas_call(
        kern,
        grid=(n_tiles,),
        in_specs=[
            pl.BlockSpec((tm, qkv_w), lambda i: (i, 1)),
            pl.BlockSpec((8, qkv_w), lambda i: (jnp.maximum(i * (tm // 8) - 1, 0), 1)),
            pl.BlockSpec((8, qkv_w), lambda i: (jnp.minimum((i + 1) * (tm // 8), last8), 1)),
            pl.BlockSpec((tm, 2 * nab_half), lambda i: (i, 0)),
            pl.BlockSpec((8, qkv_w), lambda i: (0, 0)),
            pl.BlockSpec((1, nab_half), lambda i: (0, 0)),
            pl.BlockSpec((1, nab_half), lambda i: (0, 0)),
            pl.BlockSpec((tm, HEAD_DIM), lambda i: (rope_idx(i), 0)),
            pl.BlockSpec((tm, HEAD_DIM), lambda i: (rope_idx(i), 0)),
        ],
        out_specs=[
            pl.BlockSpec((tm, GROUP_W), lambda i: (i, 0)),
            pl.BlockSpec((tm, GROUP_W), lambda i: (i, 0)),
            pl.BlockSpec((tm, GROUP_W), lambda i: (i, 0)),
            pl.BlockSpec((tm, nab_half), lambda i: (i, 0)),
            pl.BlockSpec((tm, nab_half), lambda i: (i, 0)),
        ],
        out_shape=[
            jax.ShapeDtypeStruct((t_all, GROUP_W), F32),
            jax.ShapeDtypeStruct((t_all, GROUP_W), F32),
            jax.ShapeDtypeStruct((t_all, GROUP_W), F32),
            jax.ShapeDtypeStruct((t_all, nab_half), F32),
            jax.ShapeDtypeStruct((t_all, nab_half), F32),
        ],
        scratch_shapes=[pltpu.VMEM((tm + 16, qkv_w), F32)],
        compiler_params=_cparams(("arbitrary",)),
        name="gdn_prep",
    )(p_all, p_all, p_all, pab, conv_w, alog, dtb, rope_c, rope_s)


def _unit_tri_inverse(ns, orders):
    c = GDN_CHUNK
    ds = []
    for n, (row, col) in zip(ns, orders):
        m1 = jnp.logical_and(jnp.logical_and((row >> 1) == (col >> 1), (row & 1) == 1), (col & 1) == 0)
        ds.append((row == col).astype(F32) - jnp.where(m1, n, 0.0))
    k = 2
    while k < c:
        sh = int(np.log2(2 * k))
        dsp, xs = [], []
        for n, d, (row, col) in zip(ns, ds, orders):
            mk = jnp.logical_and((row >> sh) == (col >> sh),
                                 jnp.logical_and((row & (2 * k - 1)) >= k, (col & (2 * k - 1)) < k))
            lh, ll = _split_bf16(jnp.where(mk, n, 0.0))
            dh, dl = _split_bf16(d)
            dsp.append((dh, dl))
            xs.append(_dot3_split(lh, ll, dh, dl))
        ds = [d - _dot3_split(dh, dl, *_split_bf16(x)) for d, (dh, dl), x in zip(ds, dsp, xs)]
        k *= 2
    return ds


def _gdn_scan_kernel(qf, kf, vf, gf, bf, qb, kb, vb, gb, bb, of_ref, ob_ref, st_ref, *, hb):
    s = pl.program_id(2)
    c = GDN_CHUNK

    @pl.when(s == 0)
    def _():
        st_ref[...] = jnp.zeros(st_ref.shape, F32)

    ri = lax.broadcasted_iota(jnp.int32, (c, c), 0)
    ci = lax.broadcasted_iota(jnp.int32, (c, c), 1)
    zpad = jnp.zeros((HEAD_DIM - c, HEAD_DIM), F32)
    ch = []
    for d, (q_ref, k_ref, v_ref, g_ref, be_ref, o_ref) in enumerate(
            ((qf, kf, vf, gf, bf, of_ref), (qb, kb, vb, gb, bb, ob_ref))):
        incl = (ri >= ci) if d == 0 else (ri <= ci)
        gc_all = jnp.dot(incl.astype(F32), g_ref[...], precision=HIGHEST, preferred_element_type=F32)
        gc_t = jnp.transpose(jnp.concatenate([gc_all, zpad], axis=0))
        beta_all = be_ref[...]
        for hh in range(hb):
            ln = d * hb + hh
            sl = slice(hh * HEAD_DIM, (hh + 1) * HEAD_DIM)
            ch.append(dict(
                ln=ln, sl=sl, o_ref=o_ref, incl=incl,
                strict=(ri > ci) if d == 0 else (ri < ci),
                order=(ri, ci) if d == 0 else (ci, ri),
                last=c - 1 if d == 0 else 0,
                gcc=gc_all[:, ln:ln + 1], grow=gc_t[ln:ln + 1, 0:c], beta=beta_all[:, ln:ln + 1],
                q=q_ref[:, sl], k=k_ref[:, sl], v=v_ref[:, sl]))
    for t in ch:
        t["decay"] = jnp.where(t["incl"], jnp.exp(jnp.where(t["incl"], t["gcc"] - t["grow"], 0.0)), 0.0)
        t["kbeta"] = t["k"] * t["beta"]
    for t in ch:
        t["kk"] = _dot_t(t["kbeta"], t["k"])
    for t in ch:
        t["qk"] = _dot_t(t["q"], t["k"]) * t["decay"]
    ns = [jnp.where(t["strict"], t["kk"] * t["decay"], 0.0) for t in ch]
    tinvs = _unit_tri_inverse(ns, [t["order"] for t in ch])
    for t, tinv in zip(ch, tinvs):
        eg = jnp.exp(t["gcc"])
        t["eg"] = eg
        rhs = jnp.concatenate([t["v"] * t["beta"], t["kbeta"] * eg], axis=1)
        t["sol"] = _dot3_split(*_split_bf16(tinv), *_split_bf16(rhs))
    for t in ch:
        t["state"] = st_ref[t["ln"]]
        t["v_new"] = t["sol"][:, :HEAD_DIM] - _dot(t["sol"][:, HEAD_DIM:], t["state"])
    for t in ch:
        t["o_ref"][:, t["sl"]] = _dot(t["q"] * t["eg"], t["state"]) + _dot(t["qk"], t["v_new"])
    for t in ch:
        glast = t["gcc"][t["last"]:t["last"] + 1, :]
        kd_t = jnp.transpose(jnp.concatenate([t["k"] * jnp.exp(glast - t["gcc"]), zpad], axis=0))[:, 0:c]
        st_ref[t["ln"]] = t["state"] * jnp.exp(glast) + _dot(kd_t, t["v_new"])


def _gdn_scan_call(qn, kn, vn, g, beta, *, nb, seq, ctx_len):
    t_all = qn.shape[0]
    hb = GDN_HB
    c = GDN_CHUNK
    ncc = ctx_len // c
    ncl = seq // c
    ctx0 = nb * ncl
    n_hg = N_HEADS // hb

    def row_f(b, s):
        return jnp.where(s < ncc, ctx0 + b * ncc + s, b * ncl + (s - ncc))

    def row_b(b, s):
        return jnp.where(s < ncc, ctx0 + b * ncc + (ncc - 1 - s), b * ncl + (ncl - 1 - (s - ncc)))

    def specs(rowfn):
        big = pl.BlockSpec((c, hb * HEAD_DIM), lambda b, hg, s: (rowfn(b, s), hg))
        small = pl.BlockSpec((c, HEAD_DIM), lambda b, hg, s: (rowfn(b, s), hg))
        return [big, big, big, small, small]

    out_f = pl.BlockSpec((c, hb * HEAD_DIM), lambda b, hg, s: (row_f(b, s), hg))
    out_b = pl.BlockSpec((c, hb * HEAD_DIM), lambda b, hg, s: (row_b(b, s), hg))
    return pl.pallas_call(
        functools.partial(_gdn_scan_kernel, hb=hb),
        grid=(nb, n_hg, ncc + ncl),
        in_specs=specs(row_f) + specs(row_b),
        out_specs=[out_f, out_b],
        out_shape=[jax.ShapeDtypeStruct((t_all, GROUP_W), F32)] * 2,
        scratch_shapes=[pltpu.VMEM((2 * hb, HEAD_DIM, HEAD_DIM), F32)],
        compiler_params=_cparams(("arbitrary", "arbitrary", "arbitrary")),
        name="gdn_scan",
    )(qn, kn, vn, g, beta, qn, kn, vn, g, beta)


def _outproj_kernel(nal_ref, nac_ref, of_ref, ob_ref, z_ref, x_ref, ga_ref, shf_ref, scf_ref, w_ref, ng_ref, lg_ref,
                    lb_ref, x1_ref, h2t_ref, gdn_scr, *, tiles_per_seq, nb):
    i = pl.program_id(0)
    r = jnp.minimum(i // tiles_per_seq, nb)
    na = jnp.where(i < nb * tiles_per_seq, nal_ref[...], nac_ref[...])
    o = of_ref[...] + ob_ref[...]
    ng = ng_ref[...]
    for h in range(N_HEADS):
        sl = slice(h * HEAD_DIM, (h + 1) * HEAD_DIM)
        oh = o[:, sl]
        oh = oh * lax.rsqrt(jnp.mean(oh * oh, axis=-1, keepdims=True) + 1e-6) * ng
        gdn_scr[:, sl] = (oh * _silu(z_ref[:, sl])).astype(BF16)
    mix = _dot(na.astype(BF16), w_ref[0:GROUP_W, :]) + _dot(gdn_scr[...], w_ref[GROUP_W:2 * GROUP_W, :])
    y = DEEPNORM_ALPHA * x_ref[...] + ga_ref[pl.ds(r, 1), :] * mix
    x1 = _layer_norm_rows(y, lg_ref[...], lb_ref[...])
    x1_ref[...] = x1
    h2 = x1 * (1.0 + scf_ref[pl.ds(r, 1), :]) + shf_ref[pl.ds(r, 1), :]
    h2t_ref[...] = jnp.transpose(h2).astype(BF16)


def _outproj_call(na_lat, na_ctx, o_f, o_b, p_all, xa, mod_l, w_out, norm_g, ln_g, ln_b, *, seq, nb, rows):
    t_all, d = rows, xa.shape[1]
    tm = TM_OUT
    n_lat_tiles = nb * seq // tm
    kern = functools.partial(_outproj_kernel, tiles_per_seq=seq // tm, nb=nb)
    row = lambda i: (i, 0)
    const = lambda i: (0, 0)
    return pl.pallas_call(
        kern,
        grid=(t_all // tm,),
        in_specs=[
            pl.BlockSpec((tm, GROUP_W), lambda i: (jnp.minimum(i, n_lat_tiles - 1), 0)),
            pl.BlockSpec((tm, GROUP_W), lambda i: (jnp.maximum(i - n_lat_tiles, 0), 0)),
            pl.BlockSpec((tm, GROUP_W), row),
            pl.BlockSpec((tm, GROUP_W), row),
            pl.BlockSpec((tm, GROUP_W), lambda i: (i, 6)),
            pl.BlockSpec((tm, d), row),
            pl.BlockSpec((8, d), lambda i: (0, 2)),
            pl.BlockSpec((8, d), lambda i: (0, 3)),
            pl.BlockSpec((8, d), lambda i: (0, 4)),
            pl.BlockSpec((d, d), const),
            pl.BlockSpec((1, HEAD_DIM), const),
            pl.BlockSpec((1, d), const),
            pl.BlockSpec((1, d), const),
        ],
        out_specs=[
            pl.BlockSpec((tm, d), row),
            pl.BlockSpec((d, tm), lambda i: (0, i)),
        ],
        out_shape=[
            jax.ShapeDtypeStruct((t_all, d), F32),
            jax.ShapeDtypeStruct((d, t_all), BF16),
        ],
        scratch_shapes=[pltpu.VMEM((tm, GROUP_W), BF16)],
        compiler_params=_cparams(("arbitrary",)),
        name="out_proj_ln",
    )(na_lat, na_ctx, o_f, o_b, p_all, xa, mod_l, mod_l, mod_l, w_out, norm_g, ln_g, ln_b)


def _dup_bf16_words(x):
    u = pltpu.bitcast(x.astype(BF16).astype(F32), jnp.uint32)
    return u | (u >> 16)


def _peer_score_kernel(ht_ref, w_ref, sk_ref, l_ref, ea_ref, rb_ref, eb_ref, s_scr, atop, btop, cand):
    tm = TM_PEER
    qt = _dot(w_ref[...], ht_ref[...])
    half = PEER_QDIM // 2
    s_scr[0] = _dot(sk_ref[0], qt[0:half])
    s_scr[1] = _dot(sk_ref[1], qt[half:2 * half])
    n_half = PEER_TOPK // 2
    for tc in range(tm // 128):
        ls = slice(tc * 128, (tc + 1) * 128)
        s0 = s_scr[0, :, ls]
        s1 = s_scr[1, :, ls]
        sc = s0
        for k in range(PEER_TOPK):
            m = jnp.max(sc, axis=0, keepdims=True)
            atop[k:k + 1, ls] = m
            sc = jnp.where(sc == m, -jnp.inf, sc)
        sc = s1
        rank = jnp.full(s1.shape, float(PEER_TOPK), F32)
        for k in range(PEER_TOPK):
            m = jnp.max(sc, axis=0, keepdims=True)
            btop[k:k + 1, ls] = m
            hit = sc == m
            rank = jnp.minimum(rank, jnp.where(hit, float(k), float(PEER_TOPK)))
            sc = jnp.where(hit, -jnp.inf, sc)
        bt = btop[:, ls]
        cand[0:PEER_TOPK, ls] = atop[0:1, ls] + bt
        for i in range(1, n_half):
            r0 = PEER_TOPK + (i - 1) * n_half
            cand[r0:r0 + n_half, ls] = atop[i:i + 1, ls] + bt[0:n_half, :]
        r0 = PEER_TOPK + (n_half - 1) * n_half
        cand[r0:r0 + n_half, ls] = atop[n_half:PEER_TOPK, ls] + bt[0:1, :]
        cv = cand[:, ls]
        m0 = jnp.max(cv, axis=0, keepdims=True)
        z = jnp.zeros_like(m0)
        tau = m0
        for k in range(PEER_TOPK):
            tau = jnp.max(cv, axis=0, keepdims=True)
            z = z + jnp.exp(tau - m0)
            cv = jnp.where(cv == tau, -jnp.inf, cv)
        row = lambda k: bt[k:k + 1, :]
        ge = lambda v: (s0 + v) >= tau
        c8 = ge(row(7))
        c4 = ge(jnp.where(c8, row(11), row(3)))
        c2 = ge(jnp.where(c8, jnp.where(c4, row(13), row(9)), jnp.where(c4, row(5), row(1))))
        c1 = ge(jnp.where(c8,
                          jnp.where(c4, jnp.where(c2, row(14), row(12)), jnp.where(c2, row(10), row(8))),
                          jnp.where(c4, jnp.where(c2, row(6), row(4)), jnp.where(c2, row(2), row(0)))))
        cnt = (jnp.where(c8, 8.0, 0.0) + jnp.where(c4, 4.0, 0.0) + jnp.where(c2, 2.0, 0.0)
               + jnp.where(c1, 1.0, 0.0) + jnp.where(ge(row(15)), 1.0, 0.0))
        l_ref[0, :, ls] = _dup_bf16_words(cnt)
        ea_ref[0, :, ls] = _dup_bf16_words(jnp.exp(s0 - atop[0:1, ls]) * (1.0 / z))
        rb_ref[0, :, ls] = rank.astype(BF16)
        eb_ref[0, :, ls] = jnp.exp(s1 - bt[0:1, :]).astype(BF16)


def _peer_score_call(h2t, wq_t, subkeys):
    d, t_all = h2t.shape
    tm = TM_PEER
    ncand = PEER_TOPK + (PEER_TOPK // 2) ** 2
    tok = lambda i, h: (h, 0, i)
    return pl.pallas_call(
        _peer_score_kernel,
        grid=(t_all // tm, PEER_HEADS),
        in_specs=[
            pl.BlockSpec((d, tm), lambda i, h: (0, i)),
            pl.BlockSpec((PEER_QDIM, d), lambda i, h: (h, 0)),
            pl.BlockSpec((2, PEER_NKEYS, PEER_QDIM // 2), lambda i, h: (0, 0, 0)),
        ],
        out_specs=[pl.BlockSpec((1, PEER_NKEYS, tm), tok)] * 4,
        out_shape=[
            jax.ShapeDtypeStruct((PEER_HEADS, PEER_NKEYS, t_all), jnp.uint32),
            jax.ShapeDtypeStruct((PEER_HEADS, PEER_NKEYS, t_all), jnp.uint32),
            jax.ShapeDtypeStruct((PEER_HEADS, PEER_NKEYS, t_all), BF16),
            jax.ShapeDtypeStruct((PEER_HEADS, PEER_NKEYS, t_all), BF16),
        ],
        scratch_shapes=[
            pltpu.VMEM((2, PEER_NKEYS, tm), F32),
            pltpu.VMEM((PEER_TOPK, tm), F32),
            pltpu.VMEM((PEER_TOPK, tm), F32),
            pltpu.VMEM((ncand, tm), F32),
        ],
        compiler_params=_cparams(("arbitrary", "arbitrary")),
        name="peer_scores",
    )(h2t, wq_t, subkeys)


def _peer_dense_kernel(h_ref, u_ref, vt_ref, l_ref, ea_ref, rb_ref, eb_ref, o_ref, act0, act1, w0, w1):
    m = pl.program_id(1)
    n_tiles = pl.num_programs(1) - 2
    tm = TM_PEER
    nk = PEER_NKEYS
    rows_per_step = TE_PEER // nk
    jq_rows = 32
    n_jq = nk // jq_rows

    @pl.when(m == 0)
    def _():
        o_ref[...] = jnp.zeros(o_ref.shape, F32)
        for r in (act0, act1):
            r[...] = jnp.zeros(r.shape, F32)
        for r in (w0, w1):
            r[...] = jnp.zeros(r.shape, BF16)

    valid_b = jnp.logical_and(m >= 1, m <= n_tiles)
    sqrt_half = np.float32(np.sqrt(0.5))

    def stages(act_a, w_c, act_b, w_b):
        n_tc = tm // 128

        def stage_a(r):
            rs = slice(r * 128, (r + 1) * 128)
            act_a[rs, :] = _dot(u_ref[rs, :], h_ref[...])

        def stage_c(q):
            rs = slice(q * 128, (q + 1) * 128)
            o_ref[rs, :] += _dot(vt_ref[rs, :], w_c[...])

        def stage_b(tc, jq):
            ls = slice(tc * 128, (tc + 1) * 128)
            js = slice(jq * (jq_rows // 16), (jq + 1) * (jq_rows // 16))
            g = [jnp.zeros((jq_rows // 16, 16, 128), BF16) for _ in range(rows_per_step)]
            for h in range(PEER_HEADS):
                rk = rb_ref[h, js, :, ls]
                ee = eb_ref[h, js, :, ls]
                for ii in range(rows_per_step):
                    l_b = pltpu.bitcast(jnp.broadcast_to(l_ref[h, ii:ii + 1, ls], (8, 128)), BF16)
                    e_b = pltpu.bitcast(jnp.broadcast_to(ea_ref[h, ii:ii + 1, ls], (8, 128)), BF16)
                    g[ii] = g[ii] + jnp.where(rk < l_b, ee * e_b, jnp.zeros((), BF16))
            for ii in range(rows_per_step):
                r0 = ii * nk + jq * jq_rows
                xa = act_b[r0:r0 + jq_rows, ls]
                ge = (0.5 * xa * (1.0 + lax.erf(xa * sqrt_half))).astype(BF16)
                wv = jnp.where(valid_b, ge * g[ii].reshape(jq_rows, 128), jnp.zeros((), BF16))
                w_b[r0:r0 + jq_rows, ls] = wv

        mxu = []
        for q in range(TE_PEER // 128):
            mxu += [functools.partial(stage_a, q)] + [functools.partial(stage_c, 4 * q + k) for k in range(4)]
        blocks = [(tc, jq) for tc in range(n_tc) for jq in range(n_jq)]
        mi = 0
        for bi, (tc, jq) in enumerate(blocks):
            while mi < len(mxu) and mi * len(blocks) <= bi * len(mxu):
                mxu[mi]()
                mi += 1
            stage_b(tc, jq)
        while mi < len(mxu):
            mxu[mi]()
            mi += 1

    @pl.when(m % 2 == 0)
    def _():
        stages(act0, w0, act1, w1)

    @pl.when(m % 2 == 1)
    def _():
        stages(act1, w1, act0, w0)


def _peer_dense_call(hb, u_bf, vt_bf, l_pk, ea_pk, rb, eb):
    d, t_all = hb.shape
    ne = u_bf.shape[0]
    tm, te = TM_PEER, TE_PEER
    n_tiles = ne // te
    rows = te // PEER_NKEYS
    row_tile = lambda i, m: (0, jnp.clip(m - 1, 0, n_tiles - 1), 0, i)
    all_j = lambda i, m: (0, 0, 0, i)
    l4 = l_pk.reshape(PEER_HEADS, n_tiles, rows, t_all)
    ea4 = ea_pk.reshape(PEER_HEADS, n_tiles, rows, t_all)
    rb4 = rb.reshape(PEER_HEADS, PEER_NKEYS // 16, 16, t_all)
    eb4 = eb.reshape(PEER_HEADS, PEER_NKEYS // 16, 16, t_all)
    return pl.pallas_call(
        _peer_dense_kernel,
        grid=(t_all // tm, n_tiles + 2),
        in_specs=[
            pl.BlockSpec((d, tm), lambda i, m: (0, i)),
            pl.BlockSpec((te, d), lambda i, m: (jnp.minimum(m, n_tiles - 1), 0)),
            pl.BlockSpec((d, te), lambda i, m: (0, jnp.maximum(m - 2, 0))),
            pl.BlockSpec((PEER_HEADS, None, rows, tm), row_tile),
            pl.BlockSpec((PEER_HEADS, None, rows, tm), row_tile),
            pl.BlockSpec((PEER_HEADS, PEER_NKEYS // 16, 16, tm), all_j),
            pl.BlockSpec((PEER_HEADS, PEER_NKEYS // 16, 16, tm), all_j),
        ],
        out_specs=pl.BlockSpec((d, tm), lambda i, m: (0, i)),
        out_shape=jax.ShapeDtypeStruct((d, t_all), F32),
        scratch_shapes=[
            pltpu.VMEM((te, tm), F32),
            pltpu.VMEM((te, tm), F32),
            pltpu.VMEM((te, tm), BF16),
            pltpu.VMEM((te, tm), BF16),
        ],
        compiler_params=_cparams(("arbitrary", "arbitrary")),
        name="peer_dense",
    )(hb, u_bf, vt_bf, l4, ea4, rb4, eb4)


def _peer_out_kernel(ft_ref, x1_ref, gf_ref, lg_ref, lb_ref, o_ref, *, tiles_per_seq, nb):
    i = pl.program_id(0)
    r = jnp.minimum(i // tiles_per_seq, nb)
    y = DEEPNORM_ALPHA * x1_ref[...] + gf_ref[pl.ds(r, 1), :] * jnp.transpose(ft_ref[...])
    o_ref[...] = _layer_norm_rows(y, lg_ref[...], lb_ref[...])


def _peer_out_call(ffn_t, x1, mod_l, ln_g, ln_b, *, seq, nb):
    t_all, d = x1.shape
    tm = TM_OUT
    kern = functools.partial(_peer_out_kernel, tiles_per_seq=seq // tm, nb=nb)
    return pl.pallas_call(
        kern,
        grid=(t_all // tm,),
        in_specs=[
            pl.BlockSpec((d, tm), lambda i: (0, i)),
            pl.BlockSpec((tm, d), lambda i: (i, 0)),
            pl.BlockSpec((8, d), lambda i: (0, 5)),
            pl.BlockSpec((1, d), lambda i: (0, 0)),
            pl.BlockSpec((1, d), lambda i: (0, 0)),
        ],
        out_specs=pl.BlockSpec((tm, d), lambda i: (i, 0)),
        out_shape=jax.ShapeDtypeStruct((t_all, d), F32),
        compiler_params=_cparams(("arbitrary",)),
        name="peer_out_ln",
    )(ffn_t, x1, mod_l, ln_g, ln_b)


def _ab_columns(w_tail):
    depth, d, _ = w_tail.shape
    hb = GDN_HB
    n_hg = N_HEADS // hb
    wa = w_tail[:, :, :2 * N_HEADS].reshape(depth, d, 2, n_hg, hb)
    wb = w_tail[:, :, 2 * N_HEADS:].reshape(depth, d, 2, n_hg, hb)

    def lay(w):
        w = w.transpose(0, 1, 3, 2, 4).reshape(depth, d, n_hg, 2 * hb)
        w = jnp.pad(w, ((0, 0), (0, 0), (0, 0), (0, HEAD_DIM - 2 * hb)))
        return w.reshape(depth, d, n_hg * HEAD_DIM)

    return jnp.concatenate([lay(wa), lay(wb)], axis=-1)


def _head_param_lanes(p):
    depth = p.shape[0]
    hb = GDN_HB
    n_hg = N_HEADS // hb
    p = p.reshape(depth, 2, n_hg, hb).transpose(0, 2, 1, 3).reshape(depth, n_hg, 2 * hb)
    p = jnp.pad(p, ((0, 0), (0, 0), (0, HEAD_DIM - 2 * hb)))
    return p.reshape(depth, 1, n_hg * HEAD_DIM)


def kernel(x, c, ctx, c_ctx, w_mod, b_mod, w_in, w_out, na_rpb, gdn_conv, gdn_a_log, gdn_dt_bias, gdn_norm_g,
           peer_wq, peer_subkeys, peer_u, peer_v, ln_g, ln_b):
    nb, seq, d = x.shape
    ctx_len = ctx.shape[1]
    depth = w_mod.shape[0]
    assert d == D_MODEL and nb + 1 <= 8
    assert seq % (NA_QROWS * GRID_W) == 0 and seq % TM_PROJ == 0
    t_lat = nb * seq
    dims = dict(nb=nb, seq=seq, ctx_len=ctx_len)

    xa = jnp.concatenate([x.reshape(t_lat, d), ctx.reshape(nb * ctx_len, d)], axis=0)
    cs = jnp.concatenate([c, c_ctx[None, :], jnp.zeros((8 - nb - 1, d), F32)], axis=0)
    mod = _mod_call(cs, w_mod, b_mod)

    w_main = w_in[:, :, :P_MAIN_W].astype(BF16)
    w_ab = _ab_columns(w_in[:, :, P_MAIN_W:]).astype(BF16)
    w_out_bf = w_out.astype(BF16)
    alog = _head_param_lanes(gdn_a_log)
    dtb = _head_param_lanes(gdn_dt_bias)
    conv_w = jnp.pad(gdn_conv, ((0, 0), (0, 8 - GDN_CONV), (0, 0)))
    rope_c, rope_s = _rope_tables(seq, ctx_len)
    wq_t = peer_wq.astype(BF16).transpose(0, 2, 1)
    u_bf = peer_u.astype(BF16)
    vt_bf = peer_v.astype(BF16).transpose(0, 2, 1)

    for l in range(depth):
        p_all, pab = _inproj_call(xa, mod[l], w_main[l], w_ab[l], seq=seq, nb=nb)
        bias = _na_bias(na_rpb[l], seq // GRID_W)
        na_lat = _na_call(p_all, bias, **dims)
        na_ctx = _ctx_attn_call(p_all, **dims)
        qn, kn, vn, g, beta = _gdn_prep_call(p_all, pab, conv_w[l], alog[l], dtb[l], rope_c, rope_s, **dims)
        o_f, o_b = _gdn_scan_call(qn, kn, vn, g, beta, **dims)
        x1, h2t = _outproj_call(na_lat, na_ctx, o_f, o_b, p_all, xa, mod[l], w_out_bf[l], gdn_norm_g[l][None, :],
                                ln_g[l, 0][None, :], ln_b[l, 0][None, :], seq=seq, nb=nb,
                                rows=xa.shape[0] if l < depth - 1 else t_lat)
        l_pk, ea_pk, rb, eb = _peer_score_call(h2t, wq_t[l], peer_subkeys[l])
        ffn_t = _peer_dense_call(h2t, u_bf[l], vt_bf[l], l_pk, ea_pk, rb, eb)
        xa = _peer_out_call(ffn_t, x1, mod[l], ln_g[l, 1][None, :], ln_b[l, 1][None, :], seq=seq, nb=nb)
    return xa.reshape(nb, seq, d)
```

```python
import functools

import numpy as np
import jax
import jax.numpy as jnp
from jax import lax
from jax.experimental import pallas as pl
from jax.experimental.pallas import tpu as pltpu

F32 = jnp.float32
BF16 = jnp.bfloat16
HIGHEST = lax.Precision.HIGHEST

D_MODEL = 2048
HEAD_DIM = 128
N_HEADS = 8
GROUP_W = N_HEADS * HEAD_DIM
GRID_W = 64
NA_KR = 8
NA_KC = 16
NA_QROWS = 8
NA_KROWS = 16
NA_SUB = 2
GDN_CHUNK = 64
GDN_CONV = 5
GDN_HB = 8
ROPE_THETA = 10000.0
PEER_HEADS = 8
PEER_TOPK = 16
PEER_NKEYS = 128
PEER_QDIM = 256
N_MOD = 6
LN_EPS = 1e-6
NEG_INF = -1e30
DEPTH_FOR_DEEPNORM = 4
DEEPNORM_ALPHA = (2 * DEPTH_FOR_DEEPNORM) ** 0.25
P_MAIN_W = 7 * GROUP_W
VMEM_LIMIT = 56 * 1024 * 1024

TM_PROJ = 512
TN_PROJ = 1792
TM_OUT = 256
TM_PREP = 256
TM_PEER = 512
TE_PEER = 512


def _cparams(sem):
    return pltpu.CompilerParams(dimension_semantics=sem, vmem_limit_bytes=VMEM_LIMIT)


def _sigmoid(x):
    return 1.0 / (1.0 + jnp.exp(-x))


def _silu(x):
    return x * _sigmoid(x)


def _dot(a, b):
    return jnp.dot(a, b, preferred_element_type=F32)


def _dot_t(a, b):
    return lax.dot_general(a, b, (((1,), (1,)), ((), ())), preferred_element_type=F32)


def _layer_norm_rows(y, g, b):
    mu = jnp.mean(y, axis=-1, keepdims=True)
    yc = y - mu
    var = jnp.mean(yc * yc, axis=-1, keepdims=True)
    return yc * lax.rsqrt(var + LN_EPS) * g + b


def _mod_kernel(c_ref, w_ref, b_ref, o_ref):
    s = _silu(c_ref[...])
    o_ref[0] = jnp.dot(s, w_ref[0], precision=HIGHEST, preferred_element_type=F32) + b_ref[0]


def _mod_call(cs, w_mod, b_mod):
    depth, d, n = w_mod.shape
    tn = 1024
    return pl.pallas_call(
        _mod_kernel,
        grid=(depth, n // tn),
        in_specs=[
            pl.BlockSpec((8, d), lambda l, j: (0, 0)),
            pl.BlockSpec((1, d, tn), lambda l, j: (l, 0, j)),
            pl.BlockSpec((1, 1, tn), lambda l, j: (l, 0, j)),
        ],
        out_specs=pl.BlockSpec((1, 8, tn), lambda l, j: (l, 0, j)),
        out_shape=jax.ShapeDtypeStruct((depth, 8, n), F32),
        compiler_params=_cparams(("arbitrary", "arbitrary")),
        name="adaln_mod",
    )(cs, w_mod, b_mod.reshape(depth, 1, n))


def _inproj_kernel(x_ref, sh_ref, sc_ref, w_ref, wab_ref, p_ref, pab_ref, h_scr, *, tiles_per_seq, nb):
    i = pl.program_id(0)
    j = pl.program_id(1)

    @pl.when(j == 0)
    def _():
        r = jnp.minimum(i // tiles_per_seq, nb)
        sh = sh_ref[pl.ds(r, 1), :]
        sc = sc_ref[pl.ds(r, 1), :]
        hb = (x_ref[...] * (1.0 + sc) + sh).astype(BF16)
        h_scr[...] = hb
        pab_ref[...] = _dot(hb, wab_ref[...])

    p_ref[...] = _dot(h_scr[...], w_ref[...])


def _inproj_call(xa, mod_l, w_main, w_ab, *, seq, nb):
    t_all, d = xa.shape
    n = w_main.shape[1]
    nab = w_ab.shape[1]
    tm, tn = TM_PROJ, TN_PROJ
    kern = functools.partial(_inproj_kernel, tiles_per_seq=seq // tm, nb=nb)
    return pl.pallas_call(
        kern,
        grid=(t_all // tm, n // tn),
        in_specs=[
            pl.BlockSpec((tm, d), lambda i, j: (i, 0)),
            pl.BlockSpec((8, d), lambda i, j: (0, 0)),
            pl.BlockSpec((8, d), lambda i, j: (0, 1)),
            pl.BlockSpec((d, tn), lambda i, j: (0, j)),
            pl.BlockSpec((d, nab), lambda i, j: (0, 0)),
        ],
        out_specs=[
            pl.BlockSpec((tm, tn), lambda i, j: (i, j)),
            pl.BlockSpec((tm, nab), lambda i, j: (i, 0)),
        ],
        out_shape=[
            jax.ShapeDtypeStruct((t_all, n), F32),
            jax.ShapeDtypeStruct((t_all, nab), F32),
        ],
        scratch_shapes=[pltpu.VMEM((tm, d), BF16)],
        compiler_params=_cparams(("arbitrary", "arbitrary")),
        name="in_proj",
    )(xa, mod_l, mod_l, w_main, w_ab)


def _na_bias_index_tables(nrows):
    rq, rk = NA_QROWS, NA_KROWS
    big = 1 << 20
    cfg = [(0, 0, nrows), (8, 4, big), (nrows - rq, nrows - rk, nrows)]
    dr = np.zeros((3, rq, rk), np.int32)
    rv = np.zeros((3, rq, rk), bool)
    for v, (r0, start, nr) in enumerate(cfg):
        r = r0 + np.arange(rq)[:, None]
        kr = start + np.arange(rk)[None, :]
        rs = np.clip(r - NA_KR // 2, 0, nr - NA_KR)
        rv[v] = (kr >= rs) & (kr < rs + NA_KR)
        dr[v] = np.clip(kr - r + NA_KR - 1, 0, 2 * NA_KR - 2)
    qc = np.arange(GRID_W)[:, None]
    kc = np.arange(GRID_W)[None, :]
    ws = np.clip(qc - NA_KC // 2, 0, GRID_W - NA_KC)
    cv = (kc >= ws) & (kc < ws + NA_KC)
    dc = np.clip(kc - qc + NA_KC - 1, 0, 2 * NA_KC - 2).astype(np.int32)
    return dr, rv, dc, cv


def _na_bias(rpb_l, nrows):
    dr, rv, dc, cv = _na_bias_index_tables(nrows)
    tc = jnp.where(cv[None, None], rpb_l[:, :, dc], NEG_INF)
    n_dr = tc.shape[1]
    tc = jnp.concatenate([tc, jnp.full((N_HEADS, 1, GRID_W, GRID_W), NEG_INF, F32)], axis=1)
    blk = tc[:, np.where(rv, dr, n_dr)]
    blk = blk.transpose(0, 1, 2, 4, 3, 5)
    return blk.reshape(N_HEADS, 3, NA_QROWS * GRID_W, NA_KROWS * GRID_W)


def _na_kernel(q_ref, k0, k1, k2, k3, v0, v1, v2, v3, kc_ref, vc_ref, bias_ref, o_ref):
    kb = 4 * GRID_W
    ks = [kr[...].astype(BF16) for kr in (k0, k1, k2, k3, kc_ref)]
    vs = [vr[...].astype(BF16) for vr in (v0, v1, v2, v3, vc_ref)]
    rows = q_ref.shape[0] // NA_SUB
    for sub in range(NA_SUB):
        rs = slice(sub * rows, (sub + 1) * rows)
        q = (q_ref[rs, :] * (HEAD_DIM ** -0.5)).astype(BF16)
        s = [_dot_t(q, ks[j]) + bias_ref[rs, j * kb:(j + 1) * kb] for j in range(4)]
        s.append(_dot_t(q, ks[4]))
        m = s[0].max(axis=-1, keepdims=True)
        for t in s[1:]:
            m = jnp.maximum(m, t.max(axis=-1, keepdims=True))
        p = [jnp.exp(t - m) for t in s]
        l = p[0].sum(axis=-1, keepdims=True)
        for t in p[1:]:
            l = l + t.sum(axis=-1, keepdims=True)
        o = _dot(p[0].astype(BF16), vs[0])
        for t, vv in zip(p[1:], vs[1:]):
            o = o + _dot(t.astype(BF16), vv)
        o_ref[rs, :] = o / l


def _na_call(p_all, bias, *, nb, seq, ctx_len):
    t_all = p_all.shape[0]
    nrows = seq // GRID_W
    nrb = nrows // NA_QROWS
    tq = NA_QROWS * GRID_W
    tk = 4 * GRID_W
    assert ctx_len == tk and nrows >= NA_KROWS
    kblocks_per_seq = seq // tk
    max_sb = kblocks_per_seq - 4
    ctx_blk0 = nb * seq // tk

    def sb(r):
        return jnp.clip(2 * r - 1, 0, max_sb)

    def kspec(j, colbase):
        return pl.BlockSpec((tk, HEAD_DIM), lambda h, b, r: (b * kblocks_per_seq + sb(r) + j, colbase + h))

    def variant(r):
        return jnp.where(r == 0, 0, jnp.where(r == nrb - 1, 2, 1))

    in_specs = [pl.BlockSpec((tq, HEAD_DIM), lambda h, b, r: (b * nrb + r, h))]
    in_specs += [kspec(j, N_HEADS) for j in range(4)]
    in_specs += [kspec(j, 2 * N_HEADS) for j in range(4)]
    in_specs += [
        pl.BlockSpec((tk, HEAD_DIM), lambda h, b, r: (ctx_blk0 + b, N_HEADS + h)),
        pl.BlockSpec((tk, HEAD_DIM), lambda h, b, r: (ctx_blk0 + b, 2 * N_HEADS + h)),
        pl.BlockSpec((None, None, tq, NA_KROWS * GRID_W), lambda h, b, r: (h, variant(r), 0, 0)),
    ]
    return pl.pallas_call(
        _na_kernel,
        grid=(N_HEADS, nb, nrb),
        in_specs=in_specs,
        out_specs=pl.BlockSpec((tq, HEAD_DIM), lambda h, b, r: (b * nrb + r, h)),
        out_shape=jax.ShapeDtypeStruct((nb * seq, GROUP_W), F32),
        compiler_params=_cparams(("arbitrary", "arbitrary", "arbitrary")),
        name="na_attention",
    )(*([p_all] * 11), bias)


def _ctx_attn_kernel(q_ref, k_ref, v_ref, o_ref):
    q = q_ref[...].astype(BF16)
    s = _dot_t(q, k_ref[...].astype(BF16)) * (HEAD_DIM ** -0.5)
    m = s.max(axis=-1, keepdims=True)
    p = jnp.exp(s - m)
    l = p.sum(axis=-1, keepdims=True)
    o_ref[...] = _dot(p.astype(BF16), v_ref[...].astype(BF16)) / l


def _ctx_attn_call(p_all, *, nb, seq, ctx_len):
    blk0 = nb * seq // ctx_len
    return pl.pallas_call(
        _ctx_attn_kernel,
        grid=(nb, N_HEADS),
        in_specs=[
            pl.BlockSpec((ctx_len, HEAD_DIM), lambda b, h: (blk0 + b, h)),
            pl.BlockSpec((ctx_len, HEAD_DIM), lambda b, h: (blk0 + b, N_HEADS + h)),
            pl.BlockSpec((ctx_len, HEAD_DIM), lambda b, h: (blk0 + b, 2 * N_HEADS + h)),
        ],
        out_specs=pl.BlockSpec((ctx_len, HEAD_DIM), lambda b, h: (b, h)),
        out_shape=jax.ShapeDtypeStruct((nb * ctx_len, GROUP_W), F32),
        compiler_params=_cparams(("arbitrary", "arbitrary")),
        name="ctx_attention",
    )(p_all, p_all, p_all)


def _rope_tables(seq, ctx_len):
    t = np.arange(seq)
    row = (t // GRID_W).astype(np.float32)
    col = (t % GRID_W).astype(np.float32)
    n_freq = HEAD_DIM // 4
    inv = (ROPE_THETA ** (-np.arange(n_freq, dtype=np.float32) / n_freq)).astype(np.float32)
    ang = jnp.stack([jnp.asarray(row)[:, None] * inv, jnp.asarray(col)[:, None] * inv], axis=1)
    cos, sin = jnp.cos(ang), jnp.sin(ang)
    c = jnp.concatenate([cos, cos], axis=-1).reshape(seq, HEAD_DIM)
    s = jnp.concatenate([-sin, sin], axis=-1).reshape(seq, HEAD_DIM)
    c = jnp.concatenate([c, jnp.ones((ctx_len, HEAD_DIM), F32)], axis=0)
    s = jnp.concatenate([s, jnp.zeros((ctx_len, HEAD_DIM), F32)], axis=0)
    return c, s


def _gdn_prep_kernel(cur_ref, prev_ref, next_ref, pab_ref, cw_ref, alog_ref, dtb_ref, rc_ref, rs_ref,
                     q_ref, k_ref, v_ref, g_ref, b_ref, ext_ref, *, n_lat_tiles, tps, nab_half):
    i = pl.program_id(0)
    tm = TM_PREP
    is_ctx = i >= n_lat_tiles
    first = jnp.logical_or(is_ctx, i % tps == 0)
    last = jnp.logical_or(is_ctx, i % tps == tps - 1)
    ext_ref[8:8 + tm, :] = cur_ref[...]
    ext_ref[0:8, :] = jnp.where(first, 0.0, prev_ref[...])
    ext_ref[8 + tm:16 + tm, :] = jnp.where(last, 0.0, next_ref[...])

    lane = lax.broadcasted_iota(jnp.int32, (tm, HEAD_DIM), 1)
    half0 = (lane % (HEAD_DIM // 2)) < (HEAD_DIM // 4)
    rc = rc_ref[...]
    rs = rs_ref[...]
    base = 8 - GDN_CONV // 2
    outs = (q_ref, k_ref, v_ref)
    for part in range(3):
        for h in range(N_HEADS):
            c0 = part * GROUP_W + h * HEAD_DIM
            acc = cw_ref[0:1, c0:c0 + HEAD_DIM] * ext_ref[base:base + tm, c0:c0 + HEAD_DIM]
            for t in range(1, GDN_CONV):
                acc = acc + cw_ref[t:t + 1, c0:c0 + HEAD_DIM] * ext_ref[base + t:base + t + tm, c0:c0 + HEAD_DIM]
            y = _silu(acc)
            if part < 2:
                y = y * lax.rsqrt(jnp.sum(y * y, axis=-1, keepdims=True) + 1e-6)
                partner = jnp.where(half0, pltpu.roll(y, HEAD_DIM - HEAD_DIM // 4, 1),
                                    pltpu.roll(y, HEAD_DIM // 4, 1))
                y = y * rc + partner * rs
                if part == 0:
                    y = y * (HEAD_DIM ** -0.5)
            outs[part][:, h * HEAD_DIM:(h + 1) * HEAD_DIM] = y

    a = pab_ref[:, 0:nab_half] + dtb_ref[...]
    softplus = jnp.maximum(a, 0.0) + jnp.log1p(jnp.exp(-jnp.abs(a)))
    g_ref[...] = -jnp.exp(alog_ref[...]) * softplus
    b_ref[...] = _sigmoid(pab_ref[:, nab_half:2 * nab_half])


def _gdn_prep_call(p_all, pab, conv_w, alog, dtb, rope_c, rope_s, *, nb, seq, ctx_len):
    t_all = p_all.shape[0]
    tm = TM_PREP
    assert ctx_len == tm
    n_tiles = t_all // tm
    n_lat_tiles = nb * seq // tm
    tps = seq // tm
    nab_half = pab.shape[1] // 2
    qkv_w = 3 * GROUP_W
    last8 = t_all // 8 - 1
    kern = functools.partial(_gdn_prep_kernel, n_lat_tiles=n_lat_tiles, tps=tps, nab_half=nab_half)

    def rope_idx(i):
        return jnp.where(i < n_lat_tiles, i % tps, tps)

    return pl.pallas_call(
        kern,
        grid=(n_tiles,),
        in_specs=[
            pl.BlockSpec((tm, qkv_w), lambda i: (i, 1)),
            pl.BlockSpec((8, qkv_w), lambda i: (jnp.maximum(i * (tm // 8) - 1, 0), 1)),
            pl.BlockSpec((8, qkv_w), lambda i: (jnp.minimum((i + 1) * (tm // 8), last8), 1)),
            pl.BlockSpec((tm, 2 * nab_half), lambda i: (i, 0)),
            pl.BlockSpec((8, qkv_w), lambda i: (0, 0)),
            pl.BlockSpec((1, nab_half), lambda i: (0, 0)),
            pl.BlockSpec((1, nab_half), lambda i: (0, 0)),
            pl.BlockSpec((tm, HEAD_DIM), lambda i: (rope_idx(i), 0)),
            pl.BlockSpec((tm, HEAD_DIM), lambda i: (rope_idx(i), 0)),
        ],
        out_specs=[
            pl.BlockSpec((tm, GROUP_W), lambda i: (i, 0)),
            pl.BlockSpec((tm, GROUP_W), lambda i: (i, 0)),
            pl.BlockSpec((tm, GROUP_W), lambda i: (i, 0)),
            pl.BlockSpec((tm, nab_half), lambda i: (i, 0)),
            pl.BlockSpec((tm, nab_half), lambda i: (i, 0)),
        ],
        out_shape=[
            jax.ShapeDtypeStruct((t_all, GROUP_W), F32),
            jax.ShapeDtypeStruct((t_all, GROUP_W), F32),
            jax.ShapeDtypeStruct((t_all, GROUP_W), F32),
            jax.ShapeDtypeStruct((t_all, nab_half), F32),
            jax.ShapeDtypeStruct((t_all, nab_half), F32),
        ],
        scratch_shapes=[pltpu.VMEM((tm + 16, qkv_w), F32)],
        compiler_params=_cparams(("arbitrary",)),
        name="gdn_prep",
    )(p_all, p_all, p_all, pab, conv_w, alog, dtb, rope_c, rope_s)


def _unit_tri_inverse(ns, orders):
    c = GDN_CHUNK
    ds = []
    for n, (row, col) in zip(ns, orders):
        m1 = jnp.logical_and(jnp.logical_and((row >> 1) == (col >> 1), (row & 1) == 1), (col & 1) == 0)
        ds.append((row == col).astype(F32) - jnp.where(m1, n, 0.0))
    k = 2
    while k < c:
        sh = int(np.log2(2 * k))
        dsp, xs = [], []
        for n, d, (row, col) in zip(ns, ds, orders):
            mk = jnp.logical_and((row >> sh) == (col >> sh),
                                 jnp.logical_and((row & (2 * k - 1)) >= k, (col & (2 * k - 1)) < k))
            lk = jnp.where(mk, n, 0.0).astype(BF16)
            dh = d.astype(BF16)
            dsp.append(dh)
            xs.append(_dot(lk, dh))
        ds = [d - _dot(dh, x.astype(BF16)) for d, dh, x in zip(ds, dsp, xs)]
        k *= 2
    return ds


def _gdn_scan_kernel(qf, kf, vf, gf, bf, qb, kb, vb, gb, bb, of_ref, ob_ref, st_ref, *, hb):
    s = pl.program_id(2)
    c = GDN_CHUNK

    @pl.when(s == 0)
    def _():
        st_ref[...] = jnp.zeros(st_ref.shape, F32)

    ri = lax.broadcasted_iota(jnp.int32, (c, c), 0)
    ci = lax.broadcasted_iota(jnp.int32, (c, c), 1)
    zpad = jnp.zeros((HEAD_DIM - c, HEAD_DIM), F32)
    ch = []
    for d, (q_ref, k_ref, v_ref, g_ref, be_ref, o_ref) in enumerate(
            ((qf, kf, vf, gf, bf, of_ref), (qb, kb, vb, gb, bb, ob_ref))):
        incl = (ri >= ci) if d == 0 else (ri <= ci)
        gc_all = jnp.dot(incl.astype(F32), g_ref[...], precision=HIGHEST, preferred_element_type=F32)
        gc_t = jnp.transpose(jnp.concatenate([gc_all, zpad], axis=0))
        beta_all = be_ref[...]
        for hh in range(hb):
            ln = d * hb + hh
            sl = slice(hh * HEAD_DIM, (hh + 1) * HEAD_DIM)
            ch.append(dict(
                ln=ln, sl=sl, o_ref=o_ref, incl=incl,
                strict=(ri > ci) if d == 0 else (ri < ci),
                order=(ri, ci) if d == 0 else (ci, ri),
                last=c - 1 if d == 0 else 0,
                gcc=gc_all[:, ln:ln + 1], grow=gc_t[ln:ln + 1, 0:c], beta=beta_all[:, ln:ln + 1],
                q=q_ref[:, sl], k=k_ref[:, sl], v=v_ref[:, sl]))
    for t in ch:
        t["decay"] = jnp.where(t["incl"], jnp.exp(jnp.where(t["incl"], t["gcc"] - t["grow"], 0.0)), 0.0)
        t["kbeta"] = t["k"] * t["beta"]
    for t in ch:
        t["kk"] = _dot_t(t["kbeta"], t["k"])
    for t in ch:
        t["qk"] = _dot_t(t["q"], t["k"]) * t["decay"]
    ns = [jnp.where(t["strict"], t["kk"] * t["decay"], 0.0) for t in ch]
    tinvs = _unit_tri_inverse(ns, [t["order"] for t in ch])
    for t, tinv in zip(ch, tinvs):
        eg = jnp.exp(t["gcc"])
        t["eg"] = eg
        rhs = jnp.concatenate([t["v"] * t["beta"], t["kbeta"] * eg], axis=1)
        t["sol"] = _dot(tinv.astype(BF16), rhs.astype(BF16))
    for t in ch:
        t["state"] = st_ref[t["ln"]]
        t["v_new"] = t["sol"][:, :HEAD_DIM] - _dot(t["sol"][:, HEAD_DIM:], t["state"])
    for t in ch:
        t["o_ref"][:, t["sl"]] = _dot(t["q"] * t["eg"], t["state"]) + _dot(t["qk"], t["v_new"])
    for t in ch:
        glast = t["gcc"][t["last"]:t["last"] + 1, :]
        kd_t = jnp.transpose(jnp.concatenate([t["k"] * jnp.exp(glast - t["gcc"]), zpad], axis=0))[:, 0:c]
        st_ref[t["ln"]] = t["state"] * jnp.exp(glast) + _dot(kd_t, t["v_new"])


def _gdn_scan_call(qn, kn, vn, g, beta, *, nb, seq, ctx_len):
    t_all = qn.shape[0]
    hb = GDN_HB
    c = GDN_CHUNK
    ncc = ctx_len // c
    ncl = seq // c
    ctx0 = nb * ncl
    n_hg = N_HEADS // hb

    def row_f(b, s):
        return jnp.where(s < ncc, ctx0 + b * ncc + s, b * ncl + (s - ncc))

    def row_b(b, s):
        return jnp.where(s < ncc, ctx0 + b * ncc + (ncc - 1 - s), b * ncl + (ncl - 1 - (s - ncc)))

    def specs(rowfn):
        big = pl.BlockSpec((c, hb * HEAD_DIM), lambda b, hg, s: (rowfn(b, s), hg))
        small = pl.BlockSpec((c, HEAD_DIM), lambda b, hg, s: (rowfn(b, s), hg))
        return [big, big, big, small, small]

    out_f = pl.BlockSpec((c, hb * HEAD_DIM), lambda b, hg, s: (row_f(b, s), hg))
    out_b = pl.BlockSpec((c, hb * HEAD_DIM), lambda b, hg, s: (row_b(b, s), hg))
    return pl.pallas_call(
        functools.partial(_gdn_scan_kernel, hb=hb),
        grid=(nb, n_hg, ncc + ncl),
        in_specs=specs(row_f) + specs(row_b),
        out_specs=[out_f, out_b],
        out_shape=[jax.ShapeDtypeStruct((t_all, GROUP_W), F32)] * 2,
        scratch_shapes=[pltpu.VMEM((2 * hb, HEAD_DIM, HEAD_DIM), F32)],
        compiler_params=_cparams(("arbitrary", "arbitrary", "arbitrary")),
        name="gdn_scan",
    )(qn, kn, vn, g, beta, qn, kn, vn, g, beta)


def _outproj_kernel(nal_ref, nac_ref, of_ref, ob_ref, z_ref, x_ref, ga_ref, shf_ref, scf_ref, w_ref, ng_ref, lg_ref,
                    lb_ref, x1_ref, h2t_ref, gdn_scr, *, tiles_per_seq, nb):
    i = pl.program_id(0)
    r = jnp.minimum(i // tiles_per_seq, nb)
    na = jnp.where(i < nb * tiles_per_seq, nal_ref[...], nac_ref[...])
    o = of_ref[...] + ob_ref[...]
    ng = ng_ref[...]
    for h in range(N_HEADS):
        sl = slice(h * HEAD_DIM, (h + 1) * HEAD_DIM)
        oh = o[:, sl]
        oh = oh * lax.rsqrt(jnp.mean(oh * oh, axis=-1, keepdims=True) + 1e-6) * ng
        gdn_scr[:, sl] = (oh * _silu(z_ref[:, sl])).astype(BF16)
    mix = _dot(na.astype(BF16), w_ref[0:GROUP_W, :]) + _dot(gdn_scr[...], w_ref[GROUP_W:2 * GROUP_W, :])
    y = DEEPNORM_ALPHA * x_ref[...] + ga_ref[pl.ds(r, 1), :] * mix
    x1 = _layer_norm_rows(y, lg_ref[...], lb_ref[...])
    x1_ref[...] = x1
    h2 = x1 * (1.0 + scf_ref[pl.ds(r, 1), :]) + shf_ref[pl.ds(r, 1), :]
    h2t_ref[...] = jnp.transpose(h2).astype(BF16)


def _outproj_call(na_lat, na_ctx, o_f, o_b, p_all, xa, mod_l, w_out, norm_g, ln_g, ln_b, *, seq, nb, rows):
    t_all, d = rows, xa.shape[1]
    tm = TM_OUT
    n_lat_tiles = nb * seq // tm
    kern = functools.partial(_outproj_kernel, tiles_per_seq=seq // tm, nb=nb)
    row = lambda i: (i, 0)
    const = lambda i: (0, 0)
    return pl.pallas_call(
        kern,
        grid=(t_all // tm,),
        in_specs=[
            pl.BlockSpec((tm, GROUP_W), lambda i: (jnp.minimum(i, n_lat_tiles - 1), 0)),
            pl.BlockSpec((tm, GROUP_W), lambda i: (jnp.maximum(i - n_lat_tiles, 0), 0)),
            pl.BlockSpec((tm, GROUP_W), row),
            pl.BlockSpec((tm, GROUP_W), row),
            pl.BlockSpec((tm, GROUP_W), lambda i: (i, 6)),
            pl.BlockSpec((tm, d), row),
            pl.BlockSpec((8, d), lambda i: (0, 2)),
            pl.BlockSpec((8, d), lambda i: (0, 3)),
            pl.BlockSpec((8, d), lambda i: (0, 4)),
            pl.BlockSpec((d, d), const),
            pl.BlockSpec((1, HEAD_DIM), const),
            pl.BlockSpec((1, d), const),
            pl.BlockSpec((1, d), const),
        ],
        out_specs=[
            pl.BlockSpec((tm, d), row),
            pl.BlockSpec((d, tm), lambda i: (0, i)),
        ],
        out_shape=[
            jax.ShapeDtypeStruct((t_all, d), F32),
            jax.ShapeDtypeStruct((d, t_all), BF16),
        ],
        scratch_shapes=[pltpu.VMEM((tm, GROUP_W), BF16)],
        compiler_params=_cparams(("arbitrary",)),
        name="out_proj_ln",
    )(na_lat, na_ctx, o_f, o_b, p_all, xa, mod_l, mod_l, mod_l, w_out, norm_g, ln_g, ln_b)


def _dup_bf16_words(x):
    u = pltpu.bitcast(x.astype(BF16).astype(F32), jnp.uint32)
    return u | (u >> 16)


def _peer_score_kernel(ht_ref, w_ref, sk_ref, l_ref, ea_ref, rb_ref, eb_ref, s_scr, atop, btop, cand):
    tm = TM_PEER
    qt = _dot(w_ref[...], ht_ref[...])
    half = PEER_QDIM // 2
    s_scr[0] = _dot(sk_ref[0], qt[0:half])
    s_scr[1] = _dot(sk_ref[1], qt[half:2 * half])
    n_half = PEER_TOPK // 2
    for tc in range(tm // 128):
        ls = slice(tc * 128, (tc + 1) * 128)
        s0 = s_scr[0, :, ls]
        s1 = s_scr[1, :, ls]
        sc = s0
        for k in range(PEER_TOPK):
            m = jnp.max(sc, axis=0, keepdims=True)
            atop[k:k + 1, ls] = m
            sc = jnp.where(sc == m, -jnp.inf, sc)
        sc = s1
        rank = jnp.full(s1.shape, float(PEER_TOPK), F32)
        for k in range(PEER_TOPK):
            m = jnp.max(sc, axis=0, keepdims=True)
            btop[k:k + 1, ls] = m
            hit = sc == m
            rank = jnp.minimum(rank, jnp.where(hit, float(k), float(PEER_TOPK)))
            sc = jnp.where(hit, -jnp.inf, sc)
        bt = btop[:, ls]
        cand[0:PEER_TOPK, ls] = atop[0:1, ls] + bt
        for i in range(1, n_half):
            r0 = PEER_TOPK + (i - 1) * n_half
            cand[r0:r0 + n_half, ls] = atop[i:i + 1, ls] + bt[0:n_half, :]
        r0 = PEER_TOPK + (n_half - 1) * n_half
        cand[r0:r0 + n_half, ls] = atop[n_half:PEER_TOPK, ls] + bt[0:1, :]
        cv = cand[:, ls]
        m0 = jnp.max(cv, axis=0, keepdims=True)
        z = jnp.zeros_like(m0)
        tau = m0
        for k in range(PEER_TOPK):
            tau = jnp.max(cv, axis=0, keepdims=True)
            z = z + jnp.exp(tau - m0)
            cv = jnp.where(cv == tau, -jnp.inf, cv)
        row = lambda k: bt[k:k + 1, :]
        ge = lambda v: (s0 + v) >= tau
        c8 = ge(row(7))
        c4 = ge(jnp.where(c8, row(11), row(3)))
        c2 = ge(jnp.where(c8, jnp.where(c4, row(13), row(9)), jnp.where(c4, row(5), row(1))))
        c1 = ge(jnp.where(c8,
                          jnp.where(c4, jnp.where(c2, row(14), row(12)), jnp.where(c2, row(10), row(8))),
                          jnp.where(c4, jnp.where(c2, row(6), row(4)), jnp.where(c2, row(2), row(0)))))
        cnt = (jnp.where(c8, 8.0, 0.0) + jnp.where(c4, 4.0, 0.0) + jnp.where(c2, 2.0, 0.0)
               + jnp.where(c1, 1.0, 0.0) + jnp.where(ge(row(15)), 1.0, 0.0))
        l_ref[0, :, ls] = _dup_bf16_words(cnt)
        ea_ref[0, :, ls] = _dup_bf16_words(jnp.exp(s0 - atop[0:1, ls]) * (1.0 / z))
        rb_ref[0, :, ls] = rank.astype(BF16)
        eb_ref[0, :, ls] = jnp.exp(s1 - bt[0:1, :]).astype(BF16)


def _peer_score_call(h2t, wq_t, subkeys):
    d, t_all = h2t.shape
    tm = TM_PEER
    ncand = PEER_TOPK + (PEER_TOPK // 2) ** 2
    tok = lambda i, h: (h, 0, i)
    return pl.pallas_call(
        _peer_score_kernel,
        grid=(t_all // tm, PEER_HEADS),
        in_specs=[
            pl.BlockSpec((d, tm), lambda i, h: (0, i)),
            pl.BlockSpec((PEER_QDIM, d), lambda i, h: (h, 0)),
            pl.BlockSpec((2, PEER_NKEYS, PEER_QDIM // 2), lambda i, h: (0, 0, 0)),
        ],
        out_specs=[pl.BlockSpec((1, PEER_NKEYS, tm), tok)] * 4,
        out_shape=[
            jax.ShapeDtypeStruct((PEER_HEADS, PEER_NKEYS, t_all), jnp.uint32),
            jax.ShapeDtypeStruct((PEER_HEADS, PEER_NKEYS, t_all), jnp.uint32),
            jax.ShapeDtypeStruct((PEER_HEADS, PEER_NKEYS, t_all), BF16),
            jax.ShapeDtypeStruct((PEER_HEADS, PEER_NKEYS, t_all), BF16),
        ],
        scratch_shapes=[
            pltpu.VMEM((2, PEER_NKEYS, tm), F32),
            pltpu.VMEM((PEER_TOPK, tm), F32),
            pltpu.VMEM((PEER_TOPK, tm), F32),
            pltpu.VMEM((ncand, tm), F32),
        ],
        compiler_params=_cparams(("arbitrary", "arbitrary")),
        name="peer_scores",
    )(h2t, wq_t, subkeys)


def _peer_dense_kernel(h_ref, u_ref, vt_ref, l_ref, ea_ref, rb_ref, eb_ref, o_ref, act0, act1, w0, w1):
    m = pl.program_id(1)
    n_tiles = pl.num_programs(1) - 2
    tm = TM_PEER
    nk = PEER_NKEYS
    rows_per_step = TE_PEER // nk
    jq_rows = 32
    n_jq = nk // jq_rows

    @pl.when(m == 0)
    def _():
        o_ref[...] = jnp.zeros(o_ref.shape, F32)
        for r in (act0, act1):
            r[...] = jnp.zeros(r.shape, F32)
        for r in (w0, w1):
            r[...] = jnp.zeros(r.shape, BF16)

    valid_b = jnp.logical_and(m >= 1, m <= n_tiles)
    sqrt_half = np.float32(np.sqrt(0.5))

    def stages(act_a, w_c, act_b, w_b):
        n_tc = tm // 128

        def stage_a(n, r):
            cs = slice(n * 256, (n + 1) * 256)
            rs = slice(r * 128, (r + 1) * 128)
            act_a[rs, cs] = _dot(u_ref[rs, :], h_ref[:, cs])

        def stage_c(q):
            rs = slice(q * 128, (q + 1) * 128)
            o_ref[rs, :] += _dot(vt_ref[rs, :], w_c[...])

        def stage_b(tc, jq):
            ls = slice(tc * 128, (tc + 1) * 128)
            js = slice(jq * (jq_rows // 16), (jq + 1) * (jq_rows // 16))
            g = [jnp.zeros((jq_rows // 16, 16, 128), BF16) for _ in range(rows_per_step)]
            for h in range(PEER_HEADS):
                rk = rb_ref[h, js, :, ls]
                ee = eb_ref[h, js, :, ls]
                for ii in range(rows_per_step):
                    l_b = pltpu.bitcast(jnp.broadcast_to(l_ref[h, ii:ii + 1, ls], (8, 128)), BF16)
                    e_b = pltpu.bitcast(jnp.broadcast_to(ea_ref[h, ii:ii + 1, ls], (8, 128)), BF16)
                    g[ii] = g[ii] + jnp.where(rk < l_b, ee * e_b, jnp.zeros((), BF16))
            for ii in range(rows_per_step):
                r0 = ii * nk + jq * jq_rows
                xa = act_b[r0:r0 + jq_rows, ls]
                ge = (0.5 * xa * (1.0 + lax.erf(xa * sqrt_half))).astype(BF16)
                wv = jnp.where(valid_b, ge * g[ii].reshape(jq_rows, 128), jnp.zeros((), BF16))
                w_b[r0:r0 + jq_rows, ls] = wv

        mxu = []
        for q in range(8):
            mxu += [functools.partial(stage_a, q // 4, q % 4), functools.partial(stage_c, 2 * q),
                    functools.partial(stage_c, 2 * q + 1)]
        blocks = [(tc, jq) for tc in range(n_tc) for jq in range(n_jq)]
        mi = 0
        for bi, (tc, jq) in enumerate(blocks):
            while mi < len(mxu) and mi * len(blocks) <= bi * len(mxu):
                mxu[mi]()
                mi += 1
            stage_b(tc, jq)
        while mi < len(mxu):
            mxu[mi]()
            mi += 1

    @pl.when(m % 2 == 0)
    def _():
        stages(act0, w0, act1, w1)

    @pl.when(m % 2 == 1)
    def _():
        stages(act1, w1, act0, w0)


def _peer_dense_call(hb, u_bf, vt_bf, l_pk, ea_pk, rb, eb):
    d, t_all = hb.shape
    ne = u_bf.shape[0]
    tm, te = TM_PEER, TE_PEER
    n_tiles = ne // te
    rows = te // PEER_NKEYS
    row_tile = lambda i, m: (0, jnp.clip(m - 1, 0, n_tiles - 1), 0, i)
    all_j = lambda i, m: (0, 0, 0, i)
    l4 = l_pk.reshape(PEER_HEADS, n_tiles, rows, t_all)
    ea4 = ea_pk.reshape(PEER_HEADS, n_tiles, rows, t_all)
    rb4 = rb.reshape(PEER_HEADS, PEER_NKEYS // 16, 16, t_all)
    eb4 = eb.reshape(PEER_HEADS, PEER_NKEYS // 16, 16, t_all)
    return pl.pallas_call(
        _peer_dense_kernel,
        grid=(t_all // tm, n_tiles + 2),
        in_specs=[
            pl.BlockSpec((d, tm), lambda i, m: (0, i)),
            pl.BlockSpec((te, d), lambda i, m: (jnp.minimum(m, n_tiles - 1), 0)),
            pl.BlockSpec((d, te), lambda i, m: (0, jnp.maximum(m - 2, 0))),
            pl.BlockSpec((PEER_HEADS, None, rows, tm), row_tile),
            pl.BlockSpec((PEER_HEADS, None, rows, tm), row_tile),
            pl.BlockSpec((PEER_HEADS, PEER_NKEYS // 16, 16, tm), all_j),
            pl.BlockSpec((PEER_HEADS, PEER_NKEYS // 16, 16, tm), all_j),
        ],
        out_specs=pl.BlockSpec((d, tm), lambda i, m: (0, i)),
        out_shape=jax.ShapeDtypeStruct((d, t_all), F32),
        scratch_shapes=[
            pltpu.VMEM((te, tm), F32),
            pltpu.VMEM((te, tm), F32),
            pltpu.VMEM((te, tm), BF16),
            pltpu.VMEM((te, tm), BF16),
        ],
        compiler_params=_cparams(("arbitrary", "arbitrary")),
        name="peer_dense",
    )(hb, u_bf, vt_bf, l4, ea4, rb4, eb4)


def _peer_out_kernel(ft_ref, x1_ref, gf_ref, lg_ref, lb_ref, o_ref, *, tiles_per_seq, nb):
    i = pl.program_id(0)
    r = jnp.minimum(i // tiles_per_seq, nb)
    y = DEEPNORM_ALPHA * x1_ref[...] + gf_ref[pl.ds(r, 1), :] * jnp.transpose(ft_ref[...])
    o_ref[...] = _layer_norm_rows(y, lg_ref[...], lb_ref[...])


def _peer_out_call(ffn_t, x1, mod_l, ln_g, ln_b, *, seq, nb):
    t_all, d = x1.shape
    tm = TM_OUT
    kern = functools.partial(_peer_out_kernel, tiles_per_seq=seq // tm, nb=nb)
    return pl.pallas_call(
        kern,
        grid=(t_all // tm,),
        in_specs=[
            pl.BlockSpec((d, tm), lambda i: (0, i)),
            pl.BlockSpec((tm, d), lambda i: (i, 0)),
            pl.BlockSpec((8, d), lambda i: (0, 5)),
            pl.BlockSpec((1, d), lambda i: (0, 0)),
            pl.BlockSpec((1, d), lambda i: (0, 0)),
        ],
        out_specs=pl.BlockSpec((tm, d), lambda i: (i, 0)),
        out_shape=jax.ShapeDtypeStruct((t_all, d), F32),
        compiler_params=_cparams(("arbitrary",)),
        name="peer_out_ln",
    )(ffn_t, x1, mod_l, ln_g, ln_b)


def _ab_columns(w_tail):
    depth, d, _ = w_tail.shape
    hb = GDN_HB
    n_hg = N_HEADS // hb
    wa = w_tail[:, :, :2 * N_HEADS].reshape(depth, d, 2, n_hg, hb)
    wb = w_tail[:, :, 2 * N_HEADS:].reshape(depth, d, 2, n_hg, hb)

    def lay(w):
        w = w.transpose(0, 1, 3, 2, 4).reshape(depth, d, n_hg, 2 * hb)
        w = jnp.pad(w, ((0, 0), (0, 0), (0, 0), (0, HEAD_DIM - 2 * hb)))
        return w.reshape(depth, d, n_hg * HEAD_DIM)

    return jnp.concatenate([lay(wa), lay(wb)], axis=-1)


def _head_param_lanes(p):
    depth = p.shape[0]
    hb = GDN_HB
    n_hg = N_HEADS // hb
    p = p.reshape(depth, 2, n_hg, hb).transpose(0, 2, 1, 3).reshape(depth, n_hg, 2 * hb)
    p = jnp.pad(p, ((0, 0), (0, 0), (0, HEAD_DIM - 2 * hb)))
    return p.reshape(depth, 1, n_hg * HEAD_DIM)


def kernel(x, c, ctx, c_ctx, w_mod, b_mod, w_in, w_out, na_rpb, gdn_conv, gdn_a_log, gdn_dt_bias, gdn_norm_g,
           peer_wq, peer_subkeys, peer_u, peer_v, ln_g, ln_b):
    nb, seq, d = x.shape
    ctx_len = ctx.shape[1]
    depth = w_mod.shape[0]
    assert d == D_MODEL and nb + 1 <= 8
    assert seq % (NA_QROWS * GRID_W) == 0 and seq % TM_PROJ == 0
    t_lat = nb * seq
    dims = dict(nb=nb, seq=seq, ctx_len=ctx_len)

    xa = jnp.concatenate([x.reshape(t_lat, d), ctx.reshape(nb * ctx_len, d)], axis=0)
    cs = jnp.concatenate([c, c_ctx[None, :], jnp.zeros((8 - nb - 1, d), F32)], axis=0)
    mod = _mod_call(cs, w_mod, b_mod)

    w_main = w_in[:, :, :P_MAIN_W].astype(BF16)
    w_ab = _ab_columns(w_in[:, :, P_MAIN_W:]).astype(BF16)
    w_out_bf = w_out.astype(BF16)
    alog = _head_param_lanes(gdn_a_log)
    dtb = _head_param_lanes(gdn_dt_bias)
    conv_w = jnp.pad(gdn_conv, ((0, 0), (0, 8 - GDN_CONV), (0, 0)))
    rope_c, rope_s = _rope_tables(seq, ctx_len)
    wq_t = peer_wq.astype(BF16).transpose(0, 2, 1)
    u_bf = peer_u.astype(BF16)
    vt_bf = peer_v.astype(BF16).transpose(0, 2, 1)

    for l in range(depth):
        p_all, pab = _inproj_call(xa, mod[l], w_main[l], w_ab[l], seq=seq, nb=nb)
        bias = _na_bias(na_rpb[l], seq // GRID_W)
        na_lat = _na_call(p_all, bias, **dims)
        na_ctx = _ctx_attn_call(p_all, **dims)
        qn, kn, vn, g, beta = _gdn_prep_call(p_all, pab, conv_w[l], alog[l], dtb[l], rope_c, rope_s, **dims)
        o_f, o_b = _gdn_scan_call(qn, kn, vn, g, beta, **dims)
        x1, h2t = _outproj_call(na_lat, na_ctx, o_f, o_b, p_all, xa, mod[l], w_out_bf[l], gdn_norm_g[l][None, :],
                                ln_g[l, 0][None, :], ln_b[l, 0][None, :], seq=seq, nb=nb,
                                rows=xa.shape[0] if l < depth - 1 else t_lat)
        l_pk, ea_pk, rb, eb = _peer_score_call(h2t, wq_t[l], peer_subkeys[l])
        ffn_t = _peer_dense_call(h2t, u_bf[l], vt_bf[l], l_pk, ea_pk, rb, eb)
        xa = _peer_out_call(ffn_t, x1, mod[l], ln_g[l, 1][None, :], ln_b[l, 1][None, :], seq=seq, nb=nb)
    return xa.reshape(nb, seq, d)
```

```python
import functools

import numpy as np
import jax
import jax.numpy as jnp
from jax import lax
from jax.experimental import pallas as pl
from jax.experimental.pallas import tpu as pltpu

F32 = jnp.float32
BF16 = jnp.bfloat16
HIGHEST = lax.Precision.HIGHEST

D_MODEL = 2048
HEAD_DIM = 128
N_HEADS = 8
GROUP_W = N_HEADS * HEAD_DIM
GRID_W = 64
NA_KR = 8
NA_KC = 16
NA_QROWS = 8
NA_KROWS = 16
NA_SUB = 2
GDN_CHUNK = 64
GDN_CONV = 5
GDN_HB = 8
ROPE_THETA = 10000.0
PEER_HEADS = 8
PEER_TOPK = 16
PEER_NKEYS = 128
PEER_QDIM = 256
N_MOD = 6
LN_EPS = 1e-6
NEG_INF = -1e30
DEPTH_FOR_DEEPNORM = 4
DEEPNORM_ALPHA = (2 * DEPTH_FOR_DEEPNORM) ** 0.25
P_MAIN_W = 7 * GROUP_W
VMEM_LIMIT = 56 * 1024 * 1024

TM_PROJ = 512
TN_PROJ = 1792
TM_OUT = 256
TM_PREP = 256
TM_PEER = 512
TE_PEER = 512


def _cparams(sem):
    return pltpu.CompilerParams(dimension_semantics=sem, vmem_limit_bytes=VMEM_LIMIT)


def _sigmoid(x):
    return 1.0 / (1.0 + jnp.exp(-x))


def _silu(x):
    return x * _sigmoid(x)


def _dot(a, b):
    return jnp.dot(a, b, preferred_element_type=F32)


def _dot_t(a, b):
    return lax.dot_general(a, b, (((1,), (1,)), ((), ())), preferred_element_type=F32)


def _layer_norm_rows(y, g, b):
    mu = jnp.mean(y, axis=-1, keepdims=True)
    yc = y - mu
    var = jnp.mean(yc * yc, axis=-1, keepdims=True)
    return yc * lax.rsqrt(var + LN_EPS) * g + b


def _mod_kernel(c_ref, w_ref, b_ref, o_ref):
    s = _silu(c_ref[...])
    o_ref[0] = jnp.dot(s, w_ref[0], precision=HIGHEST, preferred_element_type=F32) + b_ref[0]


def _mod_call(cs, w_mod, b_mod):
    depth, d, n = w_mod.shape
    tn = 1024
    return pl.pallas_call(
        _mod_kernel,
        grid=(depth, n // tn),
        in_specs=[
            pl.BlockSpec((8, d), lambda l, j: (0, 0)),
            pl.BlockSpec((1, d, tn), lambda l, j: (l, 0, j)),
            pl.BlockSpec((1, 1, tn), lambda l, j: (l, 0, j)),
        ],
        out_specs=pl.BlockSpec((1, 8, tn), lambda l, j: (l, 0, j)),
        out_shape=jax.ShapeDtypeStruct((depth, 8, n), F32),
        compiler_params=_cparams(("arbitrary", "arbitrary")),
        name="adaln_mod",
    )(cs, w_mod, b_mod.reshape(depth, 1, n))


def _inproj_kernel(x_ref, sh_ref, sc_ref, w_ref, wab_ref, p_ref, pab_ref, h_scr, *, tiles_per_seq, nb):
    i = pl.program_id(0)
    j = pl.program_id(1)

    @pl.when(j == 0)
    def _():
        r = jnp.minimum(i // tiles_per_seq, nb)
        sh = sh_ref[pl.ds(r, 1), :]
        sc = sc_ref[pl.ds(r, 1), :]
        hb = (x_ref[...] * (1.0 + sc) + sh).astype(BF16)
        h_scr[...] = hb
        pab_ref[...] = _dot(hb, wab_ref[...])

    p_ref[...] = _dot(h_scr[...], w_ref[...])


def _inproj_call(xa, mod_l, w_main, w_ab, *, seq, nb):
    t_all, d = xa.shape
    n = w_main.shape[1]
    nab = w_ab.shape[1]
    tm, tn = TM_PROJ, TN_PROJ
    kern = functools.partial(_inproj_kernel, tiles_per_seq=seq // tm, nb=nb)
    return pl.pallas_call(
        kern,
        grid=(t_all // tm, n // tn),
        in_specs=[
            pl.BlockSpec((tm, d), lambda i, j: (i, 0)),
            pl.BlockSpec((8, d), lambda i, j: (0, 0)),
            pl.BlockSpec((8, d), lambda i, j: (0, 1)),
            pl.BlockSpec((d, tn), lambda i, j: (0, j)),
            pl.BlockSpec((d, nab), lambda i, j: (0, 0)),
        ],
        out_specs=[
            pl.BlockSpec((tm, tn), lambda i, j: (i, j)),
            pl.BlockSpec((tm, nab), lambda i, j: (i, 0)),
        ],
        out_shape=[
            jax.ShapeDtypeStruct((t_all, n), F32),
            jax.ShapeDtypeStruct((t_all, nab), F32),
        ],
        scratch_shapes=[pltpu.VMEM((tm, d), BF16)],
        compiler_params=_cparams(("arbitrary", "arbitrary")),
        name="in_proj",
    )(xa, mod_l, mod_l, w_main, w_ab)


def _na_bias_index_tables(nrows):
    rq, rk = NA_QROWS, NA_KROWS
    big = 1 << 20
    cfg = [(0, 0, nrows), (8, 4, big), (nrows - rq, nrows - rk, nrows)]
    dr = np.zeros((3, rq, rk), np.int32)
    rv = np.zeros((3, rq, rk), bool)
    for v, (r0, start, nr) in enumerate(cfg):
        r = r0 + np.arange(rq)[:, None]
        kr = start + np.arange(rk)[None, :]
        rs = np.clip(r - NA_KR // 2, 0, nr - NA_KR)
        rv[v] = (kr >= rs) & (kr < rs + NA_KR)
        dr[v] = np.clip(kr - r + NA_KR - 1, 0, 2 * NA_KR - 2)
    qc = np.arange(GRID_W)[:, None]
    kc = np.arange(GRID_W)[None, :]
    ws = np.clip(qc - NA_KC // 2, 0, GRID_W - NA_KC)
    cv = (kc >= ws) & (kc < ws + NA_KC)
    dc = np.clip(kc - qc + NA_KC - 1, 0, 2 * NA_KC - 2).astype(np.int32)
    return dr, rv, dc, cv


def _na_bias(rpb_l, nrows):
    dr, rv, dc, cv = _na_bias_index_tables(nrows)
    tc = jnp.where(cv[None, None], rpb_l[:, :, dc], NEG_INF)
    n_dr = tc.shape[1]
    tc = jnp.concatenate([tc, jnp.full((N_HEADS, 1, GRID_W, GRID_W), NEG_INF, F32)], axis=1)
    blk = tc[:, np.where(rv, dr, n_dr)]
    blk = blk.transpose(0, 1, 2, 4, 3, 5)
    return blk.reshape(N_HEADS, 3, NA_QROWS * GRID_W, NA_KROWS * GRID_W)


def _na_kernel(q_ref, k0, k1, k2, k3, v0, v1, v2, v3, kc_ref, vc_ref, bias_ref, o_ref):
    kb = 4 * GRID_W
    ks = [kr[...].astype(BF16) for kr in (k0, k1, k2, k3, kc_ref)]
    vs = [vr[...].astype(BF16) for vr in (v0, v1, v2, v3, vc_ref)]
    rows = q_ref.shape[0] // NA_SUB
    for sub in range(NA_SUB):
        rs = slice(sub * rows, (sub + 1) * rows)
        q = (q_ref[rs, :] * (HEAD_DIM ** -0.5)).astype(BF16)
        s = [_dot_t(q, ks[j]) + bias_ref[rs, j * kb:(j + 1) * kb] for j in range(4)]
        s.append(_dot_t(q, ks[4]))
        m = s[0].max(axis=-1, keepdims=True)
        for t in s[1:]:
            m = jnp.maximum(m, t.max(axis=-1, keepdims=True))
        p = [jnp.exp(t - m) for t in s]
        l = p[0].sum(axis=-1, keepdims=True)
        for t in p[1:]:
            l = l + t.sum(axis=-1, keepdims=True)
        o = _dot(p[0].astype(BF16), vs[0])
        for t, vv in zip(p[1:], vs[1:]):
            o = o + _dot(t.astype(BF16), vv)
        o_ref[rs, :] = o / l


def _na_call(p_all, bias, *, nb, seq, ctx_len):
    t_all = p_all.shape[0]
    nrows = seq // GRID_W
    nrb = nrows // NA_QROWS
    tq = NA_QROWS * GRID_W
    tk = 4 * GRID_W
    assert ctx_len == tk and nrows >= NA_KROWS
    kblocks_per_seq = seq // tk
    max_sb = kblocks_per_seq - 4
    ctx_blk0 = nb * seq // tk

    def sb(r):
        return jnp.clip(2 * r - 1, 0, max_sb)

    def kspec(j, colbase):
        return pl.BlockSpec((tk, HEAD_DIM), lambda h, b, r: (b * kblocks_per_seq + sb(r) + j, colbase + h))

    def variant(r):
        return jnp.where(r == 0, 0, jnp.where(r == nrb - 1, 2, 1))

    in_specs = [pl.BlockSpec((tq, HEAD_DIM), lambda h, b, r: (b * nrb + r, h))]
    in_specs += [kspec(j, N_HEADS) for j in range(4)]
    in_specs += [kspec(j, 2 * N_HEADS) for j in range(4)]
    in_specs += [
        pl.BlockSpec((tk, HEAD_DIM), lambda h, b, r: (ctx_blk0 + b, N_HEADS + h)),
        pl.BlockSpec((tk, HEAD_DIM), lambda h, b, r: (ctx_blk0 + b, 2 * N_HEADS + h)),
        pl.BlockSpec((None, None, tq, NA_KROWS * GRID_W), lambda h, b, r: (h, variant(r), 0, 0)),
    ]
    return pl.pallas_call(
        _na_kernel,
        grid=(N_HEADS, nb, nrb),
        in_specs=in_specs,
        out_specs=pl.BlockSpec((tq, HEAD_DIM), lambda h, b, r: (b * nrb + r, h)),
        out_shape=jax.ShapeDtypeStruct((nb * seq, GROUP_W), F32),
        compiler_params=_cparams(("arbitrary", "arbitrary", "arbitrary")),
        name="na_attention",
    )(*([p_all] * 11), bias)


def _ctx_attn_kernel(q_ref, k_ref, v_ref, o_ref):
    q = q_ref[...].astype(BF16)
    s = _dot_t(q, k_ref[...].astype(BF16)) * (HEAD_DIM ** -0.5)
    m = s.max(axis=-1, keepdims=True)
    p = jnp.exp(s - m)
    l = p.sum(axis=-1, keepdims=True)
    o_ref[...] = _dot(p.astype(BF16), v_ref[...].astype(BF16)) / l


def _ctx_attn_call(p_all, *, nb, seq, ctx_len):
    blk0 = nb * seq // ctx_len
    return pl.pallas_call(
        _ctx_attn_kernel,
        grid=(nb, N_HEADS),
        in_specs=[
            pl.BlockSpec((ctx_len, HEAD_DIM), lambda b, h: (blk0 + b, h)),
            pl.BlockSpec((ctx_len, HEAD_DIM), lambda b, h: (blk0 + b, N_HEADS + h)),
            pl.BlockSpec((ctx_len, HEAD_DIM), lambda b, h: (blk0 + b, 2 * N_HEADS + h)),
        ],
        out_specs=pl.BlockSpec((ctx_len, HEAD_DIM), lambda b, h: (b, h)),
        out_shape=jax.ShapeDtypeStruct((nb * ctx_len, GROUP_W), F32),
        compiler_params=_cparams(("arbitrary", "arbitrary")),
        name="ctx_attention",
    )(p_all, p_all, p_all)


def _rope_tables(seq, ctx_len):
    t = np.arange(seq)
    row = (t // GRID_W).astype(np.float32)
    col = (t % GRID_W).astype(np.float32)
    n_freq = HEAD_DIM // 4
    inv = (ROPE_THETA ** (-np.arange(n_freq, dtype=np.float32) / n_freq)).astype(np.float32)
    ang = jnp.stack([jnp.asarray(row)[:, None] * inv, jnp.asarray(col)[:, None] * inv], axis=1)
    cos, sin = jnp.cos(ang), jnp.sin(ang)
    c = jnp.concatenate([cos, cos], axis=-1).reshape(seq, HEAD_DIM)
    s = jnp.concatenate([-sin, sin], axis=-1).reshape(seq, HEAD_DIM)
    c = jnp.concatenate([c, jnp.ones((ctx_len, HEAD_DIM), F32)], axis=0)
    s = jnp.concatenate([s, jnp.zeros((ctx_len, HEAD_DIM), F32)], axis=0)
    return c, s


def _gdn_prep_kernel(cur_ref, prev_ref, next_ref, pab_ref, cw_ref, alog_ref, dtb_ref, rc_ref, rs_ref,
                     q_ref, k_ref, v_ref, g_ref, b_ref, ext_ref, *, n_lat_tiles, tps, nab_half):
    i = pl.program_id(0)
    tm = TM_PREP
    is_ctx = i >= n_lat_tiles
    first = jnp.logical_or(is_ctx, i % tps == 0)
    last = jnp.logical_or(is_ctx, i % tps == tps - 1)
    ext_ref[8:8 + tm, :] = cur_ref[...]
    ext_ref[0:8, :] = jnp.where(first, 0.0, prev_ref[...])
    ext_ref[8 + tm:16 + tm, :] = jnp.where(last, 0.0, next_ref[...])

    lane = lax.broadcasted_iota(jnp.int32, (tm, HEAD_DIM), 1)
    half0 = (lane % (HEAD_DIM // 2)) < (HEAD_DIM // 4)
    rc = rc_ref[...]
    rs = rs_ref[...]
    base = 8 - GDN_CONV // 2
    outs = (q_ref, k_ref, v_ref)
    for part in range(3):
        for h in range(N_HEADS):
            c0 = part * GROUP_W + h * HEAD_DIM
            acc = cw_ref[0:1, c0:c0 + HEAD_DIM] * ext_ref[base:base + tm, c0:c0 + HEAD_DIM]
            for t in range(1, GDN_CONV):
                acc = acc + cw_ref[t:t + 1, c0:c0 + HEAD_DIM] * ext_ref[base + t:base + t + tm, c0:c0 + HEAD_DIM]
            y = _silu(acc)
            if part < 2:
                y = y * lax.rsqrt(jnp.sum(y * y, axis=-1, keepdims=True) + 1e-6)
                partner = jnp.where(half0, pltpu.roll(y, HEAD_DIM - HEAD_DIM // 4, 1),
                                    pltpu.roll(y, HEAD_DIM // 4, 1))
                y = y * rc + partner * rs
                if part == 0:
                    y = y * (HEAD_DIM ** -0.5)
            outs[part][:, h * HEAD_DIM:(h + 1) * HEAD_DIM] = y

    a = pab_ref[:, 0:nab_half] + dtb_ref[...]
    softplus = jnp.maximum(a, 0.0) + jnp.log1p(jnp.exp(-jnp.abs(a)))
    g_ref[...] = -jnp.exp(alog_ref[...]) * softplus
    b_ref[...] = _sigmoid(pab_ref[:, nab_half:2 * nab_half])


def _gdn_prep_call(p_all, pab, conv_w, alog, dtb, rope_c, rope_s, *, nb, seq, ctx_len):
    t_all = p_all.shape[0]
    tm = TM_PREP
    assert ctx_len == tm
    n_tiles = t_all // tm
    n_lat_tiles = nb * seq // tm
    tps = seq // tm
    nab_half = pab.shape[1] // 2
    qkv_w = 3 * GROUP_W
    last8 = t_all // 8 - 1
    kern = functools.partial(_gdn_prep_kernel, n_lat_tiles=n_lat_tiles, tps=tps, nab_half=nab_half)

    def rope_idx(i):
        return jnp.where(i < n_lat_tiles, i % tps, tps)

    return pl.pallas_call(
        kern,
        grid=(n_tiles,),
        in_specs=[
            pl.BlockSpec((tm, qkv_w), lambda i: (i, 1)),
            pl.BlockSpec((8, qkv_w), lambda i: (jnp.maximum(i * (tm // 8) - 1, 0), 1)),
            pl.BlockSpec((8, qkv_w), lambda i: (jnp.minimum((i + 1) * (tm // 8), last8), 1)),
            pl.BlockSpec((tm, 2 * nab_half), lambda i: (i, 0)),
            pl.BlockSpec((8, qkv_w), lambda i: (0, 0)),
            pl.BlockSpec((1, nab_half), lambda i: (0, 0)),
            pl.BlockSpec((1, nab_half), lambda i: (0, 0)),
            pl.BlockSpec((tm, HEAD_DIM), lambda i: (rope_idx(i), 0)),
            pl.BlockSpec((tm, HEAD_DIM), lambda i: (rope_idx(i), 0)),
        ],
        out_specs=[
            pl.BlockSpec((tm, GROUP_W), lambda i: (i, 0)),
            pl.BlockSpec((tm, GROUP_W), lambda i: (i, 0)),
            pl.BlockSpec((tm, GROUP_W), lambda i: (i, 0)),
            pl.BlockSpec((tm, nab_half), lambda i: (i, 0)),
            pl.BlockSpec((tm, nab_half), lambda i: (i, 0)),
        ],
        out_shape=[
            jax.ShapeDtypeStruct((t_all, GROUP_W), F32),
            jax.ShapeDtypeStruct((t_all, GROUP_W), F32),
            jax.ShapeDtypeStruct((t_all, GROUP_W), F32),
            jax.ShapeDtypeStruct((t_all, nab_half), F32),
            jax.ShapeDtypeStruct((t_all, nab_half), F32),
        ],
        scratch_shapes=[pltpu.VMEM((tm + 16, qkv_w), F32)],
        compiler_params=_cparams(("arbitrary",)),
        name="gdn_prep",
    )(p_all, p_all, p_all, pab, conv_w, alog, dtb, rope_c, rope_s)


def _unit_tri_inverse(ns, orders):
    c = GDN_CHUNK
    ds = []
    for n, (row, col) in zip(ns, orders):
        m1 = jnp.logical_and(jnp.logical_and((row >> 1) == (col >> 1), (row & 1) == 1), (col & 1) == 0)
        ds.append((row == col).astype(F32) - jnp.where(m1, n, 0.0))
    k = 2
    while k < c:
        sh = int(np.log2(2 * k))
        dsp, xs = [], []
        for n, d, (row, col) in zip(ns, ds, orders):
            mk = jnp.logical_and((row >> sh) == (col >> sh),
                                 jnp.logical_and((row & (2 * k - 1)) >= k, (col & (2 * k - 1)) < k))
            lk = jnp.where(mk, n, 0.0).astype(BF16)
            dh = d.astype(BF16)
            dsp.append(dh)
            xs.append(_dot(lk, dh))
        ds = [d - _dot(dh, x.astype(BF16)) for d, dh, x in zip(ds, dsp, xs)]
        k *= 2
    return ds


def _gdn_scan_kernel(qf, kf, vf, gf, bf, qb, kb, vb, gb, bb, of_ref, ob_ref, st_ref, *, hb):
    s = pl.program_id(2)
    c = GDN_CHUNK

    @pl.when(s == 0)
    def _():
        st_ref[...] = jnp.zeros(st_ref.shape, F32)

    ri = lax.broadcasted_iota(jnp.int32, (c, c), 0)
    ci = lax.broadcasted_iota(jnp.int32, (c, c), 1)
    zpad = jnp.zeros((HEAD_DIM - c, HEAD_DIM), F32)
    ch = []
    for d, (q_ref, k_ref, v_ref, g_ref, be_ref, o_ref) in enumerate(
            ((qf, kf, vf, gf, bf, of_ref), (qb, kb, vb, gb, bb, ob_ref))):
        incl = (ri >= ci) if d == 0 else (ri <= ci)
        gc_all = jnp.dot(incl.astype(F32), g_ref[...], precision=HIGHEST, preferred_element_type=F32)
        gc_t = jnp.transpose(jnp.concatenate([gc_all, zpad], axis=0))
        beta_all = be_ref[...]
        for hh in range(hb):
            ln = d * hb + hh
            sl = slice(hh * HEAD_DIM, (hh + 1) * HEAD_DIM)
            ch.append(dict(
                ln=ln, sl=sl, o_ref=o_ref, incl=incl,
                strict=(ri > ci) if d == 0 else (ri < ci),
                order=(ri, ci) if d == 0 else (ci, ri),
                last=c - 1 if d == 0 else 0,
                gcc=gc_all[:, ln:ln + 1], grow=gc_t[ln:ln + 1, 0:c], beta=beta_all[:, ln:ln + 1],
                q=q_ref[:, sl], k=k_ref[:, sl], v=v_ref[:, sl]))
    for t in ch:
        t["decay"] = jnp.where(t["incl"], jnp.exp(jnp.where(t["incl"], t["gcc"] - t["grow"], 0.0)), 0.0)
        t["kbeta"] = t["k"] * t["beta"]
    for t in ch:
        t["kk"] = _dot_t(t["kbeta"], t["k"])
    for t in ch:
        t["qk"] = _dot_t(t["q"], t["k"]) * t["decay"]
    ns = [jnp.where(t["strict"], t["kk"] * t["decay"], 0.0) for t in ch]
    tinvs = _unit_tri_inverse(ns, [t["order"] for t in ch])
    for t, tinv in zip(ch, tinvs):
        eg = jnp.exp(t["gcc"])
        t["eg"] = eg
        rhs = jnp.concatenate([t["v"] * t["beta"], t["kbeta"] * eg], axis=1)
        t["sol"] = _dot(tinv.astype(BF16), rhs.astype(BF16))
    for t in ch:
        t["state"] = st_ref[t["ln"]]
        t["v_new"] = t["sol"][:, :HEAD_DIM] - _dot(t["sol"][:, HEAD_DIM:], t["state"])
    for t in ch:
        t["o_ref"][:, t["sl"]] = _dot(t["q"] * t["eg"], t["state"]) + _dot(t["qk"], t["v_new"])
    for t in ch:
        glast = t["gcc"][t["last"]:t["last"] + 1, :]
        kd_t = jnp.transpose(jnp.concatenate([t["k"] * jnp.exp(glast - t["gcc"]), zpad], axis=0))[:, 0:c]
        st_ref[t["ln"]] = t["state"] * jnp.exp(glast) + _dot(kd_t, t["v_new"])


def _gdn_scan_call(qn, kn, vn, g, beta, *, nb, seq, ctx_len):
    t_all = qn.shape[0]
    hb = GDN_HB
    c = GDN_CHUNK
    ncc = ctx_len // c
    ncl = seq // c
    ctx0 = nb * ncl
    n_hg = N_HEADS // hb

    def row_f(b, s):
        return jnp.where(s < ncc, ctx0 + b * ncc + s, b * ncl + (s - ncc))

    def row_b(b, s):
        return jnp.where(s < ncc, ctx0 + b * ncc + (ncc - 1 - s), b * ncl + (ncl - 1 - (s - ncc)))

    def specs(rowfn):
        big = pl.BlockSpec((c, hb * HEAD_DIM), lambda b, hg, s: (rowfn(b, s), hg))
        small = pl.BlockSpec((c, HEAD_DIM), lambda b, hg, s: (rowfn(b, s), hg))
        return [big, big, big, small, small]

    out_f = pl.BlockSpec((c, hb * HEAD_DIM), lambda b, hg, s: (row_f(b, s), hg))
    out_b = pl.BlockSpec((c, hb * HEAD_DIM), lambda b, hg, s: (row_b(b, s), hg))
    return pl.pallas_call(
        functools.partial(_gdn_scan_kernel, hb=hb),
        grid=(nb, n_hg, ncc + ncl),
        in_specs=specs(row_f) + specs(row_b),
        out_specs=[out_f, out_b],
        out_shape=[jax.ShapeDtypeStruct((t_all, GROUP_W), F32)] * 2,
        scratch_shapes=[pltpu.VMEM((2 * hb, HEAD_DIM, HEAD_DIM), F32)],
        compiler_params=_cparams(("arbitrary", "arbitrary", "arbitrary")),
        name="gdn_scan",
    )(qn, kn, vn, g, beta, qn, kn, vn, g, beta)


def _outproj_kernel(nal_ref, nac_ref, of_ref, ob_ref, z_ref, x_ref, ga_ref, shf_ref, scf_ref, w_ref, ng_ref, lg_ref,
                    lb_ref, x1_ref, h2t_ref, gdn_scr, *, tiles_per_seq, nb):
    i = pl.program_id(0)
    r = jnp.minimum(i // tiles_per_seq, nb)
    na = jnp.where(i < nb * tiles_per_seq, nal_ref[...], nac_ref[...])
    o = of_ref[...] + ob_ref[...]
    ng = ng_ref[...]
    for h in range(N_HEADS):
        sl = slice(h * HEAD_DIM, (h + 1) * HEAD_DIM)
        oh = o[:, sl]
        oh = oh * lax.rsqrt(jnp.mean(oh * oh, axis=-1, keepdims=True) + 1e-6) * ng
        gdn_scr[:, sl] = (oh * _silu(z_ref[:, sl])).astype(BF16)
    mix = _dot(na.astype(BF16), w_ref[0:GROUP_W, :]) + _dot(gdn_scr[...], w_ref[GROUP_W:2 * GROUP_W, :])
    y = DEEPNORM_ALPHA * x_ref[...] + ga_ref[pl.ds(r, 1), :] * mix
    x1 = _layer_norm_rows(y, lg_ref[...], lb_ref[...])
    x1_ref[...] = x1
    h2 = x1 * (1.0 + scf_ref[pl.ds(r, 1), :]) + shf_ref[pl.ds(r, 1), :]
    h2t_ref[...] = jnp.transpose(h2).astype(BF16)


def _outproj_call(na_lat, na_ctx, o_f, o_b, p_all, xa, mod_l, w_out, norm_g, ln_g, ln_b, *, seq, nb, rows):
    t_all, d = rows, xa.shape[1]
    tm = TM_OUT
    n_lat_tiles = nb * seq // tm
    kern = functools.partial(_outproj_kernel, tiles_per_seq=seq // tm, nb=nb)
    row = lambda i: (i, 0)
    const = lambda i: (0, 0)
    return pl.pallas_call(
        kern,
        grid=(t_all // tm,),
        in_specs=[
            pl.BlockSpec((tm, GROUP_W), lambda i: (jnp.minimum(i, n_lat_tiles - 1), 0)),
            pl.BlockSpec((tm, GROUP_W), lambda i: (jnp.maximum(i - n_lat_tiles, 0), 0)),
            pl.BlockSpec((tm, GROUP_W), row),
            pl.BlockSpec((tm, GROUP_W), row),
            pl.BlockSpec((tm, GROUP_W), lambda i: (i, 6)),
            pl.BlockSpec((tm, d), row),
            pl.BlockSpec((8, d), lambda i: (0, 2)),
            pl.BlockSpec((8, d), lambda i: (0, 3)),
            pl.BlockSpec((8, d), lambda i: (0, 4)),
            pl.BlockSpec((d, d), const),
            pl.BlockSpec((1, HEAD_DIM), const),
            pl.BlockSpec((1, d), const),
            pl.BlockSpec((1, d), const),
        ],
        out_specs=[
            pl.BlockSpec((tm, d), row),
            pl.BlockSpec((d, tm), lambda i: (0, i)),
        ],
        out_shape=[
            jax.ShapeDtypeStruct((t_all, d), F32),
            jax.ShapeDtypeStruct((d, t_all), BF16),
        ],
        scratch_shapes=[pltpu.VMEM((tm, GROUP_W), BF16)],
        compiler_params=_cparams(("arbitrary",)),
        name="out_proj_ln",
    )(na_lat, na_ctx, o_f, o_b, p_all, xa, mod_l, mod_l, mod_l, w_out, norm_g, ln_g, ln_b)


def _dup_bf16_words(x):
    u = pltpu.bitcast(x.astype(BF16).astype(F32), jnp.uint32)
    return u | (u >> 16)


def _peer_score_kernel(ht_ref, w_ref, sk_ref, l_ref, ea_ref, rb_ref, eb_ref, s_scr, atop, btop, cand):
    tm = TM_PEER
    qt = _dot(w_ref[...], ht_ref[...])
    half = PEER_QDIM // 2
    s_scr[0] = _dot(sk_ref[0], qt[0:half])
    s_scr[1] = _dot(sk_ref[1], qt[half:2 * half])
    n_half = PEER_TOPK // 2
    for tc in range(tm // 128):
        ls = slice(tc * 128, (tc + 1) * 128)
        s0 = s_scr[0, :, ls]
        s1 = s_scr[1, :, ls]
        sc = s0
        for k in range(PEER_TOPK):
            m = jnp.max(sc, axis=0, keepdims=True)
            atop[k:k + 1, ls] = m
            sc = jnp.where(sc == m, -jnp.inf, sc)
        sc = s1
        rank = jnp.full(s1.shape, float(PEER_TOPK), F32)
        for k in range(PEER_TOPK):
            m = jnp.max(sc, axis=0, keepdims=True)
            btop[k:k + 1, ls] = m
            hit = sc == m
            rank = jnp.minimum(rank, jnp.where(hit, float(k), float(PEER_TOPK)))
            sc = jnp.where(hit, -jnp.inf, sc)
        bt = btop[:, ls]
        cand[0:PEER_TOPK, ls] = atop[0:1, ls] + bt
        for i in range(1, n_half):
            r0 = PEER_TOPK + (i - 1) * n_half
            cand[r0:r0 + n_half, ls] = atop[i:i + 1, ls] + bt[0:n_half, :]
        r0 = PEER_TOPK + (n_half - 1) * n_half
        cand[r0:r0 + n_half, ls] = atop[n_half:PEER_TOPK, ls] + bt[0:1, :]
        cv = cand[:, ls]
        m0 = jnp.max(cv, axis=0, keepdims=True)
        z = jnp.zeros_like(m0)
        tau = m0
        for k in range(PEER_TOPK):
            tau = jnp.max(cv, axis=0, keepdims=True)
            z = z + jnp.exp(tau - m0)
            cv = jnp.where(cv == tau, -jnp.inf, cv)
        row = lambda k: bt[k:k + 1, :]
        ge = lambda v: (s0 + v) >= tau
        c8 = ge(row(7))
        c4 = ge(jnp.where(c8, row(11), row(3)))
        c2 = ge(jnp.where(c8, jnp.where(c4, row(13), row(9)), jnp.where(c4, row(5), row(1))))
        c1 = ge(jnp.where(c8,
                          jnp.where(c4, jnp.where(c2, row(14), row(12)), jnp.where(c2, row(10), row(8))),
                          jnp.where(c4, jnp.where(c2, row(6), row(4)), jnp.where(c2, row(2), row(0)))))
        cnt = (jnp.where(c8, 8.0, 0.0) + jnp.where(c4, 4.0, 0.0) + jnp.where(c2, 2.0, 0.0)
               + jnp.where(c1, 1.0, 0.0) + jnp.where(ge(row(15)), 1.0, 0.0))
        l_ref[0, :, ls] = _dup_bf16_words(cnt)
        ea_ref[0, :, ls] = _dup_bf16_words(jnp.exp(s0 - atop[0:1, ls]) * (1.0 / z))
        rb_ref[0, :, ls] = rank.astype(BF16)
        eb_ref[0, :, ls] = jnp.exp(s1 - bt[0:1, :]).astype(BF16)


def _peer_score_call(h2t, wq_t, subkeys):
    d, t_all = h2t.shape
    tm = TM_PEER
    ncand = PEER_TOPK + (PEER_TOPK // 2) ** 2
    tok = lambda i, h: (h, 0, i)
    return pl.pallas_call(
        _peer_score_kernel,
        grid=(t_all // tm, PEER_HEADS),
        in_specs=[
            pl.BlockSpec((d, tm), lambda i, h: (0, i)),
            pl.BlockSpec((PEER_QDIM, d), lambda i, h: (h, 0)),
            pl.BlockSpec((2, PEER_NKEYS, PEER_QDIM // 2), lambda i, h: (0, 0, 0)),
        ],
        out_specs=[pl.BlockSpec((1, PEER_NKEYS, tm), tok)] * 4,
        out_shape=[
            jax.ShapeDtypeStruct((PEER_HEADS, PEER_NKEYS, t_all), jnp.uint32),
            jax.ShapeDtypeStruct((PEER_HEADS, PEER_NKEYS, t_all), jnp.uint32),
            jax.ShapeDtypeStruct((PEER_HEADS, PEER_NKEYS, t_all), BF16),
            jax.ShapeDtypeStruct((PEER_HEADS, PEER_NKEYS, t_all), BF16),
        ],
        scratch_shapes=[
            pltpu.VMEM((2, PEER_NKEYS, tm), F32),
            pltpu.VMEM((PEER_TOPK, tm), F32),
            pltpu.VMEM((PEER_TOPK, tm), F32),
            pltpu.VMEM((ncand, tm), F32),
        ],
        compiler_params=_cparams(("arbitrary", "arbitrary")),
        name="peer_scores",
    )(h2t, wq_t, subkeys)


def _peer_dense_kernel(h_ref, u_ref, vt_ref, l_ref, ea_ref, rb_ref, eb_ref, o_ref, act0, act1, w0, w1, *,
                       n_steps, n_tiles):
    s = pl.program_id(0)
    m_c = jnp.clip(s - 2, 0, n_steps - 1) % n_tiles
    tm = TM_PEER
    nk = PEER_NKEYS
    rows_per_step = TE_PEER // nk
    jq_rows = 32
    n_jq = nk // jq_rows

    @pl.when(s == 0)
    def _():
        for r in (act0, act1):
            r[...] = jnp.zeros(r.shape, F32)
        for r in (w0, w1):
            r[...] = jnp.zeros(r.shape, BF16)

    @pl.when(m_c == 0)
    def _():
        o_ref[...] = jnp.zeros(o_ref.shape, F32)

    valid_b = jnp.logical_and(s >= 1, s <= n_steps)
    sqrt_half = np.float32(np.sqrt(0.5))

    def stages(act_a, w_c, act_b, w_b):
        n_tc = tm // 128

        def stage_a(n, r):
            cs = slice(n * 256, (n + 1) * 256)
            rs = slice(r * 128, (r + 1) * 128)
            act_a[rs, cs] = _dot(u_ref[rs, :], h_ref[:, cs])

        def stage_c(q):
            rs = slice(q * 128, (q + 1) * 128)
            o_ref[rs, :] += _dot(vt_ref[rs, :], w_c[...])

        def stage_b(tc, jq):
            ls = slice(tc * 128, (tc + 1) * 128)
            js = slice(jq * (jq_rows // 16), (jq + 1) * (jq_rows // 16))
            g = [jnp.zeros((jq_rows // 16, 16, 128), BF16) for _ in range(rows_per_step)]
            for h in range(PEER_HEADS):
                rk = rb_ref[h, js, :, ls]
                ee = eb_ref[h, js, :, ls]
                for ii in range(rows_per_step):
                    l_b = pltpu.bitcast(jnp.broadcast_to(l_ref[h, ii:ii + 1, ls], (8, 128)), BF16)
                    e_b = pltpu.bitcast(jnp.broadcast_to(ea_ref[h, ii:ii + 1, ls], (8, 128)), BF16)
                    g[ii] = g[ii] + jnp.where(rk < l_b, ee * e_b, jnp.zeros((), BF16))
            for ii in range(rows_per_step):
                r0 = ii * nk + jq * jq_rows
                xa = act_b[r0:r0 + jq_rows, ls]
                ge = (0.5 * xa * (1.0 + lax.erf(xa * sqrt_half))).astype(BF16)
                wv = jnp.where(valid_b, ge * g[ii].reshape(jq_rows, 128), jnp.zeros((), BF16))
                w_b[r0:r0 + jq_rows, ls] = wv

        mxu = []
        for q in range(8):
            mxu += [functools.partial(stage_a, q // 4, q % 4), functools.partial(stage_c, 2 * q),
                    functools.partial(stage_c, 2 * q + 1)]
        blocks = [(tc, jq) for tc in range(n_tc) for jq in range(n_jq)]
        mi = 0
        for bi, (tc, jq) in enumerate(blocks):
            while mi < len(mxu) and mi * len(blocks) <= bi * len(mxu):
                mxu[mi]()
                mi += 1
            stage_b(tc, jq)
        while mi < len(mxu):
            mxu[mi]()
            mi += 1

    @pl.when(s % 2 == 0)
    def _():
        stages(act0, w0, act1, w1)

    @pl.when(s % 2 == 1)
    def _():
        stages(act1, w1, act0, w0)


def _peer_dense_call(hb, u_bf, vt_bf, l_pk, ea_pk, rb, eb):
    d, t_all = hb.shape
    ne = u_bf.shape[0]
    tm, te = TM_PEER, TE_PEER
    n_tiles = ne // te
    n_steps = (t_all // tm) * n_tiles
    rows = te // PEER_NKEYS

    def lag(k):
        return lambda s: divmod(jnp.clip(s - k, 0, n_steps - 1), n_tiles)

    tile_a, tile_b, tile_c = lag(0), lag(1), lag(2)
    l4 = l_pk.reshape(PEER_HEADS, n_tiles, rows, t_all)
    ea4 = ea_pk.reshape(PEER_HEADS, n_tiles, rows, t_all)
    rb4 = rb.reshape(PEER_HEADS, PEER_NKEYS // 16, 16, t_all)
    eb4 = eb.reshape(PEER_HEADS, PEER_NKEYS // 16, 16, t_all)
    return pl.pallas_call(
        functools.partial(_peer_dense_kernel, n_steps=n_steps, n_tiles=n_tiles),
        grid=(n_steps + 2,),
        in_specs=[
            pl.BlockSpec((d, tm), lambda s: (0, tile_a(s)[0])),
            pl.BlockSpec((te, d), lambda s: (tile_a(s)[1], 0)),
            pl.BlockSpec((d, te), lambda s: (0, tile_c(s)[1])),
            pl.BlockSpec((PEER_HEADS, None, rows, tm), lambda s: (0, tile_b(s)[1], 0, tile_b(s)[0])),
            pl.BlockSpec((PEER_HEADS, None, rows, tm), lambda s: (0, tile_b(s)[1], 0, tile_b(s)[0])),
            pl.BlockSpec((PEER_HEADS, PEER_NKEYS // 16, 16, tm), lambda s: (0, 0, 0, tile_b(s)[0])),
            pl.BlockSpec((PEER_HEADS, PEER_NKEYS // 16, 16, tm), lambda s: (0, 0, 0, tile_b(s)[0])),
        ],
        out_specs=pl.BlockSpec((d, tm), lambda s: (0, tile_c(s)[0])),
        out_shape=jax.ShapeDtypeStruct((d, t_all), F32),
        scratch_shapes=[
            pltpu.VMEM((te, tm), F32),
            pltpu.VMEM((te, tm), F32),
            pltpu.VMEM((te, tm), BF16),
            pltpu.VMEM((te, tm), BF16),
        ],
        compiler_params=_cparams(("arbitrary",)),
        name="peer_dense",
    )(hb, u_bf, vt_bf, l4, ea4, rb4, eb4)


def _peer_out_kernel(ft_ref, x1_ref, gf_ref, lg_ref, lb_ref, o_ref, *, tiles_per_seq, nb):
    i = pl.program_id(0)
    r = jnp.minimum(i // tiles_per_seq, nb)
    y = DEEPNORM_ALPHA * x1_ref[...] + gf_ref[pl.ds(r, 1), :] * jnp.transpose(ft_ref[...])
    o_ref[...] = _layer_norm_rows(y, lg_ref[...], lb_ref[...])


def _peer_out_call(ffn_t, x1, mod_l, ln_g, ln_b, *, seq, nb):
    t_all, d = x1.shape
    tm = TM_OUT
    kern = functools.partial(_peer_out_kernel, tiles_per_seq=seq // tm, nb=nb)
    return pl.pallas_call(
        kern,
        grid=(t_all // tm,),
        in_specs=[
            pl.BlockSpec((d, tm), lambda i: (0, i)),
            pl.BlockSpec((tm, d), lambda i: (i, 0)),
            pl.BlockSpec((8, d), lambda i: (0, 5)),
            pl.BlockSpec((1, d), lambda i: (0, 0)),
            pl.BlockSpec((1, d), lambda i: (0, 0)),
        ],
        out_specs=pl.BlockSpec((tm, d), lambda i: (i, 0)),
        out_shape=jax.ShapeDtypeStruct((t_all, d), F32),
        compiler_params=_cparams(("arbitrary",)),
        name="peer_out_ln",
    )(ffn_t, x1, mod_l, ln_g, ln_b)


def _ab_columns(w_tail):
    depth, d, _ = w_tail.shape
    hb = GDN_HB
    n_hg = N_HEADS // hb
    wa = w_tail[:, :, :2 * N_HEADS].reshape(depth, d, 2, n_hg, hb)
    wb = w_tail[:, :, 2 * N_HEADS:].reshape(depth, d, 2, n_hg, hb)

    def lay(w):
        w = w.transpose(0, 1, 3, 2, 4).reshape(depth, d, n_hg, 2 * hb)
        w = jnp.pad(w, ((0, 0), (0, 0), (0, 0), (0, HEAD_DIM - 2 * hb)))
        return w.reshape(depth, d, n_hg * HEAD_DIM)

    return jnp.concatenate([lay(wa), lay(wb)], axis=-1)


def _head_param_lanes(p):
    depth = p.shape[0]
    hb = GDN_HB
    n_hg = N_HEADS // hb
    p = p.reshape(depth, 2, n_hg, hb).transpose(0, 2, 1, 3).reshape(depth, n_hg, 2 * hb)
    p = jnp.pad(p, ((0, 0), (0, 0), (0, HEAD_DIM - 2 * hb)))
    return p.reshape(depth, 1, n_hg * HEAD_DIM)


def kernel(x, c, ctx, c_ctx, w_mod, b_mod, w_in, w_out, na_rpb, gdn_conv, gdn_a_log, gdn_dt_bias, gdn_norm_g,
           peer_wq, peer_subkeys, peer_u, peer_v, ln_g, ln_b):
    nb, seq, d = x.shape
    ctx_len = ctx.shape[1]
    depth = w_mod.shape[0]
    assert d == D_MODEL and nb + 1 <= 8
    assert seq % (NA_QROWS * GRID_W) == 0 and seq % TM_PROJ == 0
    t_lat = nb * seq
    dims = dict(nb=nb, seq=seq, ctx_len=ctx_len)

    xa = jnp.concatenate([x.reshape(t_lat, d), ctx.reshape(nb * ctx_len, d)], axis=0)
    cs = jnp.concatenate([c, c_ctx[None, :], jnp.zeros((8 - nb - 1, d), F32)], axis=0)
    mod = _mod_call(cs, w_mod, b_mod)

    w_main = w_in[:, :, :P_MAIN_W].astype(BF16)
    w_ab = _ab_columns(w_in[:, :, P_MAIN_W:]).astype(BF16)
    w_out_bf = w_out.astype(BF16)
    alog = _head_param_lanes(gdn_a_log)
    dtb = _head_param_lanes(gdn_dt_bias)
    conv_w = jnp.pad(gdn_conv, ((0, 0), (0, 8 - GDN_CONV), (0, 0)))
    rope_c, rope_s = _rope_tables(seq, ctx_len)
    wq_t = peer_wq.astype(BF16).transpose(0, 2, 1)
    u_bf = peer_u.astype(BF16)
    vt_bf = peer_v.astype(BF16).transpose(0, 2, 1)

    for l in range(depth):
        p_all, pab = _inproj_call(xa, mod[l], w_main[l], w_ab[l], seq=seq, nb=nb)
        bias = _na_bias(na_rpb[l], seq // GRID_W)
        na_lat = _na_call(p_all, bias, **dims)
        na_ctx = _ctx_attn_call(p_all, **dims)
        qn, kn, vn, g, beta = _gdn_prep_call(p_all, pab, conv_w[l], alog[l], dtb[l], rope_c, rope_s, **dims)
        o_f, o_b = _gdn_scan_call(qn, kn, vn, g, beta, **dims)
        x1, h2t = _outproj_call(na_lat, na_ctx, o_f, o_b, p_all, xa, mod[l], w_out_bf[l], gdn_norm_g[l][None, :],
                                ln_g[l, 0][None, :], ln_b[l, 0][None, :], seq=seq, nb=nb,
                                rows=xa.shape[0] if l < depth - 1 else t_lat)
        l_pk, ea_pk, rb, eb = _peer_score_call(h2t, wq_t[l], peer_subkeys[l])
        ffn_t = _peer_dense_call(h2t, u_bf[l], vt_bf[l], l_pk, ea_pk, rb, eb)
        xa = _peer_out_call(ffn_t, x1, mod[l], ln_g[l, 1][None, :], ln_b[l, 1][None, :], seq=seq, nb=nb)
    return xa.reshape(nb, seq, d)
```

```python
import functools

import numpy as np
import jax
import jax.numpy as jnp
from jax import lax
from jax.experimental import pallas as pl
from jax.experimental.pallas import tpu as pltpu

F32 = jnp.float32
BF16 = jnp.bfloat16
HIGHEST = lax.Precision.HIGHEST

D_MODEL = 2048
HEAD_DIM = 128
N_HEADS = 8
GROUP_W = N_HEADS * HEAD_DIM
GRID_W = 64
NA_KR = 8
NA_KC = 16
NA_QROWS = 8
NA_KROWS = 16
NA_SUB = 2
GDN_CHUNK = 64
GDN_CONV = 5
GDN_HB = 8
ROPE_THETA = 10000.0
PEER_HEADS = 8
PEER_TOPK = 16
PEER_NKEYS = 128
PEER_QDIM = 256
N_MOD = 6
LN_EPS = 1e-6
NEG_INF = -1e30
DEPTH_FOR_DEEPNORM = 4
DEEPNORM_ALPHA = (2 * DEPTH_FOR_DEEPNORM) ** 0.25
P_MAIN_W = 7 * GROUP_W
VMEM_LIMIT = 56 * 1024 * 1024

TM_PROJ = 512
TN_PROJ = 1792
TM_OUT = 256
TM_PREP = 256
TM_PEER = 512
TE_PEER = 512


def _cparams(sem):
    return pltpu.CompilerParams(dimension_semantics=sem, vmem_limit_bytes=VMEM_LIMIT)


def _sigmoid(x):
    return 1.0 / (1.0 + jnp.exp(-x))


def _silu(x):
    return x * _sigmoid(x)


def _dot(a, b):
    return jnp.dot(a, b, preferred_element_type=F32)


def _dot_t(a, b):
    return lax.dot_general(a, b, (((1,), (1,)), ((), ())), preferred_element_type=F32)


def _layer_norm_rows(y, g, b):
    mu = jnp.mean(y, axis=-1, keepdims=True)
    yc = y - mu
    var = jnp.mean(yc * yc, axis=-1, keepdims=True)
    return yc * lax.rsqrt(var + LN_EPS) * g + b


def _mod_kernel(c_ref, w_ref, b_ref, o_ref):
    s = _silu(c_ref[...])
    o_ref[0] = jnp.dot(s, w_ref[0], precision=HIGHEST, preferred_element_type=F32) + b_ref[0]


def _mod_call(cs, w_mod, b_mod):
    depth, d, n = w_mod.shape
    tn = 1024
    return pl.pallas_call(
        _mod_kernel,
        grid=(depth, n // tn),
        in_specs=[
            pl.BlockSpec((8, d), lambda l, j: (0, 0)),
            pl.BlockSpec((1, d, tn), lambda l, j: (l, 0, j)),
            pl.BlockSpec((1, 1, tn), lambda l, j: (l, 0, j)),
        ],
        out_specs=pl.BlockSpec((1, 8, tn), lambda l, j: (l, 0, j)),
        out_shape=jax.ShapeDtypeStruct((depth, 8, n), F32),
        compiler_params=_cparams(("arbitrary", "arbitrary")),
        name="adaln_mod",
    )(cs, w_mod, b_mod.reshape(depth, 1, n))


def _inproj_kernel(x_ref, sh_ref, sc_ref, w_ref, wab_ref, p_ref, pab_ref, h_scr, *, tiles_per_seq, nb):
    i = pl.program_id(0)
    j = pl.program_id(1)

    @pl.when(j == 0)
    def _():
        r = jnp.minimum(i // tiles_per_seq, nb)
        sh = sh_ref[pl.ds(r, 1), :]
        sc = sc_ref[pl.ds(r, 1), :]
        hb = (x_ref[...] * (1.0 + sc) + sh).astype(BF16)
        h_scr[...] = hb
        pab_ref[...] = _dot(hb, wab_ref[...])

    p_ref[...] = _dot(h_scr[...], w_ref[...])


def _inproj_call(xa, mod_l, w_main, w_ab, *, seq, nb):
    t_all, d = xa.shape
    n = w_main.shape[1]
    nab = w_ab.shape[1]
    tm, tn = TM_PROJ, TN_PROJ
    kern = functools.partial(_inproj_kernel, tiles_per_seq=seq // tm, nb=nb)
    return pl.pallas_call(
        kern,
        grid=(t_all // tm, n // tn),
        in_specs=[
            pl.BlockSpec((tm, d), lambda i, j: (i, 0)),
            pl.BlockSpec((8, d), lambda i, j: (0, 0)),
            pl.BlockSpec((8, d), lambda i, j: (0, 1)),
            pl.BlockSpec((d, tn), lambda i, j: (0, j)),
            pl.BlockSpec((d, nab), lambda i, j: (0, 0)),
        ],
        out_specs=[
            pl.BlockSpec((tm, tn), lambda i, j: (i, j)),
            pl.BlockSpec((tm, nab), lambda i, j: (i, 0)),
        ],
        out_shape=[
            jax.ShapeDtypeStruct((t_all, n), F32),
            jax.ShapeDtypeStruct((t_all, nab), F32),
        ],
        scratch_shapes=[pltpu.VMEM((tm, d), BF16)],
        compiler_params=_cparams(("arbitrary", "arbitrary")),
        name="in_proj",
    )(xa, mod_l, mod_l, w_main, w_ab)


def _na_bias_index_tables(nrows):
    rq, rk = NA_QROWS, NA_KROWS
    big = 1 << 20
    cfg = [(0, 0, nrows), (8, 4, big), (nrows - rq, nrows - rk, nrows)]
    dr = np.zeros((3, rq, rk), np.int32)
    rv = np.zeros((3, rq, rk), bool)
    for v, (r0, start, nr) in enumerate(cfg):
        r = r0 + np.arange(rq)[:, None]
        kr = start + np.arange(rk)[None, :]
        rs = np.clip(r - NA_KR // 2, 0, nr - NA_KR)
        rv[v] = (kr >= rs) & (kr < rs + NA_KR)
        dr[v] = np.clip(kr - r + NA_KR - 1, 0, 2 * NA_KR - 2)
    qc = np.arange(GRID_W)[:, None]
    kc = np.arange(GRID_W)[None, :]
    ws = np.clip(qc - NA_KC // 2, 0, GRID_W - NA_KC)
    cv = (kc >= ws) & (kc < ws + NA_KC)
    dc = np.clip(kc - qc + NA_KC - 1, 0, 2 * NA_KC - 2).astype(np.int32)
    return dr, rv, dc, cv


def _na_bias(rpb_l, nrows):
    dr, rv, dc, cv = _na_bias_index_tables(nrows)
    tc = jnp.where(cv[None, None], rpb_l[:, :, dc], NEG_INF)
    n_dr = tc.shape[1]
    tc = jnp.concatenate([tc, jnp.full((N_HEADS, 1, GRID_W, GRID_W), NEG_INF, F32)], axis=1)
    blk = tc[:, np.where(rv, dr, n_dr)]
    blk = blk.transpose(0, 1, 2, 4, 3, 5)
    return blk.reshape(N_HEADS, 3, NA_QROWS * GRID_W, NA_KROWS * GRID_W)


def _na_kernel(q_ref, k0, k1, k2, k3, v0, v1, v2, v3, kc_ref, vc_ref, bias_ref, o_ref):
    kb = 4 * GRID_W
    ks = [kr[...].astype(BF16) for kr in (k0, k1, k2, k3, kc_ref)]
    vs = [vr[...].astype(BF16) for vr in (v0, v1, v2, v3, vc_ref)]
    rows = q_ref.shape[0] // NA_SUB
    for sub in range(NA_SUB):
        rs = slice(sub * rows, (sub + 1) * rows)
        q = (q_ref[rs, :] * (HEAD_DIM ** -0.5)).astype(BF16)
        s = [_dot_t(q, ks[j]) + bias_ref[rs, j * kb:(j + 1) * kb] for j in range(4)]
        s.append(_dot_t(q, ks[4]))
        m = s[0].max(axis=-1, keepdims=True)
        for t in s[1:]:
            m = jnp.maximum(m, t.max(axis=-1, keepdims=True))
        p = [jnp.exp(t - m) for t in s]
        l = p[0].sum(axis=-1, keepdims=True)
        for t in p[1:]:
            l = l + t.sum(axis=-1, keepdims=True)
        o = _dot(p[0].astype(BF16), vs[0])
        for t, vv in zip(p[1:], vs[1:]):
            o = o + _dot(t.astype(BF16), vv)
        o_ref[rs, :] = o / l


def _na_call(p_all, bias, *, nb, seq, ctx_len):
    t_all = p_all.shape[0]
    nrows = seq // GRID_W
    nrb = nrows // NA_QROWS
    tq = NA_QROWS * GRID_W
    tk = 4 * GRID_W
    assert ctx_len == tk and nrows >= NA_KROWS
    kblocks_per_seq = seq // tk
    max_sb = kblocks_per_seq - 4
    ctx_blk0 = nb * seq // tk

    def sb(r):
        return jnp.clip(2 * r - 1, 0, max_sb)

    def kspec(j, colbase):
        return pl.BlockSpec((tk, HEAD_DIM), lambda h, b, r: (b * kblocks_per_seq + sb(r) + j, colbase + h))

    def variant(r):
        return jnp.where(r == 0, 0, jnp.where(r == nrb - 1, 2, 1))

    in_specs = [pl.BlockSpec((tq, HEAD_DIM), lambda h, b, r: (b * nrb + r, h))]
    in_specs += [kspec(j, N_HEADS) for j in range(4)]
    in_specs += [kspec(j, 2 * N_HEADS) for j in range(4)]
    in_specs += [
        pl.BlockSpec((tk, HEAD_DIM), lambda h, b, r: (ctx_blk0 + b, N_HEADS + h)),
        pl.BlockSpec((tk, HEAD_DIM), lambda h, b, r: (ctx_blk0 + b, 2 * N_HEADS + h)),
        pl.BlockSpec((None, None, tq, NA_KROWS * GRID_W), lambda h, b, r: (h, variant(r), 0, 0)),
    ]
    return pl.pallas_call(
        _na_kernel,
        grid=(N_HEADS, nb, nrb),
        in_specs=in_specs,
        out_specs=pl.BlockSpec((tq, HEAD_DIM), lambda h, b, r: (b * nrb + r, h)),
        out_shape=jax.ShapeDtypeStruct((nb * seq, GROUP_W), F32),
        compiler_params=_cparams(("arbitrary", "arbitrary", "arbitrary")),
        name="na_attention",
    )(*([p_all] * 11), bias)


def _ctx_attn_kernel(q_ref, k_ref, v_ref, o_ref):
    q = q_ref[...].astype(BF16)
    s = _dot_t(q, k_ref[...].astype(BF16)) * (HEAD_DIM ** -0.5)
    m = s.max(axis=-1, keepdims=True)
    p = jnp.exp(s - m)
    l = p.sum(axis=-1, keepdims=True)
    o_ref[...] = _dot(p.astype(BF16), v_ref[...].astype(BF16)) / l


def _ctx_attn_call(p_all, *, nb, seq, ctx_len):
    blk0 = nb * seq // ctx_len
    return pl.pallas_call(
        _ctx_attn_kernel,
        grid=(nb, N_HEADS),
        in_specs=[
            pl.BlockSpec((ctx_len, HEAD_DIM), lambda b, h: (blk0 + b, h)),
            pl.BlockSpec((ctx_len, HEAD_DIM), lambda b, h: (blk0 + b, N_HEADS + h)),
            pl.BlockSpec((ctx_len, HEAD_DIM), lambda b, h: (blk0 + b, 2 * N_HEADS + h)),
        ],
        out_specs=pl.BlockSpec((ctx_len, HEAD_DIM), lambda b, h: (b, h)),
        out_shape=jax.ShapeDtypeStruct((nb * ctx_len, GROUP_W), F32),
        compiler_params=_cparams(("arbitrary", "arbitrary")),
        name="ctx_attention",
    )(p_all, p_all, p_all)


def _rope_tables(seq, ctx_len):
    t = np.arange(seq)
    row = (t // GRID_W).astype(np.float32)
    col = (t % GRID_W).astype(np.float32)
    n_freq = HEAD_DIM // 4
    inv = (ROPE_THETA ** (-np.arange(n_freq, dtype=np.float32) / n_freq)).astype(np.float32)
    ang = jnp.stack([jnp.asarray(row)[:, None] * inv, jnp.asarray(col)[:, None] * inv], axis=1)
    cos, sin = jnp.cos(ang), jnp.sin(ang)
    c = jnp.concatenate([cos, cos], axis=-1).reshape(seq, HEAD_DIM)
    s = jnp.concatenate([-sin, sin], axis=-1).reshape(seq, HEAD_DIM)
    c = jnp.concatenate([c, jnp.ones((ctx_len, HEAD_DIM), F32)], axis=0)
    s = jnp.concatenate([s, jnp.zeros((ctx_len, HEAD_DIM), F32)], axis=0)
    return c, s


def _gdn_prep_kernel(cur_ref, prev_ref, next_ref, pab_ref, cw_ref, alog_ref, dtb_ref, rc_ref, rs_ref,
                     q_ref, k_ref, v_ref, g_ref, b_ref, ext_ref, *, n_lat_tiles, tps, nab_half):
    i = pl.program_id(0)
    tm = TM_PREP
    is_ctx = i >= n_lat_tiles
    first = jnp.logical_or(is_ctx, i % tps == 0)
    last = jnp.logical_or(is_ctx, i % tps == tps - 1)
    ext_ref[8:8 + tm, :] = cur_ref[...]
    ext_ref[0:8, :] = jnp.where(first, 0.0, prev_ref[...])
    ext_ref[8 + tm:16 + tm, :] = jnp.where(last, 0.0, next_ref[...])

    lane = lax.broadcasted_iota(jnp.int32, (tm, HEAD_DIM), 1)
    half0 = (lane % (HEAD_DIM // 2)) < (HEAD_DIM // 4)
    rc = rc_ref[...]
    rs = rs_ref[...]
    base = 8 - GDN_CONV // 2
    outs = (q_ref, k_ref, v_ref)
    for part in range(3):
        for h in range(N_HEADS):
            c0 = part * GROUP_W + h * HEAD_DIM
            acc = cw_ref[0:1, c0:c0 + HEAD_DIM] * ext_ref[base:base + tm, c0:c0 + HEAD_DIM]
            for t in range(1, GDN_CONV):
                acc = acc + cw_ref[t:t + 1, c0:c0 + HEAD_DIM] * ext_ref[base + t:base + t + tm, c0:c0 + HEAD_DIM]
            y = _silu(acc)
            if part < 2:
                y = y * lax.rsqrt(jnp.sum(y * y, axis=-1, keepdims=True) + 1e-6)
                partner = jnp.where(half0, pltpu.roll(y, HEAD_DIM - HEAD_DIM // 4, 1),
                                    pltpu.roll(y, HEAD_DIM // 4, 1))
                y = y * rc + partner * rs
                if part == 0:
                    y = y * (HEAD_DIM ** -0.5)
            outs[part][:, h * HEAD_DIM:(h + 1) * HEAD_DIM] = y

    a = pab_ref[:, 0:nab_half] + dtb_ref[...]
    softplus = jnp.maximum(a, 0.0) + jnp.log1p(jnp.exp(-jnp.abs(a)))
    g_ref[...] = -jnp.exp(alog_ref[...]) * softplus
    b_ref[...] = _sigmoid(pab_ref[:, nab_half:2 * nab_half])


def _gdn_prep_call(p_all, pab, conv_w, alog, dtb, rope_c, rope_s, *, nb, seq, ctx_len):
    t_all = p_all.shape[0]
    tm = TM_PREP
    assert ctx_len == tm
    n_tiles = t_all // tm
    n_lat_tiles = nb * seq // tm
    tps = seq // tm
    nab_half = pab.shape[1] // 2
    qkv_w = 3 * GROUP_W
    last8 = t_all // 8 - 1
    kern = functools.partial(_gdn_prep_kernel, n_lat_tiles=n_lat_tiles, tps=tps, nab_half=nab_half)

    def rope_idx(i):
        return jnp.where(i < n_lat_tiles, i % tps, tps)

    return pl.pallas_call(
        kern,
        grid=(n_tiles,),
        in_specs=[
            pl.BlockSpec((tm, qkv_w), lambda i: (i, 1)),
            pl.BlockSpec((8, qkv_w), lambda i: (jnp.maximum(i * (tm // 8) - 1, 0), 1)),
            pl.BlockSpec((8, qkv_w), lambda i: (jnp.minimum((i + 1) * (tm // 8), last8), 1)),
            pl.BlockSpec((tm, 2 * nab_half), lambda i: (i, 0)),
            pl.BlockSpec((8, qkv_w), lambda i: (0, 0)),
            pl.BlockSpec((1, nab_half), lambda i: (0, 0)),
            pl.BlockSpec((1, nab_half), lambda i: (0, 0)),
            pl.BlockSpec((tm, HEAD_DIM), lambda i: (rope_idx(i), 0)),
            pl.BlockSpec((tm, HEAD_DIM), lambda i: (rope_idx(i), 0)),
        ],
        out_specs=[
            pl.BlockSpec((tm, GROUP_W), lambda i: (i, 0)),
            pl.BlockSpec((tm, GROUP_W), lambda i: (i, 0)),
            pl.BlockSpec((tm, GROUP_W), lambda i: (i, 0)),
            pl.BlockSpec((tm, nab_half), lambda i: (i, 0)),
            pl.BlockSpec((tm, nab_half), lambda i: (i, 0)),
        ],
        out_shape=[
            jax.ShapeDtypeStruct((t_all, GROUP_W), F32),
            jax.ShapeDtypeStruct((t_all, GROUP_W), F32),
            jax.ShapeDtypeStruct((t_all, GROUP_W), F32),
            jax.ShapeDtypeStruct((t_all, nab_half), F32),
            jax.ShapeDtypeStruct((t_all, nab_half), F32),
        ],
        scratch_shapes=[pltpu.VMEM((tm + 16, qkv_w), F32)],
        compiler_params=_cparams(("arbitrary",)),
        name="gdn_prep",
    )(p_all, p_all, p_all, pab, conv_w, alog, dtb, rope_c, rope_s)


def _unit_tri_inverse(ns, orders):
    c = GDN_CHUNK
    ds = []
    for n, (row, col) in zip(ns, orders):
        m1 = jnp.logical_and(jnp.logical_and((row >> 1) == (col >> 1), (row & 1) == 1), (col & 1) == 0)
        ds.append((row == col).astype(F32) - jnp.where(m1, n, 0.0))
    k = 2
    while k < c:
        sh = int(np.log2(2 * k))
        dsp, xs = [], []
        for n, d, (row, col) in zip(ns, ds, orders):
            mk = jnp.logical_and((row >> sh) == (col >> sh),
                                 jnp.logical_and((row & (2 * k - 1)) >= k, (col & (2 * k - 1)) < k))
            lk = jnp.where(mk, n, 0.0).astype(BF16)
            dh = d.astype(BF16)
            dsp.append(dh)
            xs.append(_dot(lk, dh))
        ds = [d - _dot(dh, x.astype(BF16)) for d, dh, x in zip(ds, dsp, xs)]
        k *= 2
    return ds


def _gdn_scan_kernel(qf, kf, vf, gf, bf, qb, kb, vb, gb, bb, of_ref, ob_ref, st_ref, *, hb):
    s = pl.program_id(2)
    c = GDN_CHUNK

    @pl.when(s == 0)
    def _():
        st_ref[...] = jnp.zeros(st_ref.shape, F32)

    ri = lax.broadcasted_iota(jnp.int32, (c, c), 0)
    ci = lax.broadcasted_iota(jnp.int32, (c, c), 1)
    zpad = jnp.zeros((HEAD_DIM - c, HEAD_DIM), F32)
    ch = []
    for d, (q_ref, k_ref, v_ref, g_ref, be_ref, o_ref) in enumerate(
            ((qf, kf, vf, gf, bf, of_ref), (qb, kb, vb, gb, bb, ob_ref))):
        incl = (ri >= ci) if d == 0 else (ri <= ci)
        gc_all = jnp.dot(incl.astype(F32), g_ref[...], precision=HIGHEST, preferred_element_type=F32)
        gc_t = jnp.transpose(jnp.concatenate([gc_all, zpad], axis=0))
        beta_all = be_ref[...]
        for hh in range(hb):
            ln = d * hb + hh
            sl = slice(hh * HEAD_DIM, (hh + 1) * HEAD_DIM)
            ch.append(dict(
                ln=ln, sl=sl, o_ref=o_ref, incl=incl,
                strict=(ri > ci) if d == 0 else (ri < ci),
                order=(ri, ci) if d == 0 else (ci, ri),
                last=c - 1 if d == 0 else 0,
                gcc=gc_all[:, ln:ln + 1], grow=gc_t[ln:ln + 1, 0:c], beta=beta_all[:, ln:ln + 1],
                q=q_ref[:, sl], k=k_ref[:, sl], v=v_ref[:, sl]))
    for t in ch:
        t["decay"] = jnp.where(t["incl"], jnp.exp(jnp.where(t["incl"], t["gcc"] - t["grow"], 0.0)), 0.0)
        t["kbeta"] = t["k"] * t["beta"]
        t["k16"] = t["k"].astype(BF16)
    for t in ch:
        t["kk"] = _dot_t(t["kbeta"].astype(BF16), t["k16"])
    for t in ch:
        t["qk"] = _dot_t(t["q"].astype(BF16), t["k16"]) * t["decay"]
    ns = [jnp.where(t["strict"], t["kk"] * t["decay"], 0.0) for t in ch]
    tinvs = _unit_tri_inverse(ns, [t["order"] for t in ch])
    for t, tinv in zip(ch, tinvs):
        eg = jnp.exp(t["gcc"])
        t["eg"] = eg
        rhs = jnp.concatenate([t["v"] * t["beta"], t["kbeta"] * eg], axis=1)
        t["sol"] = _dot(tinv.astype(BF16), rhs.astype(BF16))
    for t in ch:
        t["state"] = st_ref[t["ln"]]
        t["s16"] = t["state"].astype(BF16)
        t["v_new"] = t["sol"][:, :HEAD_DIM] - _dot(t["sol"][:, HEAD_DIM:].astype(BF16), t["s16"])
        t["v16"] = t["v_new"].astype(BF16)
    for t in ch:
        t["o_ref"][:, t["sl"]] = (_dot((t["q"] * t["eg"]).astype(BF16), t["s16"])
                                  + _dot(t["qk"].astype(BF16), t["v16"]))
    for t in ch:
        glast = t["gcc"][t["last"]:t["last"] + 1, :]
        kd_t = jnp.transpose(jnp.concatenate([t["k"] * jnp.exp(glast - t["gcc"]), zpad], axis=0))[:, 0:c]
        st_ref[t["ln"]] = t["state"] * jnp.exp(glast) + _dot(kd_t.astype(BF16), t["v16"])


def _gdn_scan_call(qn, kn, vn, g, beta, *, nb, seq, ctx_len):
    t_all = qn.shape[0]
    hb = GDN_HB
    c = GDN_CHUNK
    ncc = ctx_len // c
    ncl = seq // c
    ctx0 = nb * ncl
    n_hg = N_HEADS // hb

    def row_f(b, s):
        return jnp.where(s < ncc, ctx0 + b * ncc + s, b * ncl + (s - ncc))

    def row_b(b, s):
        return jnp.where(s < ncc, ctx0 + b * ncc + (ncc - 1 - s), b * ncl + (ncl - 1 - (s - ncc)))

    def specs(rowfn):
        big = pl.BlockSpec((c, hb * HEAD_DIM), lambda b, hg, s: (rowfn(b, s), hg))
        small = pl.BlockSpec((c, HEAD_DIM), lambda b, hg, s: (rowfn(b, s), hg))
        return [big, big, big, small, small]

    out_f = pl.BlockSpec((c, hb * HEAD_DIM), lambda b, hg, s: (row_f(b, s), hg))
    out_b = pl.BlockSpec((c, hb * HEAD_DIM), lambda b, hg, s: (row_b(b, s), hg))
    return pl.pallas_call(
        functools.partial(_gdn_scan_kernel, hb=hb),
        grid=(nb, n_hg, ncc + ncl),
        in_specs=specs(row_f) + specs(row_b),
        out_specs=[out_f, out_b],
        out_shape=[jax.ShapeDtypeStruct((t_all, GROUP_W), F32)] * 2,
        scratch_shapes=[pltpu.VMEM((2 * hb, HEAD_DIM, HEAD_DIM), F32)],
        compiler_params=_cparams(("arbitrary", "arbitrary", "arbitrary")),
        name="gdn_scan",
    )(qn, kn, vn, g, beta, qn, kn, vn, g, beta)


def _outproj_kernel(nal_ref, nac_ref, of_ref, ob_ref, z_ref, x_ref, ga_ref, shf_ref, scf_ref, w_ref, ng_ref, lg_ref,
                    lb_ref, x1_ref, h2t_ref, gdn_scr, *, tiles_per_seq, nb):
    i = pl.program_id(0)
    r = jnp.minimum(i // tiles_per_seq, nb)
    na = jnp.where(i < nb * tiles_per_seq, nal_ref[...], nac_ref[...])
    o = of_ref[...] + ob_ref[...]
    ng = ng_ref[...]
    for h in range(N_HEADS):
        sl = slice(h * HEAD_DIM, (h + 1) * HEAD_DIM)
        oh = o[:, sl]
        oh = oh * lax.rsqrt(jnp.mean(oh * oh, axis=-1, keepdims=True) + 1e-6) * ng
        gdn_scr[:, sl] = (oh * _silu(z_ref[:, sl])).astype(BF16)
    mix = _dot(na.astype(BF16), w_ref[0:GROUP_W, :]) + _dot(gdn_scr[...], w_ref[GROUP_W:2 * GROUP_W, :])
    y = DEEPNORM_ALPHA * x_ref[...] + ga_ref[pl.ds(r, 1), :] * mix
    x1 = _layer_norm_rows(y, lg_ref[...], lb_ref[...])
    x1_ref[...] = x1
    h2 = x1 * (1.0 + scf_ref[pl.ds(r, 1), :]) + shf_ref[pl.ds(r, 1), :]
    h2t_ref[...] = jnp.transpose(h2).astype(BF16)


def _outproj_call(na_lat, na_ctx, o_f, o_b, p_all, xa, mod_l, w_out, norm_g, ln_g, ln_b, *, seq, nb, rows):
    t_all, d = rows, xa.shape[1]
    tm = TM_OUT
    n_lat_tiles = nb * seq // tm
    kern = functools.partial(_outproj_kernel, tiles_per_seq=seq // tm, nb=nb)
    row = lambda i: (i, 0)
    const = lambda i: (0, 0)
    return pl.pallas_call(
        kern,
        grid=(t_all // tm,),
        in_specs=[
            pl.BlockSpec((tm, GROUP_W), lambda i: (jnp.minimum(i, n_lat_tiles - 1), 0)),
            pl.BlockSpec((tm, GROUP_W), lambda i: (jnp.maximum(i - n_lat_tiles, 0), 0)),
            pl.BlockSpec((tm, GROUP_W), row),
            pl.BlockSpec((tm, GROUP_W), row),
            pl.BlockSpec((tm, GROUP_W), lambda i: (i, 6)),
            pl.BlockSpec((tm, d), row),
            pl.BlockSpec((8, d), lambda i: (0, 2)),
            pl.BlockSpec((8, d), lambda i: (0, 3)),
            pl.BlockSpec((8, d), lambda i: (0, 4)),
            pl.BlockSpec((d, d), const),
            pl.BlockSpec((1, HEAD_DIM), const),
            pl.BlockSpec((1, d), const),
            pl.BlockSpec((1, d), const),
        ],
        out_specs=[
            pl.BlockSpec((tm, d), row),
            pl.BlockSpec((d, tm), lambda i: (0, i)),
        ],
        out_shape=[
            jax.ShapeDtypeStruct((t_all, d), F32),
            jax.ShapeDtypeStruct((d, t_all), BF16),
        ],
        scratch_shapes=[pltpu.VMEM((tm, GROUP_W), BF16)],
        compiler_params=_cparams(("arbitrary",)),
        name="out_proj_ln",
    )(na_lat, na_ctx, o_f, o_b, p_all, xa, mod_l, mod_l, mod_l, w_out, norm_g, ln_g, ln_b)


def _dup_bf16_words(x):
    u = pltpu.bitcast(x.astype(BF16).astype(F32), jnp.uint32)
    return u | (u >> 16)


def _peer_score_kernel(ht_ref, w_ref, sk_ref, l_ref, ea_ref, rb_ref, eb_ref, s_scr, atop, btop, cand):
    tm = TM_PEER
    qt = _dot(w_ref[...], ht_ref[...])
    half = PEER_QDIM // 2
    s_scr[0] = _dot(sk_ref[0], qt[0:half])
    s_scr[1] = _dot(sk_ref[1], qt[half:2 * half])
    n_half = PEER_TOPK // 2
    for tc in range(tm // 128):
        ls = slice(tc * 128, (tc + 1) * 128)
        s0 = s_scr[0, :, ls]
        s1 = s_scr[1, :, ls]
        sc = s0
        for k in range(PEER_TOPK):
            m = jnp.max(sc, axis=0, keepdims=True)
            atop[k:k + 1, ls] = m
            sc = jnp.where(sc == m, -jnp.inf, sc)
        sc = s1
        rank = jnp.full(s1.shape, float(PEER_TOPK), F32)
        for k in range(PEER_TOPK):
            m = jnp.max(sc, axis=0, keepdims=True)
            btop[k:k + 1, ls] = m
            hit = sc == m
            rank = jnp.minimum(rank, jnp.where(hit, float(k), float(PEER_TOPK)))
            sc = jnp.where(hit, -jnp.inf, sc)
        bt = btop[:, ls]
        cand[0:PEER_TOPK, ls] = atop[0:1, ls] + bt
        for i in range(1, n_half):
            r0 = PEER_TOPK + (i - 1) * n_half
            cand[r0:r0 + n_half, ls] = atop[i:i + 1, ls] + bt[0:n_half, :]
        r0 = PEER_TOPK + (n_half - 1) * n_half
        cand[r0:r0 + n_half, ls] = atop[n_half:PEER_TOPK, ls] + bt[0:1, :]
        cv = cand[:, ls]
        m0 = jnp.max(cv, axis=0, keepdims=True)
        z = jnp.zeros_like(m0)
        tau = m0
        for k in range(PEER_TOPK):
            tau = jnp.max(cv, axis=0, keepdims=True)
            z = z + jnp.exp(tau - m0)
            cv = jnp.where(cv == tau, -jnp.inf, cv)
        row = lambda k: bt[k:k + 1, :]
        ge = lambda v: (s0 + v) >= tau
        c8 = ge(row(7))
        c4 = ge(jnp.where(c8, row(11), row(3)))
        c2 = ge(jnp.where(c8, jnp.where(c4, row(13), row(9)), jnp.where(c4, row(5), row(1))))
        c1 = ge(jnp.where(c8,
                          jnp.where(c4, jnp.where(c2, row(14), row(12)), jnp.where(c2, row(10), row(8))),
                          jnp.where(c4, jnp.where(c2, row(6), row(4)), jnp.where(c2, row(2), row(0)))))
        cnt = (jnp.where(c8, 8.0, 0.0) + jnp.where(c4, 4.0, 0.0) + jnp.where(c2, 2.0, 0.0)
               + jnp.where(c1, 1.0, 0.0) + jnp.where(ge(row(15)), 1.0, 0.0))
        l_ref[0, :, ls] = _dup_bf16_words(cnt)
        ea_ref[0, :, ls] = _dup_bf16_words(jnp.exp(s0 - atop[0:1, ls]) * (1.0 / z))
        rb_ref[0, :, ls] = rank.astype(BF16)
        eb_ref[0, :, ls] = jnp.exp(s1 - bt[0:1, :]).astype(BF16)


def _peer_score_call(h2t, wq_t, subkeys):
    d, t_all = h2t.shape
    tm = TM_PEER
    ncand = PEER_TOPK + (PEER_TOPK // 2) ** 2
    tok = lambda i, h: (h, 0, i)
    return pl.pallas_call(
        _peer_score_kernel,
        grid=(t_all // tm, PEER_HEADS),
        in_specs=[
            pl.BlockSpec((d, tm), lambda i, h: (0, i)),
            pl.BlockSpec((PEER_QDIM, d), lambda i, h: (h, 0)),
            pl.BlockSpec((2, PEER_NKEYS, PEER_QDIM // 2), lambda i, h: (0, 0, 0)),
        ],
        out_specs=[pl.BlockSpec((1, PEER_NKEYS, tm), tok)] * 4,
        out_shape=[
            jax.ShapeDtypeStruct((PEER_HEADS, PEER_NKEYS, t_all), jnp.uint32),
            jax.ShapeDtypeStruct((PEER_HEADS, PEER_NKEYS, t_all), jnp.uint32),
            jax.ShapeDtypeStruct((PEER_HEADS, PEER_NKEYS, t_all), BF16),
            jax.ShapeDtypeStruct((PEER_HEADS, PEER_NKEYS, t_all), BF16),
        ],
        scratch_shapes=[
            pltpu.VMEM((2, PEER_NKEYS, tm), F32),
            pltpu.VMEM((PEER_TOPK, tm), F32),
            pltpu.VMEM((PEER_TOPK, tm), F32),
            pltpu.VMEM((ncand, tm), F32),
        ],
        compiler_params=_cparams(("arbitrary", "arbitrary")),
        name="peer_scores",
    )(h2t, wq_t, subkeys)


def _peer_dense_kernel(h_ref, u_ref, vt_ref, l_ref, ea_ref, rb_ref, eb_ref, o_ref, act0, act1, w0, w1, *,
                       n_steps, n_tiles):
    s = pl.program_id(0)
    m_c = jnp.clip(s - 2, 0, n_steps - 1) % n_tiles
    tm = TM_PEER
    nk = PEER_NKEYS
    rows_per_step = TE_PEER // nk
    jq_rows = 32
    n_jq = nk // jq_rows

    @pl.when(s == 0)
    def _():
        for r in (act0, act1):
            r[...] = jnp.zeros(r.shape, F32)
        for r in (w0, w1):
            r[...] = jnp.zeros(r.shape, BF16)

    @pl.when(m_c == 0)
    def _():
        o_ref[...] = jnp.zeros(o_ref.shape, F32)

    valid_b = jnp.logical_and(s >= 1, s <= n_steps)
    sqrt_half = np.float32(np.sqrt(0.5))

    def stages(act_a, w_c, act_b, w_b):
        n_tc = tm // 128

        def stage_a(n, r):
            cs = slice(n * 256, (n + 1) * 256)
            rs = slice(r * 128, (r + 1) * 128)
            act_a[rs, cs] = _dot(u_ref[rs, :], h_ref[:, cs])

        def stage_c(q):
            rs = slice(q * 128, (q + 1) * 128)
            o_ref[rs, :] += _dot(vt_ref[rs, :], w_c[...])

        def stage_b(tc, jq):
            ls = slice(tc * 128, (tc + 1) * 128)
            js = slice(jq * (jq_rows // 16), (jq + 1) * (jq_rows // 16))
            g = [jnp.zeros((jq_rows // 16, 16, 128), BF16) for _ in range(rows_per_step)]
            for h in range(PEER_HEADS):
                rk = rb_ref[h, js, :, ls]
                ee = eb_ref[h, js, :, ls]
                for ii in range(rows_per_step):
                    l_b = pltpu.bitcast(jnp.broadcast_to(l_ref[h, ii:ii + 1, ls], (8, 128)), BF16)
                    e_b = pltpu.bitcast(jnp.broadcast_to(ea_ref[h, ii:ii + 1, ls], (8, 128)), BF16)
                    g[ii] = g[ii] + jnp.where(rk < l_b, ee * e_b, jnp.zeros((), BF16))
            for ii in range(rows_per_step):
                r0 = ii * nk + jq * jq_rows
                xa = act_b[r0:r0 + jq_rows, ls]
                ge = (0.5 * xa * (1.0 + lax.erf(xa * sqrt_half))).astype(BF16)
                wv = jnp.where(valid_b, ge * g[ii].reshape(jq_rows, 128), jnp.zeros((), BF16))
                w_b[r0:r0 + jq_rows, ls] = wv

        mxu = []
        for q in range(8):
            mxu += [functools.partial(stage_a, q // 4, q % 4), functools.partial(stage_c, 2 * q),
                    functools.partial(stage_c, 2 * q + 1)]
        blocks = [(tc, jq) for tc in range(n_tc) for jq in range(n_jq)]
        mi = 0
        for bi, (tc, jq) in enumerate(blocks):
            while mi < len(mxu) and mi * len(blocks) <= bi * len(mxu):
                mxu[mi]()
                mi += 1
            stage_b(tc, jq)
        while mi < len(mxu):
            mxu[mi]()
            mi += 1

    @pl.when(s % 2 == 0)
    def _():
        stages(act0, w0, act1, w1)

    @pl.when(s % 2 == 1)
    def _():
        stages(act1, w1, act0, w0)


def _peer_dense_call(hb, u_bf, vt_bf, l_pk, ea_pk, rb, eb):
    d, t_all = hb.shape
    ne = u_bf.shape[0]
    tm, te = TM_PEER, TE_PEER
    n_tiles = ne // te
    n_steps = (t_all // tm) * n_tiles
    rows = te // PEER_NKEYS

    def lag(k):
        return lambda s: divmod(jnp.clip(s - k, 0, n_steps - 1), n_tiles)

    tile_a, tile_b, tile_c = lag(0), lag(1), lag(2)
    l4 = l_pk.reshape(PEER_HEADS, n_tiles, rows, t_all)
    ea4 = ea_pk.reshape(PEER_HEADS, n_tiles, rows, t_all)
    rb4 = rb.reshape(PEER_HEADS, PEER_NKEYS // 16, 16, t_all)
    eb4 = eb.reshape(PEER_HEADS, PEER_NKEYS // 16, 16, t_all)
    return pl.pallas_call(
        functools.partial(_peer_dense_kernel, n_steps=n_steps, n_tiles=n_tiles),
        grid=(n_steps + 2,),
        in_specs=[
            pl.BlockSpec((d, tm), lambda s: (0, tile_a(s)[0])),
            pl.BlockSpec((te, d), lambda s: (tile_a(s)[1], 0)),
            pl.BlockSpec((d, te), lambda s: (0, tile_c(s)[1])),
            pl.BlockSpec((PEER_HEADS, None, rows, tm), lambda s: (0, tile_b(s)[1], 0, tile_b(s)[0])),
            pl.BlockSpec((PEER_HEADS, None, rows, tm), lambda s: (0, tile_b(s)[1], 0, tile_b(s)[0])),
            pl.BlockSpec((PEER_HEADS, PEER_NKEYS // 16, 16, tm), lambda s: (0, 0, 0, tile_b(s)[0])),
            pl.BlockSpec((PEER_HEADS, PEER_NKEYS // 16, 16, tm), lambda s: (0, 0, 0, tile_b(s)[0])),
        ],
        out_specs=pl.BlockSpec((d, tm), lambda s: (0, tile_c(s)[0])),
        out_shape=jax.ShapeDtypeStruct((d, t_all), F32),
        scratch_shapes=[
            pltpu.VMEM((te, tm), F32),
            pltpu.VMEM((te, tm), F32),
            pltpu.VMEM((te, tm), BF16),
            pltpu.VMEM((te, tm), BF16),
        ],
        compiler_params=_cparams(("arbitrary",)),
        name="peer_dense",
    )(hb, u_bf, vt_bf, l4, ea4, rb4, eb4)


def _peer_out_kernel(ft_ref, x1_ref, gf_ref, lg_ref, lb_ref, o_ref, *, tiles_per_seq, nb):
    i = pl.program_id(0)
    r = jnp.minimum(i // tiles_per_seq, nb)
    y = DEEPNORM_ALPHA * x1_ref[...] + gf_ref[pl.ds(r, 1), :] * jnp.transpose(ft_ref[...])
    o_ref[...] = _layer_norm_rows(y, lg_ref[...], lb_ref[...])


def _peer_out_call(ffn_t, x1, mod_l, ln_g, ln_b, *, seq, nb):
    t_all, d = x1.shape
    tm = TM_OUT
    kern = functools.partial(_peer_out_kernel, tiles_per_seq=seq // tm, nb=nb)
    return pl.pallas_call(
        kern,
        grid=(t_all // tm,),
        in_specs=[
            pl.BlockSpec((d, tm), lambda i: (0, i)),
            pl.BlockSpec((tm, d), lambda i: (i, 0)),
            pl.BlockSpec((8, d), lambda i: (0, 5)),
            pl.BlockSpec((1, d), lambda i: (0, 0)),
            pl.BlockSpec((1, d), lambda i: (0, 0)),
        ],
        out_specs=pl.BlockSpec((tm, d), lambda i: (i, 0)),
        out_shape=jax.ShapeDtypeStruct((t_all, d), F32),
        compiler_params=_cparams(("arbitrary",)),
        name="peer_out_ln",
    )(ffn_t, x1, mod_l, ln_g, ln_b)


def _ab_columns(w_tail):
    depth, d, _ = w_tail.shape
    hb = GDN_HB
    n_hg = N_HEADS // hb
    wa = w_tail[:, :, :2 * N_HEADS].reshape(depth, d, 2, n_hg, hb)
    wb = w_tail[:, :, 2 * N_HEADS:].reshape(depth, d, 2, n_hg, hb)

    def lay(w):
        w = w.transpose(0, 1, 3, 2, 4).reshape(depth, d, n_hg, 2 * hb)
        w = jnp.pad(w, ((0, 0), (0, 0), (0, 0), (0, HEAD_DIM - 2 * hb)))
        return w.reshape(depth, d, n_hg * HEAD_DIM)

    return jnp.concatenate([lay(wa), lay(wb)], axis=-1)


def _head_param_lanes(p):
    depth = p.shape[0]
    hb = GDN_HB
    n_hg = N_HEADS // hb
    p = p.reshape(depth, 2, n_hg, hb).transpose(0, 2, 1, 3).reshape(depth, n_hg, 2 * hb)
    p = jnp.pad(p, ((0, 0), (0, 0), (0, HEAD_DIM - 2 * hb)))
    return p.reshape(depth, 1, n_hg * HEAD_DIM)


def kernel(x, c, ctx, c_ctx, w_mod, b_mod, w_in, w_out, na_rpb, gdn_conv, gdn_a_log, gdn_dt_bias, gdn_norm_g,
           peer_wq, peer_subkeys, peer_u, peer_v, ln_g, ln_b):
    nb, seq, d = x.shape
    ctx_len = ctx.shape[1]
    depth = w_mod.shape[0]
    assert d == D_MODEL and nb + 1 <= 8
    assert seq % (NA_QROWS * GRID_W) == 0 and seq % TM_PROJ == 0
    t_lat = nb * seq
    dims = dict(nb=nb, seq=seq, ctx_len=ctx_len)

    xa = jnp.concatenate([x.reshape(t_lat, d), ctx.reshape(nb * ctx_len, d)], axis=0)
    cs = jnp.concatenate([c, c_ctx[None, :], jnp.zeros((8 - nb - 1, d), F32)], axis=0)
    mod = _mod_call(cs, w_mod, b_mod)

    w_main = w_in[:, :, :P_MAIN_W].astype(BF16)
    w_ab = _ab_columns(w_in[:, :, P_MAIN_W:]).astype(BF16)
    w_out_bf = w_out.astype(BF16)
    alog = _head_param_lanes(gdn_a_log)
    dtb = _head_param_lanes(gdn_dt_bias)
    conv_w = jnp.pad(gdn_conv, ((0, 0), (0, 8 - GDN_CONV), (0, 0)))
    rope_c, rope_s = _rope_tables(seq, ctx_len)
    wq_t = peer_wq.astype(BF16).transpose(0, 2, 1)
    u_bf = peer_u.astype(BF16)
    vt_bf = peer_v.astype(BF16).transpose(0, 2, 1)

    for l in range(depth):
        p_all, pab = _inproj_call(xa, mod[l], w_main[l], w_ab[l], seq=seq, nb=nb)
        bias = _na_bias(na_rpb[l], seq // GRID_W)
        na_lat = _na_call(p_all, bias, **dims)
        na_ctx = _ctx_attn_call(p_all, **dims)
        qn, kn, vn, g, beta = _gdn_prep_call(p_all, pab, conv_w[l], alog[l], dtb[l], rope_c, rope_s, **dims)
        o_f, o_b = _gdn_scan_call(qn, kn, vn, g, beta, **dims)
        x1, h2t = _outproj_call(na_lat, na_ctx, o_f, o_b, p_all, xa, mod[l], w_out_bf[l], gdn_norm_g[l][None, :],
                                ln_g[l, 0][None, :], ln_b[l, 0][None, :], seq=seq, nb=nb,
                                rows=xa.shape[0] if l < depth - 1 else t_lat)
        l_pk, ea_pk, rb, eb = _peer_score_call(h2t, wq_t[l], peer_subkeys[l])
        ffn_t = _peer_dense_call(h2t, u_bf[l], vt_bf[l], l_pk, ea_pk, rb, eb)
        xa = _peer_out_call(ffn_t, x1, mod[l], ln_g[l, 1][None, :], ln_b[l, 1][None, :], seq=seq, nb=nb)
    return xa.reshape(nb, seq, d)
```
